```python
import jax, jax.numpy as jnp
from jax import lax
import numpy as np

D_MODEL = 1024
BATCH = 8
SEQ = 4096
DEPTH = 2

CHUNK = 64
N_A_LAYERS = DEPTH // 2
N_B_LAYERS = DEPTH - N_A_LAYERS
MIX_WIDTH = D_MODEL
MEM_TOKENS = 256
MEM_HEADS = 4
MEM_WIDTH = D_MODEL // 4
MEM_HEAD_DIM = MEM_WIDTH // MEM_HEADS
A_WIDTH = MIX_WIDTH - MEM_WIDTH
A_HEADS = 4
A_HEAD_DIM = A_WIDTH // A_HEADS
A_CONV = 4
B_HEAD_DIM = 64
B_HEADS = A_WIDTH // B_HEAD_DIM
B_WIDTH = B_HEADS * B_HEAD_DIM
BAND_CHUNKS = 9
BAND = BAND_CHUNKS * CHUNK
KV_PAD = BAND - CHUNK
MAX_REL = 128
REL_SIZE = MAX_REL + CHUNK
D_FF = ((8 * D_MODEL // 3 + 127) // 128) * 128
FFN_CONV = 3
A_IN = 4 * A_WIDTH + 2 * A_HEADS + MEM_WIDTH
B_IN = B_WIDTH + MEM_WIDTH
EPS = 1e-6

kernel_name = "yoco_mlstm_chunkband_memxattn_convffn"


def rmsnorm(x, g):
    x32 = x.astype(jnp.float32)
    y = x32 * lax.rsqrt(jnp.mean(x32 * x32, axis=-1, keepdims=True) + EPS)
    return (y * g.astype(jnp.float32)).astype(x.dtype)


def causal_dwconv(x, w, b):
    k = w.shape[0]
    s = x.shape[1]
    xp = jnp.pad(x, ((0, 0), (k - 1, 0), (0, 0)))
    y = xp[:, 0:s] * w[0]
    for j in range(1, k):
        y = y + xp[:, j:j + s] * w[j]
    return y + b


def mlstm_chunkwise(q, k, v, ig, logf):
    bsz, seq, nh, dh = q.shape
    nc = seq // CHUNK
    f32 = jnp.float32

    def to_chunks(t):
        t = t.astype(f32).reshape((bsz, nc, CHUNK) + t.shape[2:])
        perm = (1, 0, 3, 2) + tuple(range(4, t.ndim))
        return t.transpose(perm)

    qc, kc, vc = to_chunks(q), to_chunks(k * (dh ** -0.5)), to_chunks(v)
    igc, lfc = to_chunks(ig), to_chunks(logf)
    tril = jnp.tril(jnp.ones((CHUNK, CHUNK), dtype=bool))

    def step(carry, inp):
        c_st, n_st, m_st = carry
        qq, kk, vv, ii, lf = inp
        b = jnp.cumsum(lf, axis=-1)
        d = b[..., :, None] - b[..., None, :] + ii[..., None, :]
        d = jnp.where(tril, d, -jnp.inf)
        m_inter = b + m_st[..., None]
        m_t = jnp.maximum(m_inter, jnp.max(d, axis=-1))
        inter = jnp.exp(m_inter - m_t)
        s = jnp.einsum('bhtd,bhsd->bhts', qq, kk) * jnp.exp(d - m_t[..., None])
        num = jnp.einsum('bhts,bhse->bhte', s, vv) + inter[..., None] * jnp.einsum('bhtd,bhde->bhte', qq, c_st)
        den = jnp.sum(s, axis=-1) + inter * jnp.einsum('bhtd,bhd->bht', qq, n_st)
        h = num / jnp.maximum(jnp.abs(den), jnp.exp(-m_t))[..., None]
        b_last = b[..., -1]
        w = b_last[..., None] - b + ii
        m_new = jnp.maximum(b_last + m_st, jnp.max(w, axis=-1))
        w_exp = jnp.exp(w - m_new[..., None])
        decay = jnp.exp(b_last + m_st - m_new)
        c_new = decay[..., None, None] * c_st + jnp.einsum('bhs,bhsd,bhse->bhde', w_exp, kk, vv)
        n_new = decay[..., None] * n_st + jnp.einsum('bhs,bhsd->bhd', w_exp, kk)
        return (c_new, n_new, m_new), h

    init = (jnp.zeros((bsz, nh, dh, dh), f32), jnp.zeros((bsz, nh, dh), f32), jnp.zeros((bsz, nh), f32))
    _, h = lax.scan(step, init, (qc, kc, vc, igc, lfc))
    return h.transpose(1, 0, 3, 2, 4).reshape(bsz, seq, nh, dh)


def band_attention(q, k_pad, v_pad, rel_table):
    bsz, seq, _ = q.shape
    nc = seq // CHUNK
    q = q.reshape(bsz, seq, B_HEADS, B_HEAD_DIM)
    qi = jnp.arange(CHUNK)
    kj = jnp.arange(BAND)
    rel = KV_PAD + qi[:, None] - kj[None, :]
    idx = jnp.clip(rel, -(CHUNK - 1), MAX_REL) + (CHUNK - 1)
    bias = rel_table.astype(jnp.float32)[:, idx]
    scale = B_HEAD_DIM ** -0.5

    def one_chunk(c):
        start = c * CHUNK
        qc = lax.dynamic_slice_in_dim(q, start, CHUNK, axis=1)
        kc = lax.dynamic_slice_in_dim(k_pad, start, BAND, axis=1)
        vc = lax.dynamic_slice_in_dim(v_pad, start, BAND, axis=1)
        s = jnp.einsum('bqhd,bkhd->bhqk', qc, kc).astype(jnp.float32) * scale + bias
        valid = (start - KV_PAD + kj) >= 0
        s = jnp.where(valid, s, -jnp.inf)
        p = jax.nn.softmax(s, axis=-1).astype(vc.dtype)
        return jnp.einsum('bhqk,bkhd->bqhd', p, vc)

    out = lax.map(one_chunk, jnp.arange(nc))
    return out.transpose(1, 0, 2, 3, 4).reshape(bsz, seq, B_WIDTH)


def memory_attention(q, mem, w_kv):
    bsz, seq, _ = q.shape
    q = q.reshape(bsz, seq, MEM_HEADS, MEM_HEAD_DIM)
    mk, mv = jnp.split(mem @ w_kv, 2, axis=-1)
    mk = mk.reshape(bsz, -1, MEM_HEADS, MEM_HEAD_DIM)
    mv = mv.reshape(bsz, -1, MEM_HEADS, MEM_HEAD_DIM)
    s = jnp.einsum('bshd,bmhd->bhsm', q, mk).astype(jnp.float32) * (MEM_HEAD_DIM ** -0.5)
    p = jax.nn.softmax(s, axis=-1).astype(mv.dtype)
    return jnp.einsum('bhsm,bmhd->bshd', p, mv).reshape(bsz, seq, MEM_WIDTH)


def conv_ffn(h, w_up, conv_w, conv_b, w_down):
    u, g = jnp.split(h @ w_up, 2, axis=-1)
    g = causal_dwconv(g, conv_w, conv_b)
    return (jax.nn.silu(g) * u) @ w_down


def setup_inputs(seed: int = 0) -> dict:
    key = jax.random.key(seed)
    ks = jax.random.split(key, 24)
    f32 = jnp.float32

    def nrm(k, shape, scale):
        return jax.random.normal(k, shape, f32) * scale

    gate_b = jnp.concatenate([
        nrm(ks[5], (N_A_LAYERS, A_HEADS), 0.1),
        jnp.linspace(3.0, 6.0, A_HEADS, dtype=f32)[None, :] + nrm(ks[6], (N_A_LAYERS, A_HEADS), 0.1),
    ], axis=-1)
    return {
        "x": nrm(ks[0], (BATCH, SEQ, D_MODEL), 1.0),
        "mem": nrm(ks[1], (BATCH, MEM_TOKENS, D_MODEL), 1.0),
        "norm_mix_g": 1.0 + nrm(ks[2], (DEPTH, D_MODEL), 0.1),
        "norm_ffn_g": 1.0 + nrm(ks[3], (DEPTH, D_MODEL), 0.1),
        "a_w_in": nrm(ks[4], (N_A_LAYERS, D_MODEL, A_IN), D_MODEL ** -0.5),
        "a_gate_b": gate_b,
        "a_conv_w": nrm(ks[7], (N_A_LAYERS, A_CONV, 2 * A_WIDTH), A_CONV ** -0.5),
        "a_conv_b": nrm(ks[8], (N_A_LAYERS, 2 * A_WIDTH), 0.02),
        "a_head_g": 1.0 + nrm(ks[9], (N_A_LAYERS, A_WIDTH), 0.1),
        "a_w_out": nrm(ks[10], (N_A_LAYERS, MIX_WIDTH, D_MODEL), MIX_WIDTH ** -0.5),
        "kv_norm_g": 1.0 + nrm(ks[11], (D_MODEL,), 0.1),
        "w_kv": nrm(ks[12], (D_MODEL, 2 * B_WIDTH), D_MODEL ** -0.5),
        "b_w_in": nrm(ks[13], (N_B_LAYERS, D_MODEL, B_IN), D_MODEL ** -0.5),
        "b_rel_bias": nrm(ks[14], (N_B_LAYERS, B_HEADS, REL_SIZE), 0.5),
        "b_w_out": nrm(ks[15], (N_B_LAYERS, MIX_WIDTH, D_MODEL), MIX_WIDTH ** -0.5),
        "mem_w_kv": nrm(ks[16], (DEPTH, D_MODEL, 2 * MEM_WIDTH), D_MODEL ** -0.5),
        "ffn_w_up": nrm(ks[17], (DEPTH, D_MODEL, 2 * D_FF), D_MODEL ** -0.5),
        "ffn_conv_w": nrm(ks[18], (DEPTH, FFN_CONV, D_FF), FFN_CONV ** -0.5),
        "ffn_conv_b": nrm(ks[19], (DEPTH, D_FF), 0.02),
        "ffn_w_down": nrm(ks[20], (DEPTH, D_FF, D_MODEL), D_FF ** -0.5),
        "final_g": 1.0 + nrm(ks[21], (D_MODEL,), 0.1),
    }


def reference(x, mem, norm_mix_g, norm_ffn_g, a_w_in, a_gate_b, a_conv_w, a_conv_b, a_head_g, a_w_out,
              kv_norm_g, w_kv, b_w_in, b_rel_bias, b_w_out, mem_w_kv, ffn_w_up, ffn_conv_w, ffn_conv_b,
              ffn_w_down, final_g):
    bsz, seq, _ = x.shape
    k_pad = None
    v_pad = None
    for l in range(DEPTH):
        h = rmsnorm(x, norm_mix_g[l])
        if l < N_A_LAYERS:
            a = l
            proj = h @ a_w_in[a]
            qk, v, o, gates, q_mem = jnp.split(
                proj, [2 * A_WIDTH, 3 * A_WIDTH, 4 * A_WIDTH, 4 * A_WIDTH + 2 * A_HEADS], axis=-1)
            qk = jax.nn.silu(causal_dwconv(qk, a_conv_w[a], a_conv_b[a]))
            q, k = jnp.split(qk, 2, axis=-1)
            gates = gates.astype(jnp.float32) + a_gate_b[a].astype(jnp.float32)
            ig, fg = jnp.split(gates, 2, axis=-1)
            hm = mlstm_chunkwise(q.reshape(bsz, seq, A_HEADS, A_HEAD_DIM),
                                 k.reshape(bsz, seq, A_HEADS, A_HEAD_DIM),
                                 v.reshape(bsz, seq, A_HEADS, A_HEAD_DIM),
                                 ig, jax.nn.log_sigmoid(fg))
            hm = hm * lax.rsqrt(jnp.mean(hm * hm, axis=-1, keepdims=True) + EPS)
            hm = hm.reshape(bsz, seq, A_WIDTH) * a_head_g[a].astype(jnp.float32)
            mix_out = (hm * jax.nn.sigmoid(o.astype(jnp.float32))).astype(x.dtype)
            w_out = a_w_out[a]
        else:
            bl = l - N_A_LAYERS
            proj = h @ b_w_in[bl]
            q, q_mem = jnp.split(proj, [B_WIDTH], axis=-1)
            mix_out = band_attention(q, k_pad, v_pad, b_rel_bias[bl])
            w_out = b_w_out[bl]
        mem_out = memory_attention(q_mem, mem, mem_w_kv[l])
        x = x + jnp.concatenate([mix_out, mem_out], axis=-1) @ w_out
        x = x + conv_ffn(rmsnorm(x, norm_ffn_g[l]), ffn_w_up[l], ffn_conv_w[l], ffn_conv_b[l], ffn_w_down[l])
        if l == N_A_LAYERS - 1:
            ks_, vs_ = jnp.split(rmsnorm(x, kv_norm_g) @ w_kv, 2, axis=-1)
            pad = ((0, 0), (KV_PAD, 0), (0, 0), (0, 0))
            k_pad = jnp.pad(ks_.reshape(bsz, seq, B_HEADS, B_HEAD_DIM), pad)
            v_pad = jnp.pad(vs_.reshape(bsz, seq, B_HEADS, B_HEAD_DIM), pad)
    return rmsnorm(x, final_g)
```

```python
import functools
import math

import jax
import jax.numpy as jnp
from jax import lax
from jax.experimental import pallas as pl
from jax.experimental.pallas import tpu as pltpu

F32 = jnp.float32
BF16 = jnp.bfloat16

D_MODEL = 1024
CHUNK = 64
MEM_TOKENS = 256
MEM_HEADS = 4
MEM_WIDTH = 256
MEM_HEAD_DIM = 64
A_WIDTH = 768
A_HEADS = 4
A_HEAD_DIM = 192
A_CONV = 4
B_HEADS = 12
B_HEAD_DIM = 64
B_WIDTH = 768
BAND_CHUNKS = 9
MAX_REL = 128
REL_SIZE = MAX_REL + CHUNK
D_FF = 2816
FFN_CONV = 3
EPS = 1e-6

V_AUG = 256
GATE_LANES = 128
HALO = 8
NEG_BIG = -1e30

TM_IN = 256
L_MLSTM = 256
TM_POST = 256
TQ_BAND = 128
BAND_PREV = (BAND_CHUNKS - 1) * CHUNK
TK_BAND = BAND_PREV + TQ_BAND
N_BIAS_VARIANTS = BAND_PREV // TQ_BAND + 1

VMEM_LIMIT = 56 * 1024 * 1024


def _resident(shape):
    nd = len(shape)
    return pl.BlockSpec(shape, lambda *_: (0,) * nd, pipeline_mode=pl.Buffered(1))


def _rms(x, g):
    return x * lax.rsqrt(jnp.mean(x * x, axis=-1, keepdims=True) + EPS) * g


def _dot(a, b):
    return jnp.dot(a, b, preferred_element_type=F32)


def _dot_nt(a, b):
    return lax.dot_general(a, b, (((1,), (1,)), ((), ())), preferred_element_type=F32)


def _dot_tn(a, b):
    return lax.dot_general(a, b, (((0,), (0,)), ((), ())), preferred_element_type=F32)


def _silu(x):
    return x * (1.0 / (1.0 + jnp.exp(-x)))


def _sigmoid(x):
    return 1.0 / (1.0 + jnp.exp(-x))


def _mem_kv_kernel(mem_ref, wkt_ref, wv_ref, mkbdt_ref, mvbd_ref):
    m = mem_ref[0]
    mkt = _dot_nt(wkt_ref[0], m)
    mv = _dot(m, wv_ref[0])
    f_idx = lax.broadcasted_iota(jnp.int32, (MEM_WIDTH, MEM_HEADS * MEM_TOKENS), 0)
    c_idx = lax.broadcasted_iota(jnp.int32, (MEM_WIDTH, MEM_HEADS * MEM_TOKENS), 1)
    mkt4 = jnp.concatenate([mkt] * MEM_HEADS, axis=1)
    mkbdt_ref[0, 0] = jnp.where(f_idx // MEM_HEAD_DIM == c_idx // MEM_TOKENS, mkt4, 0.0).astype(BF16)
    r_idx = lax.broadcasted_iota(jnp.int32, (MEM_HEADS * MEM_TOKENS, MEM_WIDTH), 0)
    g_idx = lax.broadcasted_iota(jnp.int32, (MEM_HEADS * MEM_TOKENS, MEM_WIDTH), 1)
    mv4 = jnp.concatenate([mv] * MEM_HEADS, axis=0)
    mvbd_ref[0, 0] = jnp.where(r_idx // MEM_TOKENS == g_idx // MEM_HEAD_DIM, mv4, 0.0).astype(BF16)


def _mem_kv(mem_bf, wkt, wv):
    depth = wkt.shape[0]
    bsz = mem_bf.shape[0]
    return pl.pallas_call(
        _mem_kv_kernel,
        grid=(depth, bsz),
        in_specs=[
            pl.BlockSpec((1, MEM_TOKENS, D_MODEL), lambda l, b: (b, 0, 0)),
            pl.BlockSpec((1, MEM_WIDTH, D_MODEL), lambda l, b: (l, 0, 0)),
            pl.BlockSpec((1, D_MODEL, MEM_WIDTH), lambda l, b: (l, 0, 0)),
        ],
        out_specs=[
            pl.BlockSpec((1, 1, MEM_WIDTH, MEM_HEADS * MEM_TOKENS), lambda l, b: (l, b, 0, 0)),
            pl.BlockSpec((1, 1, MEM_HEADS * MEM_TOKENS, MEM_WIDTH), lambda l, b: (l, b, 0, 0)),
        ],
        out_shape=[
            jax.ShapeDtypeStruct((depth, bsz, MEM_WIDTH, MEM_HEADS * MEM_TOKENS), BF16),
            jax.ShapeDtypeStruct((depth, bsz, MEM_HEADS * MEM_TOKENS, MEM_WIDTH), BF16),
        ],
        compiler_params=pltpu.CompilerParams(dimension_semantics=("arbitrary", "arbitrary")),
        name="mem_kv",
    )(mem_bf, wkt, wv)


def _a_in_kernel(x_ref, g_ref, wq_ref, wk_ref, wv_ref, wo_ref, wgate_ref, wmem_ref, gate_b_ref, vone_ref,
                 cw_ref, cb_ref, q_ref, k_ref, v_ref, o_ref, gate_ref, qmem_ref, ext_ref):
    tm = x_ref.shape[1]

    @pl.when(pl.program_id(1) == 0)
    def _():
        ext_ref[:, 0:HALO, :] = jnp.zeros((2 * A_HEADS, HALO, A_HEAD_DIM), F32)

    hn = _rms(x_ref[0], g_ref[...]).astype(BF16)

    def conv_silu(slab, raw):
        ext_ref[slab, HALO:HALO + tm, :] = raw
        y = cb_ref[slab]
        for j in range(A_CONV):
            y = y + cw_ref[slab, j:j + 1, :] * ext_ref[slab, pl.ds(HALO - (A_CONV - 1) + j, tm), :]
        ext_ref[slab, 0:HALO, :] = ext_ref[slab, tm:tm + HALO, :]
        return _silu(y)

    for h in range(A_HEADS):
        q_ref[0, h] = conv_silu(h, _dot(hn, wq_ref[h])).astype(BF16)
        k_ref[0, h] = conv_silu(A_HEADS + h, _dot(hn, wk_ref[h])).astype(BF16)
        v_ref[0, h] = (_dot(hn, wv_ref[h]) + vone_ref[...]).astype(BF16)
        o_ref[0, h] = _dot(hn, wo_ref[h])
    gates = _dot(hn, wgate_ref[...]) + gate_b_ref[...]
    lane = lax.broadcasted_iota(jnp.int32, gates.shape, 1)
    logsig = jnp.minimum(gates, 0.0) - jnp.log(1.0 + jnp.exp(-jnp.abs(gates)))
    gate_ref[0] = jnp.where((lane >= A_HEADS) & (lane < 2 * A_HEADS), logsig, gates)
    qmem_ref[0] = _dot(hn, wmem_ref[...]).astype(BF16)


def _a_in(x, g, wq, wk, wv, wo, wgate, wmem, gate_b, vone, cw, cb):
    bsz, seq, _ = x.shape
    tm = TM_IN
    head_spec = lambda width: pl.BlockSpec((1, A_HEADS, tm, width), lambda b, j: (b, 0, j, 0))
    return pl.pallas_call(
        _a_in_kernel,
        grid=(bsz, seq // tm),
        in_specs=[
            pl.BlockSpec((1, tm, D_MODEL), lambda b, j: (b, j, 0)),
            _resident(g.shape), _resident(wq.shape), _resident(wk.shape), _resident(wv.shape),
            _resident(wo.shape), _resident(wgate.shape), _resident(wmem.shape), _resident(gate_b.shape),
            _resident(vone.shape), _resident(cw.shape), _resident(cb.shape),
        ],
        out_specs=[
            head_spec(A_HEAD_DIM), head_spec(A_HEAD_DIM), head_spec(V_AUG), head_spec(A_HEAD_DIM),
            pl.BlockSpec((1, tm, GATE_LANES), lambda b, j: (b, j, 0)),
            pl.BlockSpec((1, tm, MEM_WIDTH), lambda b, j: (b, j, 0)),
        ],
        out_shape=[
            jax.ShapeDtypeStruct((bsz, A_HEADS, seq, A_HEAD_DIM), BF16),
            jax.ShapeDtypeStruct((bsz, A_HEADS, seq, A_HEAD_DIM), BF16),
            jax.ShapeDtypeStruct((bsz, A_HEADS, seq, V_AUG), BF16),
            jax.ShapeDtypeStruct((bsz, A_HEADS, seq, A_HEAD_DIM), F32),
            jax.ShapeDtypeStruct((bsz, seq, GATE_LANES), F32),
            jax.ShapeDtypeStruct((bsz, seq, MEM_WIDTH), BF16),
        ],
        scratch_shapes=[pltpu.VMEM((2 * A_HEADS, tm + HALO, A_HEAD_DIM), F32)],
        compiler_params=pltpu.CompilerParams(
            dimension_semantics=("arbitrary", "arbitrary"), vmem_limit_bytes=VMEM_LIMIT),
        name="a_in",
    )(x, g, wq, wk, wv, wo, wgate, wmem, gate_b, vone, cw, cb)


def _mlstm_kernel(q_ref, k_ref, v_ref, o_ref, gate_ref, hg_ref, out_ref, c_ref, m_ref):
    L = q_ref.shape[2]
    scale = A_HEAD_DIM ** -0.5

    @pl.when(pl.program_id(1) == 0)
    def _():
        c_ref[...] = jnp.zeros(c_ref.shape, F32)
        m_ref[...] = jnp.zeros(m_ref.shape, F32)

    gates = gate_ref[0]
    lane = lax.broadcasted_iota(jnp.int32, gates.shape, 1)
    row = lax.broadcasted_iota(jnp.int32, (L, L), 0)
    col = lax.broadcasted_iota(jnp.int32, (L, L), 1)
    causal = row >= col
    logf = jnp.where((lane >= A_HEADS) & (lane < 2 * A_HEADS), gates, 0.0)
    tri = causal.astype(BF16)
    logf_hi = logf.astype(BF16)
    logf_lo = (logf - logf_hi.astype(F32)).astype(BF16)
    bcum = _dot(tri, logf_hi) + _dot(tri, logf_lo)
    comb_t = jnp.where(lane < A_HEADS, gates, bcum).T

    lane_v = lax.broadcasted_iota(jnp.int32, (L, V_AUG), 1)

    for h in range(A_HEADS):
        b_col = jnp.sum(jnp.where(lane == A_HEADS + h, bcum, 0.0), axis=1, keepdims=True)
        i_col = jnp.sum(jnp.where(lane == h, gates, 0.0), axis=1, keepdims=True)
        i_row = comb_t[h:h + 1, :]
        b_row = comb_t[A_HEADS + h:A_HEADS + h + 1, :]
        m_prev = m_ref[h:h + 1, 0:1]

        d = jnp.where(causal, b_col - b_row + i_row, -jnp.inf)
        m_inter = b_col + m_prev
        m_t = jnp.maximum(m_inter, jnp.max(d, axis=1, keepdims=True))
        inter = jnp.exp(m_inter - m_t)
        p = jnp.exp(d - m_t) * scale

        qh = q_ref[0, h]
        kh = k_ref[0, h]
        vh = v_ref[0, h]
        s = (_dot_nt(qh, kh) * p).astype(BF16)
        c_prev = c_ref[h]
        num = _dot(s, vh) + inter * _dot(qh, c_prev.astype(BF16))
        den = jnp.sum(jnp.where(lane_v == A_HEAD_DIM, num, 0.0), axis=1, keepdims=True)
        hv = num * (1.0 / jnp.maximum(jnp.abs(den), jnp.exp(-m_t)))
        hsq = jnp.where(lane_v < A_HEAD_DIM, hv * hv, 0.0)
        rs = lax.rsqrt(jnp.sum(hsq, axis=1, keepdims=True) * (1.0 / A_HEAD_DIM) + EPS)
        hn = (hv * rs)[:, :A_HEAD_DIM] * hg_ref[h]
        out_ref[0, h] = (hn * _sigmoid(o_ref[0, h])).astype(BF16)

        b_last = b_col[L - 1:L, :]
        w = b_last - b_col + i_col
        m_new = jnp.maximum(b_last + m_prev, jnp.max(w, axis=0, keepdims=True))
        w_exp = jnp.exp(w - m_new) * scale
        decay = jnp.exp(b_last + m_prev - m_new)
        wv = (w_exp * vh.astype(F32)).astype(BF16)
        c_ref[h] = decay * c_prev + _dot_tn(kh, wv)
        m_ref[h:h + 1, :] = jnp.broadcast_to(m_new, (1, m_ref.shape[1]))


def _mlstm(q, k, v, o, gates, head_g):
    bsz, nh, seq, dh = q.shape
    L = L_MLSTM
    head_spec = lambda width: pl.BlockSpec((1, nh, L, width), lambda b, j: (b, 0, j, 0))
    return pl.pallas_call(
        _mlstm_kernel,
        grid=(bsz, seq // L),
        in_specs=[
            head_spec(dh), head_spec(dh), head_spec(V_AUG), head_spec(dh),
            pl.BlockSpec((1, L, GATE_LANES), lambda b, j: (b, j, 0)),
            _resident(head_g.shape),
        ],
        out_specs=head_spec(dh),
        out_shape=jax.ShapeDtypeStruct((bsz, nh, seq, dh), BF16),
        scratch_shapes=[pltpu.VMEM((nh, dh, V_AUG), F32), pltpu.VMEM((8, 128), F32)],
        compiler_params=pltpu.CompilerParams(
            dimension_semantics=("arbitrary", "arbitrary"), vmem_limit_bytes=VMEM_LIMIT),
        name="mlstm",
    )(q, k, v, o, gates, head_g)


def _post_mixer_kernel(*refs, mix_heads, last):
    if last:
        (x_ref, mix_ref, qmem_ref, mkbdt_ref, mvbd_ref, wout_mix_ref, wout_mem_ref, ffn_g_ref, wu_ref, wg_ref,
         cw_ref, cb_ref, wd_ref, final_g_ref, out_ref, gext_ref) = refs
    else:
        (x_ref, mix_ref, qmem_ref, mkbdt_ref, mvbd_ref, wout_mix_ref, wout_mem_ref, ffn_g_ref, wu_ref, wg_ref,
         cw_ref, cb_ref, wd_ref, kv_g_ref, wkt_ref, wv_ref, nxt_g_ref, wq_ref, wqmem_ref,
         out_ref, kt_ref, v_ref, qn_ref, qmemn_ref, gext_ref) = refs
    tm = x_ref.shape[1]

    @pl.when(pl.program_id(1) == 0)
    def _():
        gext_ref[0:HALO, :] = jnp.zeros((HALO, D_FF), F32)

    s = _dot(qmem_ref[0], mkbdt_ref[0, 0]) * (MEM_HEAD_DIM ** -0.5)
    probs = []
    for h in range(MEM_HEADS):
        sh = s[:, h * MEM_TOKENS:(h + 1) * MEM_TOKENS]
        e = jnp.exp(sh - jnp.max(sh, axis=1, keepdims=True))
        probs.append((e * (1.0 / jnp.sum(e, axis=1, keepdims=True))).astype(BF16))
    mem_out = _dot(jnp.concatenate(probs, axis=1), mvbd_ref[0, 0])

    y = _dot(mem_out.astype(BF16), wout_mem_ref[...])
    if mix_heads:
        for h in range(mix_heads):
            y = y + _dot(mix_ref[0, h], wout_mix_ref[h])
    else:
        y = y + _dot(mix_ref[0], wout_mix_ref[...])
    x1 = x_ref[0] + y

    hn = _rms(x1, ffn_g_ref[...]).astype(BF16)
    u = _dot(hn, wu_ref[...])
    gext_ref[HALO:HALO + tm, :] = _dot(hn, wg_ref[...])
    gc = cb_ref[...]
    for j in range(FFN_CONV):
        gc = gc + cw_ref[j:j + 1, :] * gext_ref[pl.ds(HALO - (FFN_CONV - 1) + j, tm), :]
    gext_ref[0:HALO, :] = gext_ref[tm:tm + HALO, :]
    x2 = x1 + _dot((_silu(gc) * u).astype(BF16), wd_ref[...])

    if last:
        out_ref[0] = _rms(x2, final_g_ref[...])
    else:
        out_ref[0] = x2
        hkv = _rms(x2, kv_g_ref[...]).astype(BF16)
        kt_ref[0] = _dot_nt(wkt_ref[...], hkv).astype(BF16)
        v_ref[0] = _dot(hkv, wv_ref[...]).astype(BF16)
        hq = _rms(x2, nxt_g_ref[...]).astype(BF16)
        qn_ref[0] = _dot(hq, wq_ref[...]).astype(BF16)
        qmemn_ref[0] = _dot(hq, wqmem_ref[...]).astype(BF16)


def _post_mixer(x, mix, qmem, mkbdt, mvbd, layer, wout_mix, wout_mem, ffn_g, wu, wg, cw, cb, wd, tail, last):
    bsz, seq, _ = x.shape
    tm = TM_POST
    mix_heads = mix.shape[1] if mix.ndim == 4 else 0
    if mix_heads:
        mix_spec = pl.BlockSpec((1, mix_heads, tm, mix.shape[3]), lambda b, j: (b, 0, j, 0))
    else:
        mix_spec = pl.BlockSpec((1, tm, mix.shape[2]), lambda b, j: (b, j, 0))
    row_spec = lambda width: pl.BlockSpec((1, tm, width), lambda b, j: (b, j, 0))
    weights = (wout_mix, wout_mem, ffn_g, wu, wg, cw, cb, wd) + tuple(tail)
    in_specs = [
        row_spec(D_MODEL), mix_spec, row_spec(MEM_WIDTH),
        pl.BlockSpec((1, 1) + mkbdt.shape[2:], lambda b, j: (layer, b, 0, 0)),
        pl.BlockSpec((1, 1) + mvbd.shape[2:], lambda b, j: (layer, b, 0, 0)),
    ] + [_resident(w.shape) for w in weights]
    if last:
        out_specs = row_spec(D_MODEL)
        out_shape = jax.ShapeDtypeStruct((bsz, seq, D_MODEL), F32)
    else:
        out_specs = [
            row_spec(D_MODEL),
            pl.BlockSpec((1, B_WIDTH, tm), lambda b, j: (b, 0, j)),
            row_spec(B_WIDTH), row_spec(B_WIDTH), row_spec(MEM_WIDTH),
        ]
        out_shape = [
            jax.ShapeDtypeStruct((bsz, seq, D_MODEL), F32),
            jax.ShapeDtypeStruct((bsz, B_WIDTH, seq), BF16),
            jax.ShapeDtypeStruct((bsz, seq, B_WIDTH), BF16),
            jax.ShapeDtypeStruct((bsz, seq, B_WIDTH), BF16),
            jax.ShapeDtypeStruct((bsz, seq, MEM_WIDTH), BF16),
        ]
    return pl.pallas_call(
        functools.partial(_post_mixer_kernel, mix_heads=mix_heads, last=last),
        grid=(bsz, seq // tm),
        in_specs=in_specs,
        out_specs=out_specs,
        out_shape=out_shape,
        scratch_shapes=[pltpu.VMEM((tm + HALO, D_FF), F32)],
        compiler_params=pltpu.CompilerParams(
            dimension_semantics=("arbitrary", "arbitrary"), vmem_limit_bytes=VMEM_LIMIT),
        name="post_mixer_last" if last else "post_mixer",
    )(x, mix, qmem, mkbdt, mvbd, *weights)


def _band_bias_kernel(tbl_ref, out_ref):
    v = pl.program_id(0)
    h = pl.program_id(1)
    r = lax.broadcasted_iota(jnp.int32, (TQ_BAND, TK_BAND), 0)
    k = lax.broadcasted_iota(jnp.int32, (TQ_BAND, TK_BAND), 1)
    qpos = v * TQ_BAND + r
    idx = jnp.clip(qpos - k, -(CHUNK - 1), MAX_REL) + (CHUNK - 1)
    dchunk = qpos // CHUNK - k // CHUNK
    valid = (dchunk >= 0) & (dchunk < BAND_CHUNKS)

    def body(e, acc):
        return jnp.where(idx == e, tbl_ref[h, e], acc)

    acc = lax.fori_loop(0, REL_SIZE, body, jnp.zeros((TQ_BAND, TK_BAND), F32))
    out_ref[0, 0] = jnp.where(valid, acc, NEG_BIG)


def _band_bias(rel_table):
    return pl.pallas_call(
        _band_bias_kernel,
        grid=(N_BIAS_VARIANTS, B_HEADS),
        in_specs=[pl.BlockSpec(memory_space=pltpu.SMEM)],
        out_specs=pl.BlockSpec((1, 1, TQ_BAND, TK_BAND), lambda v, h: (v, h, 0, 0)),
        out_shape=jax.ShapeDtypeStruct((N_BIAS_VARIANTS, B_HEADS, TQ_BAND, TK_BAND), F32),
        compiler_params=pltpu.CompilerParams(dimension_semantics=("arbitrary", "arbitrary")),
        name="band_bias",
    )(rel_table)


def _band_attn_kernel(q_ref, kt_ref, v_ref, bias_ref, out_ref):
    j = pl.program_id(1)
    start = pl.multiple_of(jnp.maximum(j - (N_BIAS_VARIANTS - 1), 0) * TQ_BAND, TQ_BAND)
    scale = B_HEAD_DIM ** -0.5
    q = q_ref[0]
    outs = []
    for h in range(B_HEADS):
        lo = h * B_HEAD_DIM
        qh = q[:, lo:lo + B_HEAD_DIM]
        kt = kt_ref[0, lo:lo + B_HEAD_DIM, pl.ds(start, TK_BAND)]
        vh = v_ref[0, pl.ds(start, TK_BAND), lo:lo + B_HEAD_DIM]
        s = _dot(qh, kt) * scale + bias_ref[0, h]
        e = jnp.exp(s - jnp.max(s, axis=1, keepdims=True))
        inv = 1.0 / jnp.sum(e, axis=1, keepdims=True)
        outs.append((_dot(e.astype(BF16), vh) * inv).astype(BF16))
    out_ref[0] = jnp.concatenate(outs, axis=1)


def _band_attn(q, kt, v, bias):
    bsz, seq, _ = q.shape
    nv = N_BIAS_VARIANTS
    return pl.pallas_call(
        _band_attn_kernel,
        grid=(bsz, seq // TQ_BAND),
        in_specs=[
            pl.BlockSpec((1, TQ_BAND, B_WIDTH), lambda b, j: (b, j, 0)),
            pl.BlockSpec((1, B_WIDTH, seq), lambda b, j: (b, 0, 0)),
            pl.BlockSpec((1, seq, B_WIDTH), lambda b, j: (b, 0, 0)),
            pl.BlockSpec((1, B_HEADS, TQ_BAND, TK_BAND), lambda b, j: (jnp.minimum(j, nv - 1), 0, 0, 0)),
        ],
        out_specs=pl.BlockSpec((1, TQ_BAND, B_WIDTH), lambda b, j: (b, j, 0)),
        out_shape=jax.ShapeDtypeStruct((bsz, seq, B_WIDTH), BF16),
        compiler_params=pltpu.CompilerParams(
            dimension_semantics=("arbitrary", "arbitrary"), vmem_limit_bytes=VMEM_LIMIT),
        name="band_attn",
    )(q, kt, v, bias)


def kernel(x, mem, norm_mix_g, norm_ffn_g, a_w_in, a_gate_b, a_conv_w, a_conv_b, a_head_g, a_w_out, kv_norm_g, w_kv,
           b_w_in, b_rel_bias, b_w_out, mem_w_kv, ffn_w_up, ffn_conv_w, ffn_conv_b, ffn_w_down, final_g):
    bsz, seq, d = x.shape
    assert d == D_MODEL and seq % max(TM_IN, L_MLSTM, TM_POST, TQ_BAND) == 0
    assert a_w_in.shape[0] == 1 and b_w_in.shape[0] == 1, "one mLSTM layer followed by one band-attention layer"
    row = lambda g: g.reshape(1, -1).astype(F32)

    wkt = jnp.swapaxes(mem_w_kv[:, :, :MEM_WIDTH], 1, 2).astype(BF16)
    wmv = mem_w_kv[:, :, MEM_WIDTH:].astype(BF16)
    mkbdt, mvbd = _mem_kv(mem.astype(BF16), wkt, wmv)

    w = a_w_in[0]
    heads = lambda m: jnp.swapaxes(m.reshape(D_MODEL, A_HEADS, A_HEAD_DIM), 0, 1)
    wq = heads(w[:, 0:A_WIDTH]).astype(BF16)
    wk = heads(w[:, A_WIDTH:2 * A_WIDTH]).astype(BF16)
    wv = jnp.pad(heads(w[:, 2 * A_WIDTH:3 * A_WIDTH]), ((0, 0), (0, 0), (0, V_AUG - A_HEAD_DIM))).astype(BF16)
    wo = heads(w[:, 3 * A_WIDTH:4 * A_WIDTH]).astype(BF16)
    n_gate = 2 * A_HEADS
    wgate = jnp.pad(w[:, 4 * A_WIDTH:4 * A_WIDTH + n_gate], ((0, 0), (0, GATE_LANES - n_gate))).astype(BF16)
    wmem = w[:, 4 * A_WIDTH + n_gate:].astype(BF16)
    gate_b = jnp.pad(a_gate_b[0].astype(F32), (0, GATE_LANES - n_gate)).reshape(1, GATE_LANES)
    vone = (jnp.arange(V_AUG) == A_HEAD_DIM).astype(F32).reshape(1, V_AUG)
    cw = jnp.transpose(a_conv_w[0].reshape(A_CONV, 2 * A_HEADS, A_HEAD_DIM), (1, 0, 2)).astype(F32)
    cb = a_conv_b[0].reshape(2 * A_HEADS, 1, A_HEAD_DIM).astype(F32)
    q, k, v, o, gates, qmem = _a_in(x, row(norm_mix_g[0]), wq, wk, wv, wo, wgate, wmem, gate_b, vone, cw, cb)

    head_g = a_head_g[0].reshape(A_HEADS, 1, A_HEAD_DIM).astype(F32)
    mix = _mlstm(q, k, v, o, gates, head_g)

    def ffn_weights(l):
        return (row(norm_ffn_g[l]), ffn_w_up[l][:, :D_FF].astype(BF16), ffn_w_up[l][:, D_FF:].astype(BF16),
                ffn_conv_w[l].astype(F32), row(ffn_conv_b[l]), ffn_w_down[l].astype(BF16))

    wout = a_w_out[0]
    tail = (row(kv_norm_g), jnp.swapaxes(w_kv[:, :B_WIDTH], 0, 1).astype(BF16), w_kv[:, B_WIDTH:].astype(BF16),
            row(norm_mix_g[1]), b_w_in[0][:, :B_WIDTH].astype(BF16), b_w_in[0][:, B_WIDTH:].astype(BF16))
    x1, kt, vb, qb, qmem_b = _post_mixer(
        x, mix, qmem, mkbdt, mvbd, 0,
        wout[:A_WIDTH].reshape(A_HEADS, A_HEAD_DIM, D_MODEL).astype(BF16), wout[A_WIDTH:].astype(BF16),
        *ffn_weights(0), tail=tail, last=False)

    bias = _band_bias(b_rel_bias[0].astype(F32))
    mix_b = _band_attn(qb, kt, vb, bias)
    wout = b_w_out[0]
    return _post_mixer(
        x1, mix_b, qmem_b, mkbdt, mvbd, 1,
        wout[:B_WIDTH].astype(BF16), wout[B_WIDTH:].astype(BF16),
        *ffn_weights(1), tail=(row(final_g),), last=True)
```

```python
import functools
import math

import jax
import jax.numpy as jnp
from jax import lax
from jax.experimental import pallas as pl
from jax.experimental.pallas import tpu as pltpu

F32 = jnp.float32
BF16 = jnp.bfloat16

D_MODEL = 1024
CHUNK = 64
MEM_TOKENS = 256
MEM_HEADS = 4
MEM_WIDTH = 256
MEM_HEAD_DIM = 64
A_WIDTH = 768
A_HEADS = 4
A_HEAD_DIM = 192
A_CONV = 4
B_HEADS = 12
B_HEAD_DIM = 64
B_WIDTH = 768
BAND_CHUNKS = 9
MAX_REL = 128
REL_SIZE = MAX_REL + CHUNK
D_FF = 2816
FFN_CONV = 3
EPS = 1e-6

V_AUG = 256
GATE_LANES = 128
HALO = 8
NEG_BIG = -1e30

TM_IN = 256
L_MLSTM = 256
TM_POST = 256
FF_CHUNK = 1408
TQ_HALF = 128
TQ_BAND = 2 * TQ_HALF
BAND_PREV = (BAND_CHUNKS - 1) * CHUNK
TK_HALF = BAND_PREV + TQ_HALF
TK_BAND = BAND_PREV + TQ_BAND
N_BIAS_VARIANTS = BAND_PREV // TQ_HALF + 1
BIAS_LANES = TK_HALF + TQ_HALF
SOFTMAX_ROWS = 64

VMEM_LIMIT = 56 * 1024 * 1024


def _resident(shape):
    nd = len(shape)
    return pl.BlockSpec(shape, lambda *_: (0,) * nd, pipeline_mode=pl.Buffered(1))


def _rms(x, g):
    return x * lax.rsqrt(jnp.mean(x * x, axis=-1, keepdims=True) + EPS) * g


def _dot(a, b):
    return jnp.dot(a, b, preferred_element_type=F32)


def _dot_nt(a, b):
    return lax.dot_general(a, b, (((1,), (1,)), ((), ())), preferred_element_type=F32)


def _dot_tn(a, b):
    return lax.dot_general(a, b, (((0,), (0,)), ((), ())), preferred_element_type=F32)


def _sigmoid(x):
    return 0.5 + 0.5 * jnp.tanh(0.5 * x)


def _silu(x):
    half = 0.5 * x
    return half + half * jnp.tanh(half)


def _mem_kv_kernel(mem_ref, wkt_ref, wv_ref, mkbdt_ref, mvbd_ref):
    m = mem_ref[0]
    mkt = _dot_nt(wkt_ref[0], m)
    mv = _dot(m, wv_ref[0])
    f_idx = lax.broadcasted_iota(jnp.int32, (MEM_WIDTH, MEM_HEADS * MEM_TOKENS), 0)
    c_idx = lax.broadcasted_iota(jnp.int32, (MEM_WIDTH, MEM_HEADS * MEM_TOKENS), 1)
    mkt4 = jnp.concatenate([mkt] * MEM_HEADS, axis=1)
    mkbdt_ref[0, 0] = jnp.where(f_idx // MEM_HEAD_DIM == c_idx // MEM_TOKENS, mkt4, 0.0).astype(BF16)
    r_idx = lax.broadcasted_iota(jnp.int32, (MEM_HEADS * MEM_TOKENS, MEM_WIDTH), 0)
    g_idx = lax.broadcasted_iota(jnp.int32, (MEM_HEADS * MEM_TOKENS, MEM_WIDTH), 1)
    mv4 = jnp.concatenate([mv] * MEM_HEADS, axis=0)
    mvbd_ref[0, 0] = jnp.where(r_idx // MEM_TOKENS == g_idx // MEM_HEAD_DIM, mv4, 0.0).astype(BF16)


def _mem_kv(mem_bf, wkt, wv):
    depth = wkt.shape[0]
    bsz = mem_bf.shape[0]
    return pl.pallas_call(
        _mem_kv_kernel,
        grid=(depth, bsz),
        in_specs=[
            pl.BlockSpec((1, MEM_TOKENS, D_MODEL), lambda l, b: (b, 0, 0)),
            pl.BlockSpec((1, MEM_WIDTH, D_MODEL), lambda l, b: (l, 0, 0)),
            pl.BlockSpec((1, D_MODEL, MEM_WIDTH), lambda l, b: (l, 0, 0)),
        ],
        out_specs=[
            pl.BlockSpec((1, 1, MEM_WIDTH, MEM_HEADS * MEM_TOKENS), lambda l, b: (l, b, 0, 0)),
            pl.BlockSpec((1, 1, MEM_HEADS * MEM_TOKENS, MEM_WIDTH), lambda l, b: (l, b, 0, 0)),
        ],
        out_shape=[
            jax.ShapeDtypeStruct((depth, bsz, MEM_WIDTH, MEM_HEADS * MEM_TOKENS), BF16),
            jax.ShapeDtypeStruct((depth, bsz, MEM_HEADS * MEM_TOKENS, MEM_WIDTH), BF16),
        ],
        compiler_params=pltpu.CompilerParams(dimension_semantics=("arbitrary", "arbitrary")),
        name="mem_kv",
    )(mem_bf, wkt, wv)


def _a_in_kernel(x_ref, g_ref, wq_ref, wk_ref, wv_ref, wo_ref, wgate_ref, wmem_ref, gate_b_ref, vone_ref,
                 cw_ref, cb_ref, q_ref, k_ref, v_ref, o_ref, gate_ref, qmem_ref, ext_ref):
    tm = x_ref.shape[1]

    @pl.when(pl.program_id(1) == 0)
    def _():
        ext_ref[:, 0:HALO, :] = jnp.zeros((2 * A_HEADS, HALO, A_HEAD_DIM), F32)

    hn = _rms(x_ref[0], g_ref[...]).astype(BF16)

    def conv_silu(slab, raw):
        ext_ref[slab, HALO:HALO + tm, :] = raw
        y = cb_ref[slab]
        for j in range(A_CONV):
            y = y + cw_ref[slab, j:j + 1, :] * ext_ref[slab, pl.ds(HALO - (A_CONV - 1) + j, tm), :]
        ext_ref[slab, 0:HALO, :] = ext_ref[slab, tm:tm + HALO, :]
        return _silu(y)

    for h in range(A_HEADS):
        q_ref[0, h] = conv_silu(h, _dot(hn, wq_ref[h])).astype(BF16)
        k_ref[0, h] = conv_silu(A_HEADS + h, _dot(hn, wk_ref[h])).astype(BF16)
        v_ref[0, h] = (_dot(hn, wv_ref[h]) + vone_ref[...]).astype(BF16)
        o_ref[0, h] = _dot(hn, wo_ref[h])
    gates = _dot(hn, wgate_ref[...]) + gate_b_ref[...]
    lane = lax.broadcasted_iota(jnp.int32, gates.shape, 1)
    logsig = jnp.minimum(gates, 0.0) - jnp.log(1.0 + jnp.exp(-jnp.abs(gates)))
    gate_ref[0] = jnp.where((lane >= A_HEADS) & (lane < 2 * A_HEADS), logsig, gates)
    qmem_ref[0] = _dot(hn, wmem_ref[...]).astype(BF16)


def _a_in(x, g, wq, wk, wv, wo, wgate, wmem, gate_b, vone, cw, cb):
    bsz, seq, _ = x.shape
    tm = TM_IN
    head_spec = lambda width: pl.BlockSpec((1, A_HEADS, tm, width), lambda b, j: (b, 0, j, 0))
    return pl.pallas_call(
        _a_in_kernel,
        grid=(bsz, seq // tm),
        in_specs=[
            pl.BlockSpec((1, tm, D_MODEL), lambda b, j: (b, j, 0)),
            _resident(g.shape), _resident(wq.shape), _resident(wk.shape), _resident(wv.shape),
            _resident(wo.shape), _resident(wgate.shape), _resident(wmem.shape), _resident(gate_b.shape),
            _resident(vone.shape), _resident(cw.shape), _resident(cb.shape),
        ],
        out_specs=[
            head_spec(A_HEAD_DIM), head_spec(A_HEAD_DIM), head_spec(V_AUG), head_spec(A_HEAD_DIM),
            pl.BlockSpec((1, tm, GATE_LANES), lambda b, j: (b, j, 0)),
            pl.BlockSpec((1, tm, MEM_WIDTH), lambda b, j: (b, j, 0)),
        ],
        out_shape=[
            jax.ShapeDtypeStruct((bsz, A_HEADS, seq, A_HEAD_DIM), BF16),
            jax.ShapeDtypeStruct((bsz, A_HEADS, seq, A_HEAD_DIM), BF16),
            jax.ShapeDtypeStruct((bsz, A_HEADS, seq, V_AUG), BF16),
            jax.ShapeDtypeStruct((bsz, A_HEADS, seq, A_HEAD_DIM), F32),
            jax.ShapeDtypeStruct((bsz, seq, GATE_LANES), F32),
            jax.ShapeDtypeStruct((bsz, seq, MEM_WIDTH), BF16),
        ],
        scratch_shapes=[pltpu.VMEM((2 * A_HEADS, tm + HALO, A_HEAD_DIM), F32)],
        compiler_params=pltpu.CompilerParams(
            dimension_semantics=("arbitrary", "arbitrary"), vmem_limit_bytes=VMEM_LIMIT),
        name="a_in",
    )(x, g, wq, wk, wv, wo, wgate, wmem, gate_b, vone, cw, cb)


def _mlstm_kernel(q_ref, k_ref, v_ref, o_ref, gate_ref, hg_ref, out_ref, c_ref, m_ref):
    L = q_ref.shape[2]
    scale = A_HEAD_DIM ** -0.5

    @pl.when(pl.program_id(1) == 0)
    def _():
        c_ref[...] = jnp.zeros(c_ref.shape, F32)
        m_ref[...] = jnp.zeros(m_ref.shape, F32)

    gates = gate_ref[0]
    lane = lax.broadcasted_iota(jnp.int32, gates.shape, 1)
    row = lax.broadcasted_iota(jnp.int32, (L, L), 0)
    col = lax.broadcasted_iota(jnp.int32, (L, L), 1)
    causal = row >= col
    logf = jnp.where((lane >= A_HEADS) & (lane < 2 * A_HEADS), gates, 0.0)
    tri = causal.astype(BF16)
    logf_hi = logf.astype(BF16)
    logf_lo = (logf - logf_hi.astype(F32)).astype(BF16)
    bcum = _dot(tri, logf_hi) + _dot(tri, logf_lo)
    comb_t = jnp.where(lane < A_HEADS, gates, bcum).T

    lane_v = lax.broadcasted_iota(jnp.int32, (L, V_AUG), 1)

    for h in range(A_HEADS):
        b_col = jnp.sum(jnp.where(lane == A_HEADS + h, bcum, 0.0), axis=1, keepdims=True)
        i_col = jnp.sum(jnp.where(lane == h, gates, 0.0), axis=1, keepdims=True)
        i_row = comb_t[h:h + 1, :]
        b_row = comb_t[A_HEADS + h:A_HEADS + h + 1, :]
        m_prev = m_ref[h:h + 1, 0:1]

        d = jnp.where(causal, b_col - b_row + i_row, -jnp.inf)
        m_inter = b_col + m_prev
        m_t = jnp.maximum(m_inter, jnp.max(d, axis=1, keepdims=True))
        inter = jnp.exp(m_inter - m_t)
        p = jnp.exp(d - m_t) * scale

        qh = q_ref[0, h]
        kh = k_ref[0, h]
        vh = v_ref[0, h]
        s = (_dot_nt(qh, kh) * p).astype(BF16)
        c_prev = c_ref[h]
        num = _dot(s, vh) + inter * _dot(qh, c_prev.astype(BF16))
        den = jnp.sum(jnp.where(lane_v == A_HEAD_DIM, num, 0.0), axis=1, keepdims=True)
        hv = num * (1.0 / jnp.maximum(jnp.abs(den), jnp.exp(-m_t)))
        hsq = jnp.where(lane_v < A_HEAD_DIM, hv * hv, 0.0)
        rs = lax.rsqrt(jnp.sum(hsq, axis=1, keepdims=True) * (1.0 / A_HEAD_DIM) + EPS)
        hn = (hv * rs)[:, :A_HEAD_DIM] * hg_ref[h]
        out_ref[0, h] = (hn * _sigmoid(o_ref[0, h])).astype(BF16)

        b_last = b_col[L - 1:L, :]
        w = b_last - b_col + i_col
        m_new = jnp.maximum(b_last + m_prev, jnp.max(w, axis=0, keepdims=True))
        w_exp = jnp.exp(w - m_new) * scale
        decay = jnp.exp(b_last + m_prev - m_new)
        wv = (w_exp * vh.astype(F32)).astype(BF16)
        c_ref[h] = decay * c_prev + _dot_tn(kh, wv)
        m_ref[h:h + 1, :] = jnp.broadcast_to(m_new, (1, m_ref.shape[1]))


def _mlstm(q, k, v, o, gates, head_g):
    bsz, nh, seq, dh = q.shape
    L = L_MLSTM
    head_spec = lambda width: pl.BlockSpec((1, nh, L, width), lambda b, j: (b, 0, j, 0))
    return pl.pallas_call(
        _mlstm_kernel,
        grid=(bsz, seq // L),
        in_specs=[
            head_spec(dh), head_spec(dh), head_spec(V_AUG), head_spec(dh),
            pl.BlockSpec((1, L, GATE_LANES), lambda b, j: (b, j, 0)),
            _resident(head_g.shape),
        ],
        out_specs=head_spec(dh),
        out_shape=jax.ShapeDtypeStruct((bsz, nh, seq, dh), BF16),
        scratch_shapes=[pltpu.VMEM((nh, dh, V_AUG), F32), pltpu.VMEM((8, 128), F32)],
        compiler_params=pltpu.CompilerParams(
            dimension_semantics=("arbitrary", "arbitrary"), vmem_limit_bytes=VMEM_LIMIT),
        name="mlstm",
    )(q, k, v, o, gates, head_g)


def _post_mixer_kernel(*refs, mix_heads, last):
    if last:
        (x_ref, mix_ref, qmem_ref, mkbdt_ref, mvbd_ref, wout_mix_ref, wout_mem_ref, ffn_g_ref, wu_ref, wg_ref,
         cw_ref, cb_ref, wd_ref, final_g_ref, out_ref, halo_ref) = refs
    else:
        (x_ref, mix_ref, qmem_ref, mkbdt_ref, mvbd_ref, wout_mix_ref, wout_mem_ref, ffn_g_ref, wu_ref, wg_ref,
         cw_ref, cb_ref, wd_ref, kv_g_ref, wk_ref, wvt_ref, nxt_g_ref, wqt_ref, wqmem_ref,
         out_ref, k_ref, vt_ref, qnt_ref, qmemn_ref, halo_ref) = refs
    tm = x_ref.shape[1]

    @pl.when(pl.program_id(1) == 0)
    def _():
        halo_ref[...] = jnp.zeros((HALO, D_FF), F32)

    s = _dot(qmem_ref[0], mkbdt_ref[0, 0]) * (MEM_HEAD_DIM ** -0.5)
    probs = []
    for h in range(MEM_HEADS):
        sh = s[:, h * MEM_TOKENS:(h + 1) * MEM_TOKENS]
        e = jnp.exp(sh - jnp.max(sh, axis=1, keepdims=True))
        probs.append((e * (1.0 / jnp.sum(e, axis=1, keepdims=True))).astype(BF16))
    mem_out = _dot(jnp.concatenate(probs, axis=1), mvbd_ref[0, 0])

    y = _dot(mem_out.astype(BF16), wout_mem_ref[...])
    if mix_heads:
        for h in range(mix_heads):
            y = y + _dot(mix_ref[0, h], wout_mix_ref[h])
    else:
        y = y + _dot_tn(mix_ref[0], wout_mix_ref[...])
    x1 = x_ref[0] + y

    hn = _rms(x1, ffn_g_ref[...]).astype(BF16)
    halo_row = lax.broadcasted_iota(jnp.int32, (HALO, FF_CHUNK), 0)
    x2 = x1
    for c in range(D_FF // FF_CHUNK):
        cols = slice(c * FF_CHUNK, (c + 1) * FF_CHUNK)
        u = _dot(hn, wu_ref[:, cols])
        g = _dot(hn, wg_ref[:, cols])
        prev = halo_ref[:, cols]
        halo_ref[:, cols] = g[tm - HALO:, :]
        gc = cb_ref[:, cols] + cw_ref[FFN_CONV - 1:FFN_CONV, cols] * g
        for back in range(1, FFN_CONV):
            head = jnp.where(halo_row < back, pltpu.roll(prev, back, axis=0), pltpu.roll(g[:HALO, :], back, axis=0))
            shifted = jnp.concatenate([head, pltpu.roll(g, back, axis=0)[HALO:, :]], axis=0)
            gc = gc + cw_ref[FFN_CONV - 1 - back:FFN_CONV - back, cols] * shifted
        x2 = x2 + _dot((_silu(gc) * u).astype(BF16), wd_ref[cols, :])

    if last:
        out_ref[0] = _rms(x2, final_g_ref[...])
    else:
        out_ref[0] = x2
        hkv = _rms(x2, kv_g_ref[...]).astype(BF16)
        k_ref[0] = _dot(hkv, wk_ref[...]).astype(BF16)
        vt_ref[0] = _dot_nt(wvt_ref[...], hkv).astype(BF16)
        hq = _rms(x2, nxt_g_ref[...]).astype(BF16)
        qnt_ref[0] = (_dot_nt(wqt_ref[...], hq) * (B_HEAD_DIM ** -0.5)).astype(BF16)
        qmemn_ref[0] = _dot(hq, wqmem_ref[...]).astype(BF16)


def _post_mixer(x, mix, qmem, mkbdt, mvbd, layer, wout_mix, wout_mem, ffn_g, wu, wg, cw, cb, wd, tail, last):
    bsz, seq, _ = x.shape
    tm = TM_POST
    mix_heads = mix.shape[1] if mix.ndim == 4 else 0
    if mix_heads:
        mix_spec = pl.BlockSpec((1, mix_heads, tm, mix.shape[3]), lambda b, j: (b, 0, j, 0))
    else:
        mix_spec = pl.BlockSpec((1, mix.shape[1], tm), lambda b, j: (b, 0, j))
    row_spec = lambda width: pl.BlockSpec((1, tm, width), lambda b, j: (b, j, 0))
    col_spec = lambda width: pl.BlockSpec((1, width, tm), lambda b, j: (b, 0, j))
    weights = (wout_mix, wout_mem, ffn_g, wu, wg, cw, cb, wd) + tuple(tail)
    in_specs = [
        row_spec(D_MODEL), mix_spec, row_spec(MEM_WIDTH),
        pl.BlockSpec((1, 1) + mkbdt.shape[2:], lambda b, j: (layer, b, 0, 0)),
        pl.BlockSpec((1, 1) + mvbd.shape[2:], lambda b, j: (layer, b, 0, 0)),
    ] + [_resident(w.shape) for w in weights]
    if last:
        out_specs = row_spec(D_MODEL)
        out_shape = jax.ShapeDtypeStruct((bsz, seq, D_MODEL), F32)
    else:
        out_specs = [row_spec(D_MODEL), row_spec(B_WIDTH), col_spec(B_WIDTH), col_spec(B_WIDTH), row_spec(MEM_WIDTH)]
        out_shape = [
            jax.ShapeDtypeStruct((bsz, seq, D_MODEL), F32),
            jax.ShapeDtypeStruct((bsz, seq, B_WIDTH), BF16),
            jax.ShapeDtypeStruct((bsz, B_WIDTH, seq), BF16),
            jax.ShapeDtypeStruct((bsz, B_WIDTH, seq), BF16),
            jax.ShapeDtypeStruct((bsz, seq, MEM_WIDTH), BF16),
        ]
    return pl.pallas_call(
        functools.partial(_post_mixer_kernel, mix_heads=mix_heads, last=last),
        grid=(bsz, seq // tm),
        in_specs=in_specs,
        out_specs=out_specs,
        out_shape=out_shape,
        scratch_shapes=[pltpu.VMEM((HALO, D_FF), F32)],
        compiler_params=pltpu.CompilerParams(
            dimension_semantics=("arbitrary", "arbitrary"), vmem_limit_bytes=VMEM_LIMIT),
        name="post_mixer_last" if last else "post_mixer",
    )(x, mix, qmem, mkbdt, mvbd, *weights)


def _band_bias_kernel(tbl_ref, out_ref):
    v = pl.program_id(0)
    h = pl.program_id(1)
    b = lax.broadcasted_iota(jnp.int32, (8, BIAS_LANES), 0)
    c = lax.broadcasted_iota(jnp.int32, (8, BIAS_LANES), 1)
    r_signed = jnp.where(c < TQ_HALF, c, c - BIAS_LANES)
    idx = jnp.clip(v * TQ_HALF + r_signed - b, -(CHUNK - 1), MAX_REL) + (CHUNK - 1)

    def body(e, acc):
        return jnp.where(idx == e, tbl_ref[h, e], acc)

    base = lax.fori_loop(0, REL_SIZE, body, jnp.zeros((8, BIAS_LANES), F32))
    kb = lax.broadcasted_iota(jnp.int32, (8, TQ_HALF), 0)
    r = lax.broadcasted_iota(jnp.int32, (8, TQ_HALF), 1)
    for a in range(TK_HALF // 8):
        rows = base if a == 0 else pltpu.roll(base, 8 * a, axis=1)
        dchunk = (v * TQ_HALF + r) // CHUNK - (8 * a + kb) // CHUNK
        valid = (dchunk >= 0) & (dchunk < BAND_CHUNKS)
        out_ref[0, 0, 8 * a:8 * a + 8, :] = jnp.where(valid, rows[:, :TQ_HALF], NEG_BIG)


def _band_bias(rel_table):
    return pl.pallas_call(
        _band_bias_kernel,
        grid=(N_BIAS_VARIANTS, B_HEADS),
        in_specs=[pl.BlockSpec(memory_space=pltpu.SMEM)],
        out_specs=pl.BlockSpec((1, 1, TK_HALF, TQ_HALF), lambda v, h: (v, h, 0, 0)),
        out_shape=jax.ShapeDtypeStruct((N_BIAS_VARIANTS, B_HEADS, TK_HALF, TQ_HALF), F32),
        compiler_params=pltpu.CompilerParams(dimension_semantics=("arbitrary", "arbitrary")),
        name="band_bias",
    )(rel_table)


def _band_attn_kernel(qt_ref, k_ref, vt_ref, bias_a_ref, bias_b_ref, out_ref):
    j = pl.program_id(1)
    first_half = 2 * j - (N_BIAS_VARIANTS - 1)
    start = pl.multiple_of(jnp.maximum(first_half, 0) * TQ_HALF, TQ_HALF)
    pair_width = 2 * B_HEAD_DIM
    pair_row = lax.broadcasted_iota(jnp.int32, (pair_width, TQ_BAND), 0)
    zeros_half = jnp.zeros((TQ_HALF, TQ_HALF), BF16)

    def scores(pair):
        lo = pair * pair_width
        k_pair = k_ref[0, pl.ds(start, TK_BAND), lo:lo + pair_width]
        q_pair = qt_ref[0, lo:lo + pair_width, :]
        zero = jnp.zeros_like(q_pair)
        q_sel = jnp.concatenate([jnp.where(pair_row < B_HEAD_DIM, q_pair, zero),
                                 jnp.where(pair_row >= B_HEAD_DIM, q_pair, zero)], axis=1)
        return _dot(k_pair, q_sel)

    def softmax_half(s, row0, c0, bias_ref, h):
        ts = []
        m_acc = None
        for r in range(0, TK_HALF, SOFTMAX_ROWS):
            t = s[row0 + r:row0 + r + SOFTMAX_ROWS, c0:c0 + TQ_HALF] + bias_ref[0, h, r:r + SOFTMAX_ROWS, :]
            ts.append(t)
            m_acc = t if m_acc is None else jnp.maximum(m_acc, t)
        m = jnp.max(m_acc, axis=0, keepdims=True)
        ps = []
        l_acc = None
        for t in ts:
            e = jnp.exp(t - m)
            ps.append(e.astype(BF16))
            l_acc = e if l_acc is None else l_acc + e
        return jnp.concatenate(ps, axis=0), jnp.sum(l_acc, axis=0, keepdims=True)

    def tile(off_b):
        s_next = scores(0)
        for pair in range(B_HEADS // 2):
            s = s_next
            if pair + 1 < B_HEADS // 2:
                s_next = scores(pair + 1)
            cols, sums = [], []
            for parity in range(2):
                h = 2 * pair + parity
                c0 = parity * TQ_BAND
                p_a, l_a = softmax_half(s, 0, c0, bias_a_ref, h)
                p_b, l_b = softmax_half(s, off_b, c0 + TQ_HALF, bias_b_ref, h)
                cols.append(jnp.concatenate([p_a, zeros_half], axis=0))
                cols.append(jnp.concatenate([zeros_half, p_b] if off_b else [p_b, zeros_half], axis=0))
                sums += [l_a, l_b]
            lo = pair * pair_width
            vt = vt_ref[0, lo:lo + pair_width, pl.ds(start, TK_BAND)]
            o = _dot(vt, jnp.concatenate(cols, axis=1))
            inv = 1.0 / jnp.concatenate(sums, axis=1)
            out_ref[0, lo:lo + B_HEAD_DIM, :] = (o[:B_HEAD_DIM, :TQ_BAND] * inv[:, :TQ_BAND]).astype(BF16)
            out_ref[0, lo + B_HEAD_DIM:lo + pair_width, :] = (
                o[B_HEAD_DIM:, TQ_BAND:] * inv[:, TQ_BAND:]).astype(BF16)

    pl.when(first_half >= 0)(functools.partial(tile, TQ_HALF))
    pl.when(first_half < 0)(functools.partial(tile, 0))


def _band_attn(qt, k, vt, bias):
    bsz, seq, _ = k.shape
    nv = N_BIAS_VARIANTS
    bias_spec = lambda half: pl.BlockSpec(
        (1, B_HEADS, TK_HALF, TQ_HALF), lambda b, j: (jnp.minimum(2 * j + half, nv - 1), 0, 0, 0))
    return pl.pallas_call(
        _band_attn_kernel,
        grid=(bsz, seq // TQ_BAND),
        in_specs=[
            pl.BlockSpec((1, B_WIDTH, TQ_BAND), lambda b, j: (b, 0, j)),
            pl.BlockSpec((1, seq, B_WIDTH), lambda b, j: (b, 0, 0)),
            pl.BlockSpec((1, B_WIDTH, seq), lambda b, j: (b, 0, 0)),
            bias_spec(0), bias_spec(1),
        ],
        out_specs=pl.BlockSpec((1, B_WIDTH, TQ_BAND), lambda b, j: (b, 0, j)),
        out_shape=jax.ShapeDtypeStruct((bsz, B_WIDTH, seq), BF16),
        compiler_params=pltpu.CompilerParams(
            dimension_semantics=("arbitrary", "arbitrary"), vmem_limit_bytes=VMEM_LIMIT),
        name="band_attn",
    )(qt, k, vt, bias, bias)


def kernel(x, mem, norm_mix_g, norm_ffn_g, a_w_in, a_gate_b, a_conv_w, a_conv_b, a_head_g, a_w_out, kv_norm_g, w_kv,
           b_w_in, b_rel_bias, b_w_out, mem_w_kv, ffn_w_up, ffn_conv_w, ffn_conv_b, ffn_w_down, final_g):
    bsz, seq, d = x.shape
    assert d == D_MODEL and seq % max(TM_IN, L_MLSTM, TM_POST, TQ_BAND) == 0
    assert a_w_in.shape[0] == 1 and b_w_in.shape[0] == 1, "one mLSTM layer followed by one band-attention layer"
    row = lambda g: g.reshape(1, -1).astype(F32)

    wkt = jnp.swapaxes(mem_w_kv[:, :, :MEM_WIDTH], 1, 2).astype(BF16)
    wmv = mem_w_kv[:, :, MEM_WIDTH:].astype(BF16)
    mkbdt, mvbd = _mem_kv(mem.astype(BF16), wkt, wmv)

    w = a_w_in[0]
    heads = lambda m: jnp.swapaxes(m.reshape(D_MODEL, A_HEADS, A_HEAD_DIM), 0, 1)
    wq = heads(w[:, 0:A_WIDTH]).astype(BF16)
    wk = heads(w[:, A_WIDTH:2 * A_WIDTH]).astype(BF16)
    wv = jnp.pad(heads(w[:, 2 * A_WIDTH:3 * A_WIDTH]), ((0, 0), (0, 0), (0, V_AUG - A_HEAD_DIM))).astype(BF16)
    wo = heads(w[:, 3 * A_WIDTH:4 * A_WIDTH]).astype(BF16)
    n_gate = 2 * A_HEADS
    wgate = jnp.pad(w[:, 4 * A_WIDTH:4 * A_WIDTH + n_gate], ((0, 0), (0, GATE_LANES - n_gate))).astype(BF16)
    wmem = w[:, 4 * A_WIDTH + n_gate:].astype(BF16)
    gate_b = jnp.pad(a_gate_b[0].astype(F32), (0, GATE_LANES - n_gate)).reshape(1, GATE_LANES)
    vone = (jnp.arange(V_AUG) == A_HEAD_DIM).astype(F32).reshape(1, V_AUG)
    cw = jnp.transpose(a_conv_w[0].reshape(A_CONV, 2 * A_HEADS, A_HEAD_DIM), (1, 0, 2)).astype(F32)
    cb = a_conv_b[0].reshape(2 * A_HEADS, 1, A_HEAD_DIM).astype(F32)
    q, k, v, o, gates, qmem = _a_in(x, row(norm_mix_g[0]), wq, wk, wv, wo, wgate, wmem, gate_b, vone, cw, cb)

    head_g = a_head_g[0].reshape(A_HEADS, 1, A_HEAD_DIM).astype(F32)
    mix = _mlstm(q, k, v, o, gates, head_g)

    def ffn_weights(l):
        return (row(norm_ffn_g[l]), ffn_w_up[l][:, :D_FF].astype(BF16), ffn_w_up[l][:, D_FF:].astype(BF16),
                ffn_conv_w[l].astype(F32), row(ffn_conv_b[l]), ffn_w_down[l].astype(BF16))

    wout = a_w_out[0]
    tail = (row(kv_norm_g), w_kv[:, :B_WIDTH].astype(BF16), jnp.swapaxes(w_kv[:, B_WIDTH:], 0, 1).astype(BF16),
            row(norm_mix_g[1]), jnp.swapaxes(b_w_in[0][:, :B_WIDTH], 0, 1).astype(BF16),
            b_w_in[0][:, B_WIDTH:].astype(BF16))
    x1, kb, vbt, qbt, qmem_b = _post_mixer(
        x, mix, qmem, mkbdt, mvbd, 0,
        wout[:A_WIDTH].reshape(A_HEADS, A_HEAD_DIM, D_MODEL).astype(BF16), wout[A_WIDTH:].astype(BF16),
        *ffn_weights(0), tail=tail, last=False)

    bias = _band_bias(b_rel_bias[0].astype(F32))
    mix_b = _band_attn(qbt, kb, vbt, bias)
    wout = b_w_out[0]
    return _post_mixer(
        x1, mix_b, qmem_b, mkbdt, mvbd, 1,
        wout[:B_WIDTH].astype(BF16), wout[B_WIDTH:].astype(BF16),
        *ffn_weights(1), tail=(row(final_g),), last=True)
```

```python
import functools
import math

import jax
import jax.numpy as jnp
from jax import lax
from jax.experimental import pallas as pl
from jax.experimental.pallas import tpu as pltpu

F32 = jnp.float32
BF16 = jnp.bfloat16

D_MODEL = 1024
CHUNK = 64
MEM_TOKENS = 256
MEM_HEADS = 4
MEM_WIDTH = 256
MEM_HEAD_DIM = 64
A_WIDTH = 768
A_HEADS = 4
A_HEAD_DIM = 192
A_CONV = 4
B_HEADS = 12
B_HEAD_DIM = 64
B_WIDTH = 768
BAND_CHUNKS = 9
MAX_REL = 128
REL_SIZE = MAX_REL + CHUNK
D_FF = 2816
FFN_CONV = 3
EPS = 1e-6

V_AUG = 256
GATE_LANES = 128
HALO = 8
NEG_BIG = -1e30

TM_IN = 256
L_MLSTM = 256
TM_POST = 512
FF_CHUNK = 1408
TQ_HALF = 128
TQ_BAND = 2 * TQ_HALF
BAND_PREV = (BAND_CHUNKS - 1) * CHUNK
TK_HALF = BAND_PREV + TQ_HALF
TK_BAND = BAND_PREV + TQ_BAND
N_BIAS_VARIANTS = BAND_PREV // TQ_HALF + 1
BIAS_LANES = TK_HALF + TQ_HALF
SOFTMAX_ROWS = 64

VMEM_LIMIT = 56 * 1024 * 1024


def _resident(shape):
    nd = len(shape)
    return pl.BlockSpec(shape, lambda *_: (0,) * nd, pipeline_mode=pl.Buffered(1))


def _rms(x, g):
    return x * lax.rsqrt(jnp.mean(x * x, axis=-1, keepdims=True) + EPS) * g


def _dot(a, b):
    return jnp.dot(a, b, preferred_element_type=F32)


def _dot_nt(a, b):
    return lax.dot_general(a, b, (((1,), (1,)), ((), ())), preferred_element_type=F32)


def _dot_tn(a, b):
    return lax.dot_general(a, b, (((0,), (0,)), ((), ())), preferred_element_type=F32)


def _sigmoid(x):
    return 0.5 + 0.5 * jnp.tanh(0.5 * x)


def _silu(x):
    half = 0.5 * x
    return half + half * jnp.tanh(half)


def _mem_kv_kernel(mem_ref, wkt_ref, wv_ref, mkbdt_ref, mvbd_ref):
    m = mem_ref[0]
    mkt = _dot_nt(wkt_ref[0], m)
    mv = _dot(m, wv_ref[0])
    f_idx = lax.broadcasted_iota(jnp.int32, (MEM_WIDTH, MEM_HEADS * MEM_TOKENS), 0)
    c_idx = lax.broadcasted_iota(jnp.int32, (MEM_WIDTH, MEM_HEADS * MEM_TOKENS), 1)
    mkt4 = jnp.concatenate([mkt] * MEM_HEADS, axis=1)
    mkbdt_ref[0, 0] = jnp.where(f_idx // MEM_HEAD_DIM == c_idx // MEM_TOKENS, mkt4, 0.0).astype(BF16)
    r_idx = lax.broadcasted_iota(jnp.int32, (MEM_HEADS * MEM_TOKENS, MEM_WIDTH), 0)
    g_idx = lax.broadcasted_iota(jnp.int32, (MEM_HEADS * MEM_TOKENS, MEM_WIDTH), 1)
    mv4 = jnp.concatenate([mv] * MEM_HEADS, axis=0)
    mvbd_ref[0, 0] = jnp.where(r_idx // MEM_TOKENS == g_idx // MEM_HEAD_DIM, mv4, 0.0).astype(BF16)


def _mem_kv(mem_bf, wkt, wv):
    depth = wkt.shape[0]
    bsz = mem_bf.shape[0]
    return pl.pallas_call(
        _mem_kv_kernel,
        grid=(depth, bsz),
        in_specs=[
            pl.BlockSpec((1, MEM_TOKENS, D_MODEL), lambda l, b: (b, 0, 0)),
            pl.BlockSpec((1, MEM_WIDTH, D_MODEL), lambda l, b: (l, 0, 0)),
            pl.BlockSpec((1, D_MODEL, MEM_WIDTH), lambda l, b: (l, 0, 0)),
        ],
        out_specs=[
            pl.BlockSpec((1, 1, MEM_WIDTH, MEM_HEADS * MEM_TOKENS), lambda l, b: (l, b, 0, 0)),
            pl.BlockSpec((1, 1, MEM_HEADS * MEM_TOKENS, MEM_WIDTH), lambda l, b: (l, b, 0, 0)),
        ],
        out_shape=[
            jax.ShapeDtypeStruct((depth, bsz, MEM_WIDTH, MEM_HEADS * MEM_TOKENS), BF16),
            jax.ShapeDtypeStruct((depth, bsz, MEM_HEADS * MEM_TOKENS, MEM_WIDTH), BF16),
        ],
        compiler_params=pltpu.CompilerParams(dimension_semantics=("arbitrary", "arbitrary")),
        name="mem_kv",
    )(mem_bf, wkt, wv)


def _causal_taps(x, prev, taps):
    n_taps = taps.shape[0]
    halo_row = lax.broadcasted_iota(jnp.int32, (HALO, x.shape[1]), 0)
    y = taps[n_taps - 1:n_taps, :] * x
    for back in range(1, n_taps):
        head = jnp.where(halo_row < back, pltpu.roll(prev, back, axis=0), pltpu.roll(x[:HALO, :], back, axis=0))
        shifted = jnp.concatenate([head, pltpu.roll(x, back, axis=0)[HALO:, :]], axis=0)
        y = y + taps[n_taps - 1 - back:n_taps - back, :] * shifted
    return y


def _lane_scan(x, op, identity):
    lane = lax.broadcasted_iota(jnp.int32, x.shape, 1)
    step = 1
    while step < x.shape[1]:
        x = op(x, jnp.where(lane >= step, pltpu.roll(x, step, axis=1), identity))
        step *= 2
    return x


def _a_in_kernel(x_ref, g_ref, wqk_ref, wv_ref, wo_ref, wgm_ref, gate_b_ref, vone_ref, cw_ref, cb_ref,
                 q_ref, k_ref, v_ref, o_ref, arow_ref, cols_ref, decay_ref, qmem_ref, halo_ref, m_ref):
    tm = x_ref.shape[1]

    @pl.when(pl.program_id(1) == 0)
    def _():
        halo_ref[...] = jnp.zeros(halo_ref.shape, F32)
        m_ref[...] = jnp.zeros(m_ref.shape, F32)

    hn = _rms(x_ref[0], g_ref[...]).astype(BF16)

    gm = _dot(hn, wgm_ref[...])
    qmem_ref[0] = gm[:, 2 * GATE_LANES:].astype(BF16)
    gi = (gm[:, :GATE_LANES] + gate_b_ref[:, :GATE_LANES]).T[0:8, :]
    fg = (gm[:, GATE_LANES:2 * GATE_LANES] + gate_b_ref[:, GATE_LANES:]).T[0:8, :]
    head_row = lax.broadcasted_iota(jnp.int32, (8, tm), 0) < A_HEADS
    logf = jnp.where(head_row, jnp.minimum(fg, 0.0) - jnp.log(1.0 + jnp.exp(-jnp.abs(fg))), 0.0)
    b = _lane_scan(logf, jnp.add, 0.0)
    a = gi - b
    m_prev = m_ref[:, 0:1]
    g = jnp.maximum(_lane_scan(a, jnp.maximum, -jnp.inf), m_prev)
    g_last = g[:, tm - 1:tm]
    m_ref[...] = jnp.broadcast_to(b[:, tm - 1:tm] + g_last, m_ref.shape)
    arow_ref[0] = a
    decay_ref[0, 0] = jnp.broadcast_to(jnp.exp(m_prev - g_last), decay_ref.shape[2:])
    terms = [g, jnp.exp(m_prev - g), jnp.exp(-(b + g)), jnp.exp(a - g_last)]
    pad = jnp.zeros((GATE_LANES - 8 * len(terms), tm), F32)
    cols_ref[0] = jnp.concatenate(terms + [pad], axis=0).T

    raw = _dot(hn, wqk_ref[...])
    prev = halo_ref[...]
    halo_ref[...] = raw[tm - HALO:, :]
    qk = _silu(_causal_taps(raw, prev, cw_ref[...]) + cb_ref[...])
    o_half = 0.5 * _dot(hn, wo_ref[...])
    for h in range(A_HEADS):
        lo = h * A_HEAD_DIM
        q_ref[0, h] = qk[:, lo:lo + A_HEAD_DIM].astype(BF16)
        k_ref[0, h] = (qk[:, A_WIDTH + lo:A_WIDTH + lo + A_HEAD_DIM] * (A_HEAD_DIM ** -0.5)).astype(BF16)
        o_ref[0, h] = o_half[:, lo:lo + A_HEAD_DIM]
        v_ref[0, h] = (_dot(hn, wv_ref[h]) + vone_ref[...]).astype(BF16)


def _a_in(x, g, wqk, wv, wo, wgm, gate_b, vone, cw, cb):
    bsz, seq, _ = x.shape
    tm = TM_IN
    head_spec = lambda width: pl.BlockSpec((1, A_HEADS, tm, width), lambda b, j: (b, 0, j, 0))
    row_spec = lambda width: pl.BlockSpec((1, tm, width), lambda b, j: (b, j, 0))
    weights = (g, wqk, wv, wo, wgm, gate_b, vone, cw, cb)
    return pl.pallas_call(
        _a_in_kernel,
        grid=(bsz, seq // tm),
        in_specs=[row_spec(D_MODEL)] + [_resident(w.shape) for w in weights],
        out_specs=[
            head_spec(A_HEAD_DIM), head_spec(A_HEAD_DIM), head_spec(V_AUG), head_spec(A_HEAD_DIM),
            pl.BlockSpec((1, 8, tm), lambda b, j: (b, 0, j)),
            row_spec(GATE_LANES),
            pl.BlockSpec((1, 1, 8, GATE_LANES), lambda b, j: (b, j, 0, 0)),
            row_spec(MEM_WIDTH),
        ],
        out_shape=[
            jax.ShapeDtypeStruct((bsz, A_HEADS, seq, A_HEAD_DIM), BF16),
            jax.ShapeDtypeStruct((bsz, A_HEADS, seq, A_HEAD_DIM), BF16),
            jax.ShapeDtypeStruct((bsz, A_HEADS, seq, V_AUG), BF16),
            jax.ShapeDtypeStruct((bsz, A_HEADS, seq, A_HEAD_DIM), F32),
            jax.ShapeDtypeStruct((bsz, 8, seq), F32),
            jax.ShapeDtypeStruct((bsz, seq, GATE_LANES), F32),
            jax.ShapeDtypeStruct((bsz, seq // tm, 8, GATE_LANES), F32),
            jax.ShapeDtypeStruct((bsz, seq, MEM_WIDTH), BF16),
        ],
        scratch_shapes=[pltpu.VMEM((HALO, 2 * A_WIDTH), F32), pltpu.VMEM((8, GATE_LANES), F32)],
        compiler_params=pltpu.CompilerParams(
            dimension_semantics=("arbitrary", "arbitrary"), vmem_limit_bytes=VMEM_LIMIT),
        name="a_in",
    )(x, *weights)


def _mlstm_kernel(q_ref, k_ref, v_ref, o_ref, arow_ref, cols_ref, decay_ref, hg_ref, out_ref, c_ref):
    L = q_ref.shape[2]
    half = L // 2

    @pl.when(pl.program_id(1) == 0)
    def _():
        c_ref[...] = jnp.zeros(c_ref.shape, F32)

    a_rows = arow_ref[0]
    cols = cols_ref[0]
    g_cols, inter_cols, en_cols, wexp_cols = (cols[:, 8 * i:8 * i + 8] for i in range(4))
    decay = decay_ref[0, 0]

    tri_half = (lax.broadcasted_iota(jnp.int32, (half, half), 0)
                >= lax.broadcasted_iota(jnp.int32, (half, half), 1))
    lane_t = lax.broadcasted_iota(jnp.int32, (L, V_AUG - 128), 1)
    ones_lane = A_HEAD_DIM - 128

    for h in range(A_HEADS):
        g_col = g_cols[:, h:h + 1]
        a_row = a_rows[h:h + 1, :]
        qh = q_ref[0, h]
        kh = k_ref[0, h]
        vh = v_ref[0, h]
        p00 = jnp.exp(jnp.where(tri_half, a_row[:, :half] - g_col[:half], -jnp.inf))
        p10 = jnp.exp(a_row[:, :half] - g_col[half:])
        p11 = jnp.exp(jnp.where(tri_half, a_row[:, half:] - g_col[half:], -jnp.inf))
        s_top = (_dot_nt(qh[:half], kh[:half]) * p00).astype(BF16)
        s_bot = (_dot_nt(qh[half:], kh) * jnp.concatenate([p10, p11], axis=1)).astype(BF16)
        c_prev = c_ref[h]
        qc = _dot(qh, c_prev.astype(BF16)) * inter_cols[:, h:h + 1]
        num = jnp.concatenate([_dot(s_top, vh[:half]), _dot(s_bot, vh)], axis=0) + qc
        hi_tile = num[:, 128:]
        den = jnp.sum(jnp.where(lane_t == ones_lane, hi_tile, 0.0), axis=1, keepdims=True)
        r = 1.0 / jnp.maximum(jnp.abs(den), en_cols[:, h:h + 1])
        ssq = (jnp.sum(num[:, :128] * num[:, :128], axis=1, keepdims=True)
               + jnp.sum(jnp.where(lane_t < ones_lane, hi_tile * hi_tile, 0.0), axis=1, keepdims=True))
        factor = r * lax.rsqrt(r * r * ssq * (1.0 / A_HEAD_DIM) + EPS)
        gate = hg_ref[h] + hg_ref[h] * jnp.tanh(o_ref[0, h])
        out_ref[0, h] = (num[:, :A_HEAD_DIM] * factor * gate).astype(BF16)

        wv = (wexp_cols[:, h:h + 1] * vh.astype(F32)).astype(BF16)
        c_ref[h] = decay[h:h + 1, 0:1] * c_prev + _dot_tn(kh, wv)


def _mlstm(q, k, v, o_half, a_rows, cols, decay, head_g_half):
    bsz, nh, seq, dh = q.shape
    L = L_MLSTM
    head_spec = lambda width: pl.BlockSpec((1, nh, L, width), lambda b, j: (b, 0, j, 0))
    return pl.pallas_call(
        _mlstm_kernel,
        grid=(bsz, seq // L),
        in_specs=[
            head_spec(dh), head_spec(dh), head_spec(V_AUG), head_spec(dh),
            pl.BlockSpec((1, 8, L), lambda b, j: (b, 0, j)),
            pl.BlockSpec((1, L, GATE_LANES), lambda b, j: (b, j, 0)),
            pl.BlockSpec((1, 1, 8, GATE_LANES), lambda b, j: (b, j, 0, 0)),
            _resident(head_g_half.shape),
        ],
        out_specs=head_spec(dh),
        out_shape=jax.ShapeDtypeStruct((bsz, nh, seq, dh), BF16),
        scratch_shapes=[pltpu.VMEM((nh, dh, V_AUG), F32)],
        compiler_params=pltpu.CompilerParams(
            dimension_semantics=("arbitrary", "arbitrary"), vmem_limit_bytes=VMEM_LIMIT),
        name="mlstm",
    )(q, k, v, o_half, a_rows, cols, decay, head_g_half)


def _post_mixer_kernel(*refs, mix_heads, last):
    if last:
        (x_ref, mix_ref, qmem_ref, mkbdt_ref, mvbd_ref, wout_mix_ref, wout_mem_ref, ffn_g_ref, wu_ref, wg_ref,
         cw_ref, cb_ref, wd_ref, final_g_ref, out_ref, halo_ref) = refs
    else:
        (x_ref, mix_ref, qmem_ref, mkbdt_ref, mvbd_ref, wout_mix_ref, wout_mem_ref, ffn_g_ref, wu_ref, wg_ref,
         cw_ref, cb_ref, wd_ref, kv_g_ref, wk_ref, wvt_ref, nxt_g_ref, wqt_ref, wqmem_ref,
         out_ref, k_ref, vt_ref, qnt_ref, qmemn_ref, halo_ref) = refs
    tm = x_ref.shape[1]

    @pl.when(pl.program_id(1) == 0)
    def _():
        halo_ref[...] = jnp.zeros((HALO, D_FF), F32)

    s = _dot(qmem_ref[0], mkbdt_ref[0, 0]) * (MEM_HEAD_DIM ** -0.5)
    probs = []
    for h in range(MEM_HEADS):
        sh = s[:, h * MEM_TOKENS:(h + 1) * MEM_TOKENS]
        e = jnp.exp(sh - jnp.max(sh, axis=1, keepdims=True))
        probs.append((e * (1.0 / jnp.sum(e, axis=1, keepdims=True))).astype(BF16))
    mem_out = _dot(jnp.concatenate(probs, axis=1), mvbd_ref[0, 0])

    y = _dot(mem_out.astype(BF16), wout_mem_ref[...])
    if mix_heads:
        for h in range(mix_heads):
            y = y + _dot(mix_ref[0, h], wout_mix_ref[h])
    else:
        y = y + _dot_tn(mix_ref[0], wout_mix_ref[...])
    x1 = x_ref[0] + y

    hn = _rms(x1, ffn_g_ref[...]).astype(BF16)
    x2 = x1
    for c in range(D_FF // FF_CHUNK):
        cols = slice(c * FF_CHUNK, (c + 1) * FF_CHUNK)
        u = _dot(hn, wu_ref[:, cols])
        g = _dot(hn, wg_ref[:, cols])
        prev = halo_ref[:, cols]
        halo_ref[:, cols] = g[tm - HALO:, :]
        gc = _causal_taps(g, prev, cw_ref[:, cols]) + cb_ref[:, cols]
        x2 = x2 + _dot((_silu(gc) * u).astype(BF16), wd_ref[cols, :])

    if last:
        out_ref[0] = _rms(x2, final_g_ref[...])
    else:
        out_ref[0] = x2
        hkv = _rms(x2, kv_g_ref[...]).astype(BF16)
        k_ref[0] = _dot(hkv, wk_ref[...]).astype(BF16)
        vt_ref[0] = _dot_nt(wvt_ref[...], hkv).astype(BF16)
        hq = _rms(x2, nxt_g_ref[...]).astype(BF16)
        qnt_ref[0] = (_dot_nt(wqt_ref[...], hq) * (B_HEAD_DIM ** -0.5)).astype(BF16)
        qmemn_ref[0] = _dot(hq, wqmem_ref[...]).astype(BF16)


def _post_mixer(x, mix, qmem, mkbdt, mvbd, layer, wout_mix, wout_mem, ffn_g, wu, wg, cw, cb, wd, tail, last):
    bsz, seq, _ = x.shape
    tm = TM_POST
    mix_heads = mix.shape[1] if mix.ndim == 4 else 0
    if mix_heads:
        mix_spec = pl.BlockSpec((1, mix_heads, tm, mix.shape[3]), lambda b, j: (b, 0, j, 0))
    else:
        mix_spec = pl.BlockSpec((1, mix.shape[1], tm), lambda b, j: (b, 0, j))
    row_spec = lambda width: pl.BlockSpec((1, tm, width), lambda b, j: (b, j, 0))
    col_spec = lambda width: pl.BlockSpec((1, width, tm), lambda b, j: (b, 0, j))
    weights = (wout_mix, wout_mem, ffn_g, wu, wg, cw, cb, wd) + tuple(tail)
    in_specs = [
        row_spec(D_MODEL), mix_spec, row_spec(MEM_WIDTH),
        pl.BlockSpec((1, 1) + mkbdt.shape[2:], lambda b, j: (layer, b, 0, 0)),
        pl.BlockSpec((1, 1) + mvbd.shape[2:], lambda b, j: (layer, b, 0, 0)),
    ] + [_resident(w.shape) for w in weights]
    if last:
        out_specs = row_spec(D_MODEL)
        out_shape = jax.ShapeDtypeStruct((bsz, seq, D_MODEL), F32)
    else:
        out_specs = [row_spec(D_MODEL), row_spec(B_WIDTH), col_spec(B_WIDTH), col_spec(B_WIDTH), row_spec(MEM_WIDTH)]
        out_shape = [
            jax.ShapeDtypeStruct((bsz, seq, D_MODEL), F32),
            jax.ShapeDtypeStruct((bsz, seq, B_WIDTH), BF16),
            jax.ShapeDtypeStruct((bsz, B_WIDTH, seq), BF16),
            jax.ShapeDtypeStruct((bsz, B_WIDTH, seq), BF16),
            jax.ShapeDtypeStruct((bsz, seq, MEM_WIDTH), BF16),
        ]
    return pl.pallas_call(
        functools.partial(_post_mixer_kernel, mix_heads=mix_heads, last=last),
        grid=(bsz, seq // tm),
        in_specs=in_specs,
        out_specs=out_specs,
        out_shape=out_shape,
        scratch_shapes=[pltpu.VMEM((HALO, D_FF), F32)],
        compiler_params=pltpu.CompilerParams(
            dimension_semantics=("arbitrary", "arbitrary"), vmem_limit_bytes=VMEM_LIMIT),
        name="post_mixer_last" if last else "post_mixer",
    )(x, mix, qmem, mkbdt, mvbd, *weights)


def _band_bias_kernel(tbl_ref, out_ref):
    v = pl.program_id(0)
    h = pl.program_id(1)
    b = lax.broadcasted_iota(jnp.int32, (8, BIAS_LANES), 0)
    c = lax.broadcasted_iota(jnp.int32, (8, BIAS_LANES), 1)
    r_signed = jnp.where(c < TQ_HALF, c, c - BIAS_LANES)
    idx = jnp.clip(v * TQ_HALF + r_signed - b, -(CHUNK - 1), MAX_REL) + (CHUNK - 1)

    def body(e, acc):
        return jnp.where(idx == e, tbl_ref[h, e], acc)

    base = lax.fori_loop(0, REL_SIZE, body, jnp.zeros((8, BIAS_LANES), F32))
    kb = lax.broadcasted_iota(jnp.int32, (8, TQ_HALF), 0)
    r = lax.broadcasted_iota(jnp.int32, (8, TQ_HALF), 1)
    for a in range(TK_HALF // 8):
        rows = base if a == 0 else pltpu.roll(base, 8 * a, axis=1)
        dchunk = (v * TQ_HALF + r) // CHUNK - (8 * a + kb) // CHUNK
        valid = (dchunk >= 0) & (dchunk < BAND_CHUNKS)
        out_ref[0, 0, 8 * a:8 * a + 8, :] = jnp.where(valid, rows[:, :TQ_HALF], NEG_BIG)


def _band_bias(rel_table):
    return pl.pallas_call(
        _band_bias_kernel,
        grid=(N_BIAS_VARIANTS, B_HEADS),
        in_specs=[pl.BlockSpec(memory_space=pltpu.SMEM)],
        out_specs=pl.BlockSpec((1, 1, TK_HALF, TQ_HALF), lambda v, h: (v, h, 0, 0)),
        out_shape=jax.ShapeDtypeStruct((N_BIAS_VARIANTS, B_HEADS, TK_HALF, TQ_HALF), F32),
        compiler_params=pltpu.CompilerParams(dimension_semantics=("arbitrary", "arbitrary")),
        name="band_bias",
    )(rel_table)


def _band_attn_kernel(qt_ref, k_ref, vt_ref, bias_a_ref, bias_b_ref, out_ref):
    j = pl.program_id(1)
    first_half = 2 * j - (N_BIAS_VARIANTS - 1)
    start = pl.multiple_of(jnp.maximum(first_half, 0) * TQ_HALF, TQ_HALF)
    pair_width = 2 * B_HEAD_DIM
    pair_row = lax.broadcasted_iota(jnp.int32, (pair_width, TQ_BAND), 0)
    zeros_half = jnp.zeros((TQ_HALF, TQ_HALF), BF16)

    def scores(pair):
        lo = pair * pair_width
        k_pair = k_ref[0, pl.ds(start, TK_BAND), lo:lo + pair_width]
        q_pair = qt_ref[0, lo:lo + pair_width, :]
        zero = jnp.zeros_like(q_pair)
        q_sel = jnp.concatenate([jnp.where(pair_row < B_HEAD_DIM, q_pair, zero),
                                 jnp.where(pair_row >= B_HEAD_DIM, q_pair, zero)], axis=1)
        return _dot(k_pair, q_sel)

    def softmax_half(s, row0, c0, bias_ref, h):
        ts = []
        m_acc = None
        for r in range(0, TK_HALF, SOFTMAX_ROWS):
            t = s[row0 + r:row0 + r + SOFTMAX_ROWS, c0:c0 + TQ_HALF] + bias_ref[0, h, r:r + SOFTMAX_ROWS, :]
            ts.append(t)
            m_acc = t if m_acc is None else jnp.maximum(m_acc, t)
        m = jnp.max(m_acc, axis=0, keepdims=True)
        ps = []
        l_acc = None
        for t in ts:
            e = jnp.exp(t - m)
            ps.append(e.astype(BF16))
            l_acc = e if l_acc is None else l_acc + e
        return jnp.concatenate(ps, axis=0), jnp.sum(l_acc, axis=0, keepdims=True)

    def tile(off_b):
        s_next = scores(0)
        for pair in range(B_HEADS // 2):
            s = s_next
            if pair + 1 < B_HEADS // 2:
                s_next = scores(pair + 1)
            cols, sums = [], []
            for parity in range(2):
                h = 2 * pair + parity
                c0 = parity * TQ_BAND
                p_a, l_a = softmax_half(s, 0, c0, bias_a_ref, h)
                p_b, l_b = softmax_half(s, off_b, c0 + TQ_HALF, bias_b_ref, h)
                cols.append(jnp.concatenate([p_a, zeros_half], axis=0))
                cols.append(jnp.concatenate([zeros_half, p_b] if off_b else [p_b, zeros_half], axis=0))
                sums += [l_a, l_b]
            lo = pair * pair_width
            vt = vt_ref[0, lo:lo + pair_width, pl.ds(start, TK_BAND)]
            o = _dot(vt, jnp.concatenate(cols, axis=1))
            inv = 1.0 / jnp.concatenate(sums, axis=1)
            out_ref[0, lo:lo + B_HEAD_DIM, :] = (o[:B_HEAD_DIM, :TQ_BAND] * inv[:, :TQ_BAND]).astype(BF16)
            out_ref[0, lo + B_HEAD_DIM:lo + pair_width, :] = (
                o[B_HEAD_DIM:, TQ_BAND:] * inv[:, TQ_BAND:]).astype(BF16)

    pl.when(first_half >= 0)(functools.partial(tile, TQ_HALF))
    pl.when(first_half < 0)(functools.partial(tile, 0))


def _band_attn(qt, k, vt, bias):
    bsz, seq, _ = k.shape
    nv = N_BIAS_VARIANTS
    bias_spec = lambda half: pl.BlockSpec(
        (1, B_HEADS, TK_HALF, TQ_HALF), lambda b, j: (jnp.minimum(2 * j + half, nv - 1), 0, 0, 0))
    return pl.pallas_call(
        _band_attn_kernel,
        grid=(bsz, seq // TQ_BAND),
        in_specs=[
            pl.BlockSpec((1, B_WIDTH, TQ_BAND), lambda b, j: (b, 0, j)),
            pl.BlockSpec((1, seq, B_WIDTH), lambda b, j: (b, 0, 0)),
            pl.BlockSpec((1, B_WIDTH, seq), lambda b, j: (b, 0, 0)),
            bias_spec(0), bias_spec(1),
        ],
        out_specs=pl.BlockSpec((1, B_WIDTH, TQ_BAND), lambda b, j: (b, 0, j)),
        out_shape=jax.ShapeDtypeStruct((bsz, B_WIDTH, seq), BF16),
        compiler_params=pltpu.CompilerParams(
            dimension_semantics=("arbitrary", "arbitrary"), vmem_limit_bytes=VMEM_LIMIT),
        name="band_attn",
    )(qt, k, vt, bias, bias)


def kernel(x, mem, norm_mix_g, norm_ffn_g, a_w_in, a_gate_b, a_conv_w, a_conv_b, a_head_g, a_w_out, kv_norm_g, w_kv,
           b_w_in, b_rel_bias, b_w_out, mem_w_kv, ffn_w_up, ffn_conv_w, ffn_conv_b, ffn_w_down, final_g):
    bsz, seq, d = x.shape
    assert d == D_MODEL and seq % max(TM_IN, L_MLSTM, TM_POST, TQ_BAND) == 0
    assert a_w_in.shape[0] == 1 and b_w_in.shape[0] == 1, "one mLSTM layer followed by one band-attention layer"
    assert TM_IN == L_MLSTM, "the input projection emits per-chunk decay terms, one tile per mLSTM chunk"
    row = lambda g: g.reshape(1, -1).astype(F32)

    wkt = jnp.swapaxes(mem_w_kv[:, :, :MEM_WIDTH], 1, 2).astype(BF16)
    wmv = mem_w_kv[:, :, MEM_WIDTH:].astype(BF16)
    mkbdt, mvbd = _mem_kv(mem.astype(BF16), wkt, wmv)

    w = a_w_in[0]
    heads = lambda m: jnp.swapaxes(m.reshape(D_MODEL, A_HEADS, A_HEAD_DIM), 0, 1)
    wqk = w[:, 0:2 * A_WIDTH].astype(BF16)
    wv = jnp.pad(heads(w[:, 2 * A_WIDTH:3 * A_WIDTH]), ((0, 0), (0, 0), (0, V_AUG - A_HEAD_DIM))).astype(BF16)
    wo = w[:, 3 * A_WIDTH:4 * A_WIDTH].astype(BF16)
    gate_pad = ((0, 0), (0, GATE_LANES - A_HEADS))
    g0 = 4 * A_WIDTH
    wgm = jnp.concatenate([jnp.pad(w[:, g0:g0 + A_HEADS], gate_pad), jnp.pad(w[:, g0 + A_HEADS:g0 + 2 * A_HEADS], gate_pad),
                           w[:, g0 + 2 * A_HEADS:]], axis=1).astype(BF16)
    gate_b = jnp.pad(a_gate_b[0].astype(F32).reshape(2, A_HEADS), gate_pad).reshape(1, 2 * GATE_LANES)
    vone = (jnp.arange(V_AUG) == A_HEAD_DIM).astype(F32).reshape(1, V_AUG)
    q, k, v, o_half, a_rows, gate_cols, decay, qmem = _a_in(
        x, row(norm_mix_g[0]), wqk, wv, wo, wgm, gate_b, vone, a_conv_w[0].astype(F32), row(a_conv_b[0]))

    head_g_half = 0.5 * a_head_g[0].reshape(A_HEADS, 1, A_HEAD_DIM).astype(F32)
    mix = _mlstm(q, k, v, o_half, a_rows, gate_cols, decay, head_g_half)

    def ffn_weights(l):
        return (row(norm_ffn_g[l]), ffn_w_up[l][:, :D_FF].astype(BF16), ffn_w_up[l][:, D_FF:].astype(BF16),
                ffn_conv_w[l].astype(F32), row(ffn_conv_b[l]), ffn_w_down[l].astype(BF16))

    wout = a_w_out[0]
    tail = (row(kv_norm_g), w_kv[:, :B_WIDTH].astype(BF16), jnp.swapaxes(w_kv[:, B_WIDTH:], 0, 1).astype(BF16),
            row(norm_mix_g[1]), jnp.swapaxes(b_w_in[0][:, :B_WIDTH], 0, 1).astype(BF16),
            b_w_in[0][:, B_WIDTH:].astype(BF16))
    x1, kb, vbt, qbt, qmem_b = _post_mixer(
        x, mix, qmem, mkbdt, mvbd, 0,
        wout[:A_WIDTH].reshape(A_HEADS, A_HEAD_DIM, D_MODEL).astype(BF16), wout[A_WIDTH:].astype(BF16),
        *ffn_weights(0), tail=tail, last=False)

    bias = _band_bias(b_rel_bias[0].astype(F32))
    mix_b = _band_attn(qbt, kb, vbt, bias)
    wout = b_w_out[0]
    return _post_mixer(
        x1, mix_b, qmem_b, mkbdt, mvbd, 1,
        wout[:B_WIDTH].astype(BF16), wout[B_WIDTH:].astype(BF16),
        *ffn_weights(1), tail=(row(final_g),), last=True)
```

```python
import functools
import math

import jax
import jax.numpy as jnp
from jax import lax
from jax.experimental import pallas as pl
from jax.experimental.pallas import tpu as pltpu

F32 = jnp.float32
BF16 = jnp.bfloat16

D_MODEL = 1024
CHUNK = 64
MEM_TOKENS = 256
MEM_HEADS = 4
MEM_WIDTH = 256
MEM_HEAD_DIM = 64
A_WIDTH = 768
A_HEADS = 4
A_HEAD_DIM = 192
A_CONV = 4
B_HEADS = 12
B_HEAD_DIM = 64
B_WIDTH = 768
BAND_CHUNKS = 9
MAX_REL = 128
REL_SIZE = MAX_REL + CHUNK
D_FF = 2816
FFN_CONV = 3
EPS = 1e-6

V_AUG = 256
GATE_LANES = 128
HALO = 8
NEG_BIG = -1e30

TM_IN = 256
L_MLSTM = 256
TM_POST = 512
FF_CHUNK = 256
TQ_HALF = 128
TQ_BAND = 2 * TQ_HALF
BAND_PREV = (BAND_CHUNKS - 1) * CHUNK
TK_HALF = BAND_PREV + TQ_HALF
TK_BAND = BAND_PREV + TQ_BAND
N_BIAS_VARIANTS = BAND_PREV // TQ_HALF + 1
BIAS_LANES = TK_HALF + TQ_HALF
SOFTMAX_ROWS = 64

VMEM_LIMIT = 56 * 1024 * 1024


def _resident(shape):
    nd = len(shape)
    return pl.BlockSpec(shape, lambda *_: (0,) * nd, pipeline_mode=pl.Buffered(1))


def _rms(x, g):
    return x * lax.rsqrt(jnp.mean(x * x, axis=-1, keepdims=True) + EPS) * g


def _dot(a, b):
    return jnp.dot(a, b, preferred_element_type=F32)


def _dot_nt(a, b):
    return lax.dot_general(a, b, (((1,), (1,)), ((), ())), preferred_element_type=F32)


def _dot_tn(a, b):
    return lax.dot_general(a, b, (((0,), (0,)), ((), ())), preferred_element_type=F32)


def _sigmoid(x):
    return 0.5 + 0.5 * jnp.tanh(0.5 * x)


def _silu(x):
    half = 0.5 * x
    return half + half * jnp.tanh(half)


def _mem_kv_kernel(mem_ref, wkt_ref, wv_ref, mkbdt_ref, mvbd_ref):
    m = mem_ref[0]
    mkt = _dot_nt(wkt_ref[0], m)
    mv = _dot(m, wv_ref[0])
    f_idx = lax.broadcasted_iota(jnp.int32, (MEM_WIDTH, MEM_HEADS * MEM_TOKENS), 0)
    c_idx = lax.broadcasted_iota(jnp.int32, (MEM_WIDTH, MEM_HEADS * MEM_TOKENS), 1)
    mkt4 = jnp.concatenate([mkt] * MEM_HEADS, axis=1)
    mkbdt_ref[0, 0] = jnp.where(f_idx // MEM_HEAD_DIM == c_idx // MEM_TOKENS, mkt4, 0.0).astype(BF16)
    r_idx = lax.broadcasted_iota(jnp.int32, (MEM_HEADS * MEM_TOKENS, MEM_WIDTH), 0)
    g_idx = lax.broadcasted_iota(jnp.int32, (MEM_HEADS * MEM_TOKENS, MEM_WIDTH), 1)
    mv4 = jnp.concatenate([mv] * MEM_HEADS, axis=0)
    mvbd_ref[0, 0] = jnp.where(r_idx // MEM_TOKENS == g_idx // MEM_HEAD_DIM, mv4, 0.0).astype(BF16)


def _mem_kv(mem_bf, wkt, wv):
    depth = wkt.shape[0]
    bsz = mem_bf.shape[0]
    return pl.pallas_call(
        _mem_kv_kernel,
        grid=(depth, bsz),
        in_specs=[
            pl.BlockSpec((1, MEM_TOKENS, D_MODEL), lambda l, b: (b, 0, 0)),
            pl.BlockSpec((1, MEM_WIDTH, D_MODEL), lambda l, b: (l, 0, 0)),
            pl.BlockSpec((1, D_MODEL, MEM_WIDTH), lambda l, b: (l, 0, 0)),
        ],
        out_specs=[
            pl.BlockSpec((1, 1, MEM_WIDTH, MEM_HEADS * MEM_TOKENS), lambda l, b: (l, b, 0, 0)),
            pl.BlockSpec((1, 1, MEM_HEADS * MEM_TOKENS, MEM_WIDTH), lambda l, b: (l, b, 0, 0)),
        ],
        out_shape=[
            jax.ShapeDtypeStruct((depth, bsz, MEM_WIDTH, MEM_HEADS * MEM_TOKENS), BF16),
            jax.ShapeDtypeStruct((depth, bsz, MEM_HEADS * MEM_TOKENS, MEM_WIDTH), BF16),
        ],
        compiler_params=pltpu.CompilerParams(dimension_semantics=("arbitrary", "arbitrary")),
        name="mem_kv",
    )(mem_bf, wkt, wv)


def _causal_taps(x, prev, taps):
    n_taps = taps.shape[0]
    halo_row = lax.broadcasted_iota(jnp.int32, (HALO, x.shape[1]), 0)
    y = taps[n_taps - 1:n_taps, :] * x
    for back in range(1, n_taps):
        head = jnp.where(halo_row < back, pltpu.roll(prev, back, axis=0), pltpu.roll(x[:HALO, :], back, axis=0))
        shifted = jnp.concatenate([head, pltpu.roll(x, back, axis=0)[HALO:, :]], axis=0)
        y = y + taps[n_taps - 1 - back:n_taps - back, :] * shifted
    return y


def _lane_scan(x, op, identity):
    lane = lax.broadcasted_iota(jnp.int32, x.shape, 1)
    step = 1
    while step < x.shape[1]:
        x = op(x, jnp.where(lane >= step, pltpu.roll(x, step, axis=1), identity))
        step *= 2
    return x


def _a_in_kernel(x_ref, g_ref, wqk_ref, wvt_ref, wot_ref, wgm_ref, gate_b_ref, vone_ref, cw_ref, cb_ref,
                 q_ref, k_ref, vt_ref, ot_ref, rows_ref, acol_ref, decay_ref, qmem_ref, halo_ref, m_ref):
    tm = x_ref.shape[1]

    @pl.when(pl.program_id(1) == 0)
    def _():
        halo_ref[...] = jnp.zeros(halo_ref.shape, F32)
        m_ref[...] = jnp.zeros(m_ref.shape, F32)

    hn = _rms(x_ref[0], g_ref[...]).astype(BF16)

    gm = _dot(hn, wgm_ref[...])
    qmem_ref[0] = gm[:, 2 * GATE_LANES:].astype(BF16)
    gi = (gm[:, :GATE_LANES] + gate_b_ref[:, :GATE_LANES]).T[0:8, :]
    fg = (gm[:, GATE_LANES:2 * GATE_LANES] + gate_b_ref[:, GATE_LANES:]).T[0:8, :]
    head_row = lax.broadcasted_iota(jnp.int32, (8, tm), 0) < A_HEADS
    logf = jnp.where(head_row, jnp.minimum(fg, 0.0) - jnp.log(1.0 + jnp.exp(-jnp.abs(fg))), 0.0)
    b = _lane_scan(logf, jnp.add, 0.0)
    a = gi - b
    m_prev = m_ref[:, 0:1]
    g = jnp.maximum(_lane_scan(a, jnp.maximum, -jnp.inf), m_prev)
    g_last = g[:, tm - 1:tm]
    m_ref[...] = jnp.broadcast_to(b[:, tm - 1:tm] + g_last, m_ref.shape)
    decay_ref[0, 0] = jnp.broadcast_to(jnp.exp(m_prev - g_last), decay_ref.shape[2:])
    rows_ref[0] = jnp.concatenate([g, jnp.exp(m_prev - g), jnp.exp(-(b + g)), jnp.exp(a - g_last)], axis=0)
    acol_ref[0] = jnp.concatenate([a, jnp.zeros((GATE_LANES - 8, tm), F32)], axis=0).T

    raw = _dot(hn, wqk_ref[...])
    vt_ref[0] = (_dot_nt(wvt_ref[...], hn) + vone_ref[...]).astype(BF16)
    ot_ref[0] = 0.5 * _dot_nt(wot_ref[...], hn)
    prev = halo_ref[...]
    halo_ref[...] = raw[tm - HALO:, :]
    qk = _silu(_causal_taps(raw, prev, cw_ref[...]) + cb_ref[...])
    for h in range(A_HEADS):
        lo = h * A_HEAD_DIM
        q_ref[0, h] = qk[:, lo:lo + A_HEAD_DIM].astype(BF16)
        k_ref[0, h] = (qk[:, A_WIDTH + lo:A_WIDTH + lo + A_HEAD_DIM] * (A_HEAD_DIM ** -0.5)).astype(BF16)


def _a_in(x, g, wqk, wvt, wot, wgm, gate_b, vone, cw, cb):
    bsz, seq, _ = x.shape
    tm = TM_IN
    head_spec = lambda width: pl.BlockSpec((1, A_HEADS, tm, width), lambda b, j: (b, 0, j, 0))
    row_spec = lambda width: pl.BlockSpec((1, tm, width), lambda b, j: (b, j, 0))
    col_spec = lambda width: pl.BlockSpec((1, width, tm), lambda b, j: (b, 0, j))
    weights = (g, wqk, wvt, wot, wgm, gate_b, vone, cw, cb)
    return pl.pallas_call(
        _a_in_kernel,
        grid=(bsz, seq // tm),
        in_specs=[row_spec(D_MODEL)] + [_resident(w.shape) for w in weights],
        out_specs=[
            head_spec(A_HEAD_DIM), head_spec(A_HEAD_DIM), col_spec(A_HEADS * V_AUG), col_spec(A_WIDTH),
            col_spec(4 * 8),
            row_spec(GATE_LANES),
            pl.BlockSpec((1, 1, 8, GATE_LANES), lambda b, j: (b, j, 0, 0)),
            row_spec(MEM_WIDTH),
        ],
        out_shape=[
            jax.ShapeDtypeStruct((bsz, A_HEADS, seq, A_HEAD_DIM), BF16),
            jax.ShapeDtypeStruct((bsz, A_HEADS, seq, A_HEAD_DIM), BF16),
            jax.ShapeDtypeStruct((bsz, A_HEADS * V_AUG, seq), BF16),
            jax.ShapeDtypeStruct((bsz, A_WIDTH, seq), F32),
            jax.ShapeDtypeStruct((bsz, 4 * 8, seq), F32),
            jax.ShapeDtypeStruct((bsz, seq, GATE_LANES), F32),
            jax.ShapeDtypeStruct((bsz, seq // tm, 8, GATE_LANES), F32),
            jax.ShapeDtypeStruct((bsz, seq, MEM_WIDTH), BF16),
        ],
        scratch_shapes=[pltpu.VMEM((HALO, 2 * A_WIDTH), F32), pltpu.VMEM((8, GATE_LANES), F32)],
        compiler_params=pltpu.CompilerParams(
            dimension_semantics=("arbitrary", "arbitrary"), vmem_limit_bytes=VMEM_LIMIT),
        name="a_in",
    )(x, *weights)


def _mlstm_kernel(q_ref, k_ref, vt_ref, ot_ref, rows_ref, acol_ref, decay_ref, hg_ref, out_ref, ct_ref):
    L = q_ref.shape[2]
    half = L // 2

    @pl.when(pl.program_id(1) == 0)
    def _():
        ct_ref[...] = jnp.zeros(ct_ref.shape, F32)

    rows = rows_ref[0]
    a_cols = acol_ref[0]
    decay = decay_ref[0, 0]
    upper = (lax.broadcasted_iota(jnp.int32, (half, half), 0)
             <= lax.broadcasted_iota(jnp.int32, (half, half), 1))
    zero_block = jnp.zeros((half, half), BF16)

    for h in range(A_HEADS):
        a_col = a_cols[:, h:h + 1]
        g_row, inter_row, en_row, wexp_row = (rows[8 * i + h:8 * i + h + 1, :] for i in range(4))
        qh = q_ref[0, h]
        kh = k_ref[0, h]
        vt = vt_ref[0, h * V_AUG:(h + 1) * V_AUG, :]
        st = _dot_nt(kh, qh)
        p00 = jnp.exp(jnp.where(upper, a_col[:half] - g_row[:, :half], -jnp.inf))
        p01 = jnp.exp(a_col[:half] - g_row[:, half:])
        p11 = jnp.exp(jnp.where(upper, a_col[half:] - g_row[:, half:], -jnp.inf))
        s_full = jnp.concatenate([
            jnp.concatenate([(st[:half, :half] * p00).astype(BF16), (st[:half, half:] * p01).astype(BF16)], axis=1),
            jnp.concatenate([zero_block, (st[half:, half:] * p11).astype(BF16)], axis=1)], axis=0)
        ct_prev = ct_ref[h]
        numt = _dot(vt, s_full) + _dot_nt(ct_prev.astype(BF16), qh) * inter_row
        den = numt[A_HEAD_DIM:A_HEAD_DIM + 1, :]
        r = 1.0 / jnp.maximum(jnp.abs(den), en_row)
        body = numt[:A_HEAD_DIM, :]
        ssq = jnp.sum(body * body, axis=0, keepdims=True)
        factor = r * lax.rsqrt(r * r * ssq * (1.0 / A_HEAD_DIM) + EPS)
        lo = h * A_HEAD_DIM
        gate = hg_ref[h] + hg_ref[h] * jnp.tanh(ot_ref[0, lo:lo + A_HEAD_DIM, :])
        out_ref[0, lo:lo + A_HEAD_DIM, :] = (body * factor * gate).astype(BF16)

        wvt = (vt.astype(F32) * wexp_row).astype(BF16)
        ct_ref[h] = decay[h:h + 1, 0:1] * ct_prev + _dot(wvt, kh)


def _mlstm(q, k, vt, ot_half, rows, a_cols, decay, head_g_half):
    bsz, nh, seq, dh = q.shape
    L = L_MLSTM
    head_spec = pl.BlockSpec((1, nh, L, dh), lambda b, j: (b, 0, j, 0))
    col_spec = lambda width: pl.BlockSpec((1, width, L), lambda b, j: (b, 0, j))
    return pl.pallas_call(
        _mlstm_kernel,
        grid=(bsz, seq // L),
        in_specs=[
            head_spec, head_spec, col_spec(nh * V_AUG), col_spec(nh * dh), col_spec(rows.shape[1]),
            pl.BlockSpec((1, L, GATE_LANES), lambda b, j: (b, j, 0)),
            pl.BlockSpec((1, 1, 8, GATE_LANES), lambda b, j: (b, j, 0, 0)),
            _resident(head_g_half.shape),
        ],
        out_specs=col_spec(nh * dh),
        out_shape=jax.ShapeDtypeStruct((bsz, nh * dh, seq), BF16),
        scratch_shapes=[pltpu.VMEM((nh, V_AUG, dh), F32)],
        compiler_params=pltpu.CompilerParams(
            dimension_semantics=("arbitrary", "arbitrary"), vmem_limit_bytes=VMEM_LIMIT),
        name="mlstm",
    )(q, k, vt, ot_half, rows, a_cols, decay, head_g_half)


def _post_mixer_kernel(*refs, last):
    if last:
        (x_ref, mix_ref, qmem_ref, mkbdt_ref, mvbd_ref, wout_mix_ref, wout_mem_ref, ffn_g_ref, wu_ref, wg_ref,
         cw_ref, cb_ref, wd_ref, final_g_ref, out_ref, halo_ref) = refs
    else:
        (x_ref, mix_ref, qmem_ref, mkbdt_ref, mvbd_ref, wout_mix_ref, wout_mem_ref, ffn_g_ref, wu_ref, wg_ref,
         cw_ref, cb_ref, wd_ref, kv_g_ref, wk_ref, wvt_ref, nxt_g_ref, wqt_ref, wqmem_ref,
         out_ref, k_ref, vt_ref, qnt_ref, qmemn_ref, halo_ref) = refs
    tm = x_ref.shape[1]

    @pl.when(pl.program_id(1) == 0)
    def _():
        halo_ref[...] = jnp.zeros((HALO, D_FF), F32)

    s = _dot(qmem_ref[0], mkbdt_ref[0, 0]) * (MEM_HEAD_DIM ** -0.5)
    probs = []
    for h in range(MEM_HEADS):
        sh = s[:, h * MEM_TOKENS:(h + 1) * MEM_TOKENS]
        e = jnp.exp(sh - jnp.max(sh, axis=1, keepdims=True))
        probs.append((e * (1.0 / jnp.sum(e, axis=1, keepdims=True))).astype(BF16))
    mem_out = _dot(jnp.concatenate(probs, axis=1), mvbd_ref[0, 0])

    y = _dot(mem_out.astype(BF16), wout_mem_ref[...])
    y = y + _dot_tn(mix_ref[0], wout_mix_ref[...])
    x1 = x_ref[0] + y

    hn = _rms(x1, ffn_g_ref[...]).astype(BF16)
    n_chunks = D_FF // FF_CHUNK
    chunk_cols = [slice(c * FF_CHUNK, (c + 1) * FF_CHUNK) for c in range(n_chunks)]

    def up(c):
        return _dot(hn, wu_ref[:, chunk_cols[c]]), _dot(hn, wg_ref[:, chunk_cols[c]])

    x2 = x1
    u, g = up(0)
    for c in range(n_chunks):
        cols = chunk_cols[c]
        nxt = up(c + 1) if c + 1 < n_chunks else None
        prev = halo_ref[:, cols]
        halo_ref[:, cols] = g[tm - HALO:, :]
        gc = _causal_taps(g, prev, cw_ref[:, cols]) + cb_ref[:, cols]
        x2 = x2 + _dot((_silu(gc) * u).astype(BF16), wd_ref[cols, :])
        if nxt is not None:
            u, g = nxt

    if last:
        out_ref[0] = _rms(x2, final_g_ref[...])
    else:
        out_ref[0] = x2
        hkv = _rms(x2, kv_g_ref[...]).astype(BF16)
        k_ref[0] = _dot(hkv, wk_ref[...]).astype(BF16)
        vt_ref[0] = _dot_nt(wvt_ref[...], hkv).astype(BF16)
        hq = _rms(x2, nxt_g_ref[...]).astype(BF16)
        qnt_ref[0] = (_dot_nt(wqt_ref[...], hq) * (B_HEAD_DIM ** -0.5)).astype(BF16)
        qmemn_ref[0] = _dot(hq, wqmem_ref[...]).astype(BF16)


def _post_mixer(x, mix, qmem, mkbdt, mvbd, layer, wout_mix, wout_mem, ffn_g, wu, wg, cw, cb, wd, tail, last):
    bsz, seq, _ = x.shape
    tm = TM_POST
    row_spec = lambda width: pl.BlockSpec((1, tm, width), lambda b, j: (b, j, 0))
    col_spec = lambda width: pl.BlockSpec((1, width, tm), lambda b, j: (b, 0, j))
    mix_spec = col_spec(mix.shape[1])
    weights = (wout_mix, wout_mem, ffn_g, wu, wg, cw, cb, wd) + tuple(tail)
    in_specs = [
        row_spec(D_MODEL), mix_spec, row_spec(MEM_WIDTH),
        pl.BlockSpec((1, 1) + mkbdt.shape[2:], lambda b, j: (layer, b, 0, 0)),
        pl.BlockSpec((1, 1) + mvbd.shape[2:], lambda b, j: (layer, b, 0, 0)),
    ] + [_resident(w.shape) for w in weights]
    if last:
        out_specs = row_spec(D_MODEL)
        out_shape = jax.ShapeDtypeStruct((bsz, seq, D_MODEL), F32)
    else:
        out_specs = [row_spec(D_MODEL), row_spec(B_WIDTH), col_spec(B_WIDTH), col_spec(B_WIDTH), row_spec(MEM_WIDTH)]
        out_shape = [
            jax.ShapeDtypeStruct((bsz, seq, D_MODEL), F32),
            jax.ShapeDtypeStruct((bsz, seq, B_WIDTH), BF16),
            jax.ShapeDtypeStruct((bsz, B_WIDTH, seq), BF16),
            jax.ShapeDtypeStruct((bsz, B_WIDTH, seq), BF16),
            jax.ShapeDtypeStruct((bsz, seq, MEM_WIDTH), BF16),
        ]
    return pl.pallas_call(
        functools.partial(_post_mixer_kernel, last=last),
        grid=(bsz, seq // tm),
        in_specs=in_specs,
        out_specs=out_specs,
        out_shape=out_shape,
        scratch_shapes=[pltpu.VMEM((HALO, D_FF), F32)],
        compiler_params=pltpu.CompilerParams(
            dimension_semantics=("arbitrary", "arbitrary"), vmem_limit_bytes=VMEM_LIMIT),
        name="post_mixer_last" if last else "post_mixer",
    )(x, mix, qmem, mkbdt, mvbd, *weights)


def _band_bias_kernel(tbl_ref, out_ref):
    v = pl.program_id(0)
    h = pl.program_id(1)
    b = lax.broadcasted_iota(jnp.int32, (8, BIAS_LANES), 0)
    c = lax.broadcasted_iota(jnp.int32, (8, BIAS_LANES), 1)
    r_signed = jnp.where(c < TQ_HALF, c, c - BIAS_LANES)
    idx = jnp.clip(v * TQ_HALF + r_signed - b, -(CHUNK - 1), MAX_REL) + (CHUNK - 1)

    def body(e, acc):
        return jnp.where(idx == e, tbl_ref[h, e], acc)

    base = lax.fori_loop(0, REL_SIZE, body, jnp.zeros((8, BIAS_LANES), F32), unroll=8)
    kb = lax.broadcasted_iota(jnp.int32, (8, TQ_HALF), 0)
    r = lax.broadcasted_iota(jnp.int32, (8, TQ_HALF), 1)
    for a in range(TK_HALF // 8):
        first = (-8 * a) % BIAS_LANES
        lo = first // TQ_HALF * TQ_HALF
        hi = (lo + TQ_HALF) % BIAS_LANES
        window = jnp.concatenate([base[:, lo:lo + TQ_HALF], base[:, hi:hi + TQ_HALF]], axis=1)
        if first != lo:
            window = pltpu.roll(window, 2 * TQ_HALF - (first - lo), axis=1)
        dchunk = (v * TQ_HALF + r) // CHUNK - (8 * a + kb) // CHUNK
        valid = (dchunk >= 0) & (dchunk < BAND_CHUNKS)
        out_ref[0, 0, 8 * a:8 * a + 8, :] = jnp.where(valid, window[:, :TQ_HALF], NEG_BIG)


def _band_bias(rel_table):
    return pl.pallas_call(
        _band_bias_kernel,
        grid=(N_BIAS_VARIANTS, B_HEADS),
        in_specs=[pl.BlockSpec(memory_space=pltpu.SMEM)],
        out_specs=pl.BlockSpec((1, 1, TK_HALF, TQ_HALF), lambda v, h: (v, h, 0, 0)),
        out_shape=jax.ShapeDtypeStruct((N_BIAS_VARIANTS, B_HEADS, TK_HALF, TQ_HALF), F32),
        compiler_params=pltpu.CompilerParams(dimension_semantics=("arbitrary", "arbitrary")),
        name="band_bias",
    )(rel_table)


def _band_attn_kernel(qt_ref, k_ref, vt_ref, bias_a_ref, bias_b_ref, out_ref):
    j = pl.program_id(1)
    first_half = 2 * j - (N_BIAS_VARIANTS - 1)
    start = pl.multiple_of(jnp.maximum(first_half, 0) * TQ_HALF, TQ_HALF)
    pair_width = 2 * B_HEAD_DIM
    pair_row = lax.broadcasted_iota(jnp.int32, (pair_width, TQ_BAND), 0)
    zeros_half = jnp.zeros((TQ_HALF, TQ_HALF), BF16)

    def scores(pair):
        lo = pair * pair_width
        k_pair = k_ref[0, pl.ds(start, TK_BAND), lo:lo + pair_width]
        q_pair = qt_ref[0, lo:lo + pair_width, :]
        zero = jnp.zeros_like(q_pair)
        q_sel = jnp.concatenate([jnp.where(pair_row < B_HEAD_DIM, q_pair, zero),
                                 jnp.where(pair_row >= B_HEAD_DIM, q_pair, zero)], axis=1)
        return _dot(k_pair, q_sel)

    def softmax_half(s, row0, c0, bias_ref, h):
        ts = []
        m_acc = None
        for r in range(0, TK_HALF, SOFTMAX_ROWS):
            t = s[row0 + r:row0 + r + SOFTMAX_ROWS, c0:c0 + TQ_HALF] + bias_ref[0, h, r:r + SOFTMAX_ROWS, :]
            ts.append(t)
            m_acc = t if m_acc is None else jnp.maximum(m_acc, t)
        m = jnp.max(m_acc, axis=0, keepdims=True)
        ps = []
        l_acc = None
        for t in ts:
            e = jnp.exp(t - m)
            ps.append(e.astype(BF16))
            l_acc = e if l_acc is None else l_acc + e
        return jnp.concatenate(ps, axis=0), jnp.sum(l_acc, axis=0, keepdims=True)

    def tile(off_b):
        s_next = scores(0)
        for pair in range(B_HEADS // 2):
            s = s_next
            if pair + 1 < B_HEADS // 2:
                s_next = scores(pair + 1)
            cols, sums = [], []
            for parity in range(2):
                h = 2 * pair + parity
                c0 = parity * TQ_BAND
                p_a, l_a = softmax_half(s, 0, c0, bias_a_ref, h)
                p_b, l_b = softmax_half(s, off_b, c0 + TQ_HALF, bias_b_ref, h)
                cols.append(jnp.concatenate([p_a, zeros_half], axis=0))
                cols.append(jnp.concatenate([zeros_half, p_b] if off_b else [p_b, zeros_half], axis=0))
                sums += [l_a, l_b]
            lo = pair * pair_width
            vt = vt_ref[0, lo:lo + pair_width, pl.ds(start, TK_BAND)]
            o = _dot(vt, jnp.concatenate(cols, axis=1))
            inv = 1.0 / jnp.concatenate(sums, axis=1)
            out_ref[0, lo:lo + B_HEAD_DIM, :] = (o[:B_HEAD_DIM, :TQ_BAND] * inv[:, :TQ_BAND]).astype(BF16)
            out_ref[0, lo + B_HEAD_DIM:lo + pair_width, :] = (
                o[B_HEAD_DIM:, TQ_BAND:] * inv[:, TQ_BAND:]).astype(BF16)

    pl.when(first_half >= 0)(functools.partial(tile, TQ_HALF))
    pl.when(first_half < 0)(functools.partial(tile, 0))


def _band_attn(qt, k, vt, bias):
    bsz, seq, _ = k.shape
    nv = N_BIAS_VARIANTS
    bias_spec = lambda half: pl.BlockSpec(
        (1, B_HEADS, TK_HALF, TQ_HALF), lambda b, j: (jnp.minimum(2 * j + half, nv - 1), 0, 0, 0))
    return pl.pallas_call(
        _band_attn_kernel,
        grid=(bsz, seq // TQ_BAND),
        in_specs=[
            pl.BlockSpec((1, B_WIDTH, TQ_BAND), lambda b, j: (b, 0, j)),
            pl.BlockSpec((1, seq, B_WIDTH), lambda b, j: (b, 0, 0)),
            pl.BlockSpec((1, B_WIDTH, seq), lambda b, j: (b, 0, 0)),
            bias_spec(0), bias_spec(1),
        ],
        out_specs=pl.BlockSpec((1, B_WIDTH, TQ_BAND), lambda b, j: (b, 0, j)),
        out_shape=jax.ShapeDtypeStruct((bsz, B_WIDTH, seq), BF16),
        compiler_params=pltpu.CompilerParams(
            dimension_semantics=("arbitrary", "arbitrary"), vmem_limit_bytes=VMEM_LIMIT),
        name="band_attn",
    )(qt, k, vt, bias, bias)


def kernel(x, mem, norm_mix_g, norm_ffn_g, a_w_in, a_gate_b, a_conv_w, a_conv_b, a_head_g, a_w_out, kv_norm_g, w_kv,
           b_w_in, b_rel_bias, b_w_out, mem_w_kv, ffn_w_up, ffn_conv_w, ffn_conv_b, ffn_w_down, final_g):
    bsz, seq, d = x.shape
    assert d == D_MODEL and seq % max(TM_IN, L_MLSTM, TM_POST, TQ_BAND) == 0
    assert a_w_in.shape[0] == 1 and b_w_in.shape[0] == 1, "one mLSTM layer followed by one band-attention layer"
    assert TM_IN == L_MLSTM, "the input projection emits per-chunk decay terms, one tile per mLSTM chunk"
    row = lambda g: g.reshape(1, -1).astype(F32)

    wkt = jnp.swapaxes(mem_w_kv[:, :, :MEM_WIDTH], 1, 2).astype(BF16)
    wmv = mem_w_kv[:, :, MEM_WIDTH:].astype(BF16)
    mkbdt, mvbd = _mem_kv(mem.astype(BF16), wkt, wmv)

    w = a_w_in[0]
    wqk = w[:, 0:2 * A_WIDTH].astype(BF16)
    wvt = jnp.swapaxes(w[:, 2 * A_WIDTH:3 * A_WIDTH], 0, 1).reshape(A_HEADS, A_HEAD_DIM, D_MODEL)
    wvt = jnp.pad(wvt, ((0, 0), (0, V_AUG - A_HEAD_DIM), (0, 0))).reshape(A_HEADS * V_AUG, D_MODEL).astype(BF16)
    wot = jnp.swapaxes(w[:, 3 * A_WIDTH:4 * A_WIDTH], 0, 1).astype(BF16)
    gate_pad = ((0, 0), (0, GATE_LANES - A_HEADS))
    g0 = 4 * A_WIDTH
    wgm = jnp.concatenate([jnp.pad(w[:, g0:g0 + A_HEADS], gate_pad), jnp.pad(w[:, g0 + A_HEADS:g0 + 2 * A_HEADS], gate_pad),
                           w[:, g0 + 2 * A_HEADS:]], axis=1).astype(BF16)
    gate_b = jnp.pad(a_gate_b[0].astype(F32).reshape(2, A_HEADS), gate_pad).reshape(1, 2 * GATE_LANES)
    vone = (jnp.arange(A_HEADS * V_AUG) % V_AUG == A_HEAD_DIM).astype(F32).reshape(A_HEADS * V_AUG, 1)
    q, k, vt, ot_half, gate_rows, a_cols, decay, qmem = _a_in(
        x, row(norm_mix_g[0]), wqk, wvt, wot, wgm, gate_b, vone, a_conv_w[0].astype(F32), row(a_conv_b[0]))

    head_g_half = 0.5 * a_head_g[0].reshape(A_HEADS, A_HEAD_DIM, 1).astype(F32)
    mix = _mlstm(q, k, vt, ot_half, gate_rows, a_cols, decay, head_g_half)

    def ffn_weights(l):
        return (row(norm_ffn_g[l]), ffn_w_up[l][:, :D_FF].astype(BF16), ffn_w_up[l][:, D_FF:].astype(BF16),
                ffn_conv_w[l].astype(F32), row(ffn_conv_b[l]), ffn_w_down[l].astype(BF16))

    wout = a_w_out[0]
    tail = (row(kv_norm_g), w_kv[:, :B_WIDTH].astype(BF16), jnp.swapaxes(w_kv[:, B_WIDTH:], 0, 1).astype(BF16),
            row(norm_mix_g[1]), jnp.swapaxes(b_w_in[0][:, :B_WIDTH], 0, 1).astype(BF16),
            b_w_in[0][:, B_WIDTH:].astype(BF16))
    x1, kb, vbt, qbt, qmem_b = _post_mixer(
        x, mix, qmem, mkbdt, mvbd, 0,
        wout[:A_WIDTH].astype(BF16), wout[A_WIDTH:].astype(BF16),
        *ffn_weights(0), tail=tail, last=False)

    bias = _band_bias(b_rel_bias[0].astype(F32))
    mix_b = _band_attn(qbt, kb, vbt, bias)
    wout = b_w_out[0]
    return _post_mixer(
        x1, mix_b, qmem_b, mkbdt, mvbd, 1,
        wout[:B_WIDTH].astype(BF16), wout[B_WIDTH:].astype(BF16),
        *ffn_weights(1), tail=(row(final_g),), last=True)
```

```python
import functools
import math

import jax
import jax.numpy as jnp
from jax import lax
from jax.experimental import pallas as pl
from jax.experimental.pallas import tpu as pltpu

F32 = jnp.float32
BF16 = jnp.bfloat16

D_MODEL = 1024
CHUNK = 64
MEM_TOKENS = 256
MEM_HEADS = 4
MEM_WIDTH = 256
MEM_HEAD_DIM = 64
A_WIDTH = 768
A_HEADS = 4
A_HEAD_DIM = 192
A_CONV = 4
B_HEADS = 12
B_HEAD_DIM = 64
B_WIDTH = 768
BAND_CHUNKS = 9
MAX_REL = 128
REL_SIZE = MAX_REL + CHUNK
D_FF = 2816
FFN_CONV = 3
EPS = 1e-6

V_AUG = 256
GATE_LANES = 128
HALO = 8
NEG_BIG = -1e30

TM_IN = 256
L_MLSTM = 256
TM_POST = 512
FF_CHUNK = 256
TQ_HALF = 128
TQ_BAND = 2 * TQ_HALF
BAND_PREV = (BAND_CHUNKS - 1) * CHUNK
TK_HALF = BAND_PREV + TQ_HALF
TK_BAND = BAND_PREV + TQ_BAND
HALVES_BACK = BAND_PREV // TQ_HALF
BIAS_ROWS = TK_HALF + BAND_PREV
BIAS_LANES = BIAS_ROWS + TQ_HALF
SOFTMAX_ROWS = 64

VMEM_LIMIT = 56 * 1024 * 1024


def _resident(shape):
    nd = len(shape)
    return pl.BlockSpec(shape, lambda *_: (0,) * nd, pipeline_mode=pl.Buffered(1))


def _rms(x, g):
    return x * lax.rsqrt(jnp.mean(x * x, axis=-1, keepdims=True) + EPS) * g


def _dot(a, b):
    return jnp.dot(a, b, preferred_element_type=F32)


def _dot_nt(a, b):
    return lax.dot_general(a, b, (((1,), (1,)), ((), ())), preferred_element_type=F32)


def _dot_tn(a, b):
    return lax.dot_general(a, b, (((0,), (0,)), ((), ())), preferred_element_type=F32)


def _sigmoid(x):
    return 0.5 + 0.5 * jnp.tanh(0.5 * x)


def _silu(x):
    half = 0.5 * x
    return half + half * jnp.tanh(half)


def _mem_kv_kernel(mem_ref, wkt_ref, wv_ref, mkbdt_ref, mvbd_ref):
    m = mem_ref[0]
    mkt = _dot_nt(wkt_ref[0], m)
    mv = _dot(m, wv_ref[0])
    f_idx = lax.broadcasted_iota(jnp.int32, (MEM_WIDTH, MEM_HEADS * MEM_TOKENS), 0)
    c_idx = lax.broadcasted_iota(jnp.int32, (MEM_WIDTH, MEM_HEADS * MEM_TOKENS), 1)
    mkt4 = jnp.concatenate([mkt] * MEM_HEADS, axis=1)
    mkbdt_ref[0, 0] = jnp.where(f_idx // MEM_HEAD_DIM == c_idx // MEM_TOKENS, mkt4, 0.0).astype(BF16)
    r_idx = lax.broadcasted_iota(jnp.int32, (MEM_HEADS * MEM_TOKENS, MEM_WIDTH), 0)
    g_idx = lax.broadcasted_iota(jnp.int32, (MEM_HEADS * MEM_TOKENS, MEM_WIDTH), 1)
    mv4 = jnp.concatenate([mv] * MEM_HEADS, axis=0)
    mvbd_ref[0, 0] = jnp.where(r_idx // MEM_TOKENS == g_idx // MEM_HEAD_DIM, mv4, 0.0).astype(BF16)


def _mem_kv(mem_bf, wkt, wv):
    depth = wkt.shape[0]
    bsz = mem_bf.shape[0]
    return pl.pallas_call(
        _mem_kv_kernel,
        grid=(depth, bsz),
        in_specs=[
            pl.BlockSpec((1, MEM_TOKENS, D_MODEL), lambda l, b: (b, 0, 0)),
            pl.BlockSpec((1, MEM_WIDTH, D_MODEL), lambda l, b: (l, 0, 0)),
            pl.BlockSpec((1, D_MODEL, MEM_WIDTH), lambda l, b: (l, 0, 0)),
        ],
        out_specs=[
            pl.BlockSpec((1, 1, MEM_WIDTH, MEM_HEADS * MEM_TOKENS), lambda l, b: (l, b, 0, 0)),
            pl.BlockSpec((1, 1, MEM_HEADS * MEM_TOKENS, MEM_WIDTH), lambda l, b: (l, b, 0, 0)),
        ],
        out_shape=[
            jax.ShapeDtypeStruct((depth, bsz, MEM_WIDTH, MEM_HEADS * MEM_TOKENS), BF16),
            jax.ShapeDtypeStruct((depth, bsz, MEM_HEADS * MEM_TOKENS, MEM_WIDTH), BF16),
        ],
        compiler_params=pltpu.CompilerParams(dimension_semantics=("arbitrary", "arbitrary")),
        name="mem_kv",
    )(mem_bf, wkt, wv)


def _causal_taps(x, prev, taps):
    n_taps = taps.shape[0]
    halo_row = lax.broadcasted_iota(jnp.int32, (HALO, x.shape[1]), 0)
    y = taps[n_taps - 1:n_taps, :] * x
    for back in range(1, n_taps):
        head = jnp.where(halo_row < back, pltpu.roll(prev, back, axis=0), pltpu.roll(x[:HALO, :], back, axis=0))
        shifted = jnp.concatenate([head, pltpu.roll(x, back, axis=0)[HALO:, :]], axis=0)
        y = y + taps[n_taps - 1 - back:n_taps - back, :] * shifted
    return y


def _lane_scan(x, op, identity):
    lane = lax.broadcasted_iota(jnp.int32, x.shape, 1)
    step = 1
    while step < x.shape[1]:
        x = op(x, jnp.where(lane >= step, pltpu.roll(x, step, axis=1), identity))
        step *= 2
    return x


def _a_in_kernel(x_ref, g_ref, wqk_ref, wvt_ref, wot_ref, wgm_ref, gate_b_ref, vone_ref, cw_ref, cb_ref,
                 q_ref, k_ref, vt_ref, ot_ref, rows_ref, acol_ref, decay_ref, qmem_ref, halo_ref, m_ref):
    tm = x_ref.shape[1]

    @pl.when(pl.program_id(1) == 0)
    def _():
        halo_ref[...] = jnp.zeros(halo_ref.shape, F32)
        m_ref[...] = jnp.zeros(m_ref.shape, F32)

    hn = _rms(x_ref[0], g_ref[...]).astype(BF16)

    gm = _dot(hn, wgm_ref[...])
    qmem_ref[0] = gm[:, 2 * GATE_LANES:].astype(BF16)
    gi = (gm[:, :GATE_LANES] + gate_b_ref[:, :GATE_LANES]).T[0:8, :]
    fg = (gm[:, GATE_LANES:2 * GATE_LANES] + gate_b_ref[:, GATE_LANES:]).T[0:8, :]
    head_row = lax.broadcasted_iota(jnp.int32, (8, tm), 0) < A_HEADS
    logf = jnp.where(head_row, jnp.minimum(fg, 0.0) - jnp.log(1.0 + jnp.exp(-jnp.abs(fg))), 0.0)
    b = _lane_scan(logf, jnp.add, 0.0)
    a = gi - b
    m_prev = m_ref[:, 0:1]
    g = jnp.maximum(_lane_scan(a, jnp.maximum, -jnp.inf), m_prev)
    g_last = g[:, tm - 1:tm]
    m_ref[...] = jnp.broadcast_to(b[:, tm - 1:tm] + g_last, m_ref.shape)
    decay_ref[0, 0] = jnp.broadcast_to(jnp.exp(m_prev - g_last), decay_ref.shape[2:])
    rows_ref[0] = jnp.concatenate([g, jnp.exp(m_prev - g), jnp.exp(-(b + g)), jnp.exp(a - g_last)], axis=0)
    acol_ref[0] = jnp.concatenate([a, jnp.zeros((GATE_LANES - 8, tm), F32)], axis=0).T

    raw = _dot(hn, wqk_ref[...])
    vt_ref[0] = (_dot_nt(wvt_ref[...], hn) + vone_ref[...]).astype(BF16)
    ot_ref[0] = 0.5 * _dot_nt(wot_ref[...], hn)
    prev = halo_ref[...]
    halo_ref[...] = raw[tm - HALO:, :]
    qk = _silu(_causal_taps(raw, prev, cw_ref[...]) + cb_ref[...])
    for h in range(A_HEADS):
        lo = h * A_HEAD_DIM
        q_ref[0, h] = qk[:, lo:lo + A_HEAD_DIM].astype(BF16)
        k_ref[0, h] = (qk[:, A_WIDTH + lo:A_WIDTH + lo + A_HEAD_DIM] * (A_HEAD_DIM ** -0.5)).astype(BF16)


def _a_in(x, g, wqk, wvt, wot, wgm, gate_b, vone, cw, cb):
    bsz, seq, _ = x.shape
    tm = TM_IN
    head_spec = lambda width: pl.BlockSpec((1, A_HEADS, tm, width), lambda b, j: (b, 0, j, 0))
    row_spec = lambda width: pl.BlockSpec((1, tm, width), lambda b, j: (b, j, 0))
    col_spec = lambda width: pl.BlockSpec((1, width, tm), lambda b, j: (b, 0, j))
    weights = (g, wqk, wvt, wot, wgm, gate_b, vone, cw, cb)
    return pl.pallas_call(
        _a_in_kernel,
        grid=(bsz, seq // tm),
        in_specs=[row_spec(D_MODEL)] + [_resident(w.shape) for w in weights],
        out_specs=[
            head_spec(A_HEAD_DIM), head_spec(A_HEAD_DIM), col_spec(A_HEADS * V_AUG), col_spec(A_WIDTH),
            col_spec(4 * 8),
            row_spec(GATE_LANES),
            pl.BlockSpec((1, 1, 8, GATE_LANES), lambda b, j: (b, j, 0, 0)),
            row_spec(MEM_WIDTH),
        ],
        out_shape=[
            jax.ShapeDtypeStruct((bsz, A_HEADS, seq, A_HEAD_DIM), BF16),
            jax.ShapeDtypeStruct((bsz, A_HEADS, seq, A_HEAD_DIM), BF16),
            jax.ShapeDtypeStruct((bsz, A_HEADS * V_AUG, seq), BF16),
            jax.ShapeDtypeStruct((bsz, A_WIDTH, seq), F32),
            jax.ShapeDtypeStruct((bsz, 4 * 8, seq), F32),
            jax.ShapeDtypeStruct((bsz, seq, GATE_LANES), F32),
            jax.ShapeDtypeStruct((bsz, seq // tm, 8, GATE_LANES), F32),
            jax.ShapeDtypeStruct((bsz, seq, MEM_WIDTH), BF16),
        ],
        scratch_shapes=[pltpu.VMEM((HALO, 2 * A_WIDTH), F32), pltpu.VMEM((8, GATE_LANES), F32)],
        compiler_params=pltpu.CompilerParams(
            dimension_semantics=("arbitrary", "arbitrary"), vmem_limit_bytes=VMEM_LIMIT),
        name="a_in",
    )(x, *weights)


def _mlstm_kernel(q_ref, k_ref, vt_ref, ot_ref, rows_ref, acol_ref, decay_ref, hg_ref, out_ref, ct_ref):
    L = q_ref.shape[2]
    half = L // 2

    @pl.when(pl.program_id(1) == 0)
    def _():
        ct_ref[...] = jnp.zeros(ct_ref.shape, F32)

    rows = rows_ref[0]
    a_cols = acol_ref[0]
    decay = decay_ref[0, 0]
    upper = (lax.broadcasted_iota(jnp.int32, (half, half), 0)
             <= lax.broadcasted_iota(jnp.int32, (half, half), 1))
    zero_block = jnp.zeros((half, half), BF16)

    for h in range(A_HEADS):
        a_col = a_cols[:, h:h + 1]
        g_row, inter_row, en_row, wexp_row = (rows[8 * i + h:8 * i + h + 1, :] for i in range(4))
        qh = q_ref[0, h]
        kh = k_ref[0, h]
        vt = vt_ref[0, h * V_AUG:(h + 1) * V_AUG, :]
        st = _dot_nt(kh, qh)
        p00 = jnp.exp(jnp.where(upper, a_col[:half] - g_row[:, :half], -jnp.inf))
        p01 = jnp.exp(a_col[:half] - g_row[:, half:])
        p11 = jnp.exp(jnp.where(upper, a_col[half:] - g_row[:, half:], -jnp.inf))
        s_full = jnp.concatenate([
            jnp.concatenate([(st[:half, :half] * p00).astype(BF16), (st[:half, half:] * p01).astype(BF16)], axis=1),
            jnp.concatenate([zero_block, (st[half:, half:] * p11).astype(BF16)], axis=1)], axis=0)
        ct_prev = ct_ref[h]
        numt = _dot(vt, s_full) + _dot_nt(ct_prev.astype(BF16), qh) * inter_row
        den = numt[A_HEAD_DIM:A_HEAD_DIM + 1, :]
        r = 1.0 / jnp.maximum(jnp.abs(den), en_row)
        body = numt[:A_HEAD_DIM, :]
        ssq = jnp.sum(body * body, axis=0, keepdims=True)
        factor = r * lax.rsqrt(r * r * ssq * (1.0 / A_HEAD_DIM) + EPS)
        lo = h * A_HEAD_DIM
        gate = hg_ref[h] + hg_ref[h] * jnp.tanh(ot_ref[0, lo:lo + A_HEAD_DIM, :])
        out_ref[0, lo:lo + A_HEAD_DIM, :] = (body * factor * gate).astype(BF16)

        wvt = (vt.astype(F32) * wexp_row).astype(BF16)
        ct_ref[h] = decay[h:h + 1, 0:1] * ct_prev + _dot(wvt, kh)


def _mlstm(q, k, vt, ot_half, rows, a_cols, decay, head_g_half):
    bsz, nh, seq, dh = q.shape
    L = L_MLSTM
    head_spec = pl.BlockSpec((1, nh, L, dh), lambda b, j: (b, 0, j, 0))
    col_spec = lambda width: pl.BlockSpec((1, width, L), lambda b, j: (b, 0, j))
    return pl.pallas_call(
        _mlstm_kernel,
        grid=(bsz, seq // L),
        in_specs=[
            head_spec, head_spec, col_spec(nh * V_AUG), col_spec(nh * dh), col_spec(rows.shape[1]),
            pl.BlockSpec((1, L, GATE_LANES), lambda b, j: (b, j, 0)),
            pl.BlockSpec((1, 1, 8, GATE_LANES), lambda b, j: (b, j, 0, 0)),
            _resident(head_g_half.shape),
        ],
        out_specs=col_spec(nh * dh),
        out_shape=jax.ShapeDtypeStruct((bsz, nh * dh, seq), BF16),
        scratch_shapes=[pltpu.VMEM((nh, V_AUG, dh), F32)],
        compiler_params=pltpu.CompilerParams(
            dimension_semantics=("arbitrary", "arbitrary"), vmem_limit_bytes=VMEM_LIMIT),
        name="mlstm",
    )(q, k, vt, ot_half, rows, a_cols, decay, head_g_half)


def _post_mixer_kernel(*refs, last):
    if last:
        (x_ref, mix_ref, qmem_ref, mkbdt_ref, mvbd_ref, wout_mix_ref, wout_mem_ref, ffn_g_ref, wu_ref, wg_ref,
         cw_ref, cb_ref, wd_ref, final_g_ref, out_ref, halo_ref, act_ref) = refs
    else:
        (x_ref, mix_ref, qmem_ref, mkbdt_ref, mvbd_ref, wout_mix_ref, wout_mem_ref, ffn_g_ref, wu_ref, wg_ref,
         cw_ref, cb_ref, wd_ref, kv_g_ref, wk_ref, wvt_ref, nxt_g_ref, wqt_ref, wqmem_ref,
         out_ref, k_ref, vt_ref, qnt_ref, qmemn_ref, halo_ref, act_ref) = refs
    tm = x_ref.shape[1]

    @pl.when(pl.program_id(1) == 0)
    def _():
        halo_ref[...] = jnp.zeros((HALO, D_FF), F32)

    s = _dot(qmem_ref[0], mkbdt_ref[0, 0]) * (MEM_HEAD_DIM ** -0.5)
    probs = []
    for h in range(MEM_HEADS):
        sh = s[:, h * MEM_TOKENS:(h + 1) * MEM_TOKENS]
        e = jnp.exp(sh - jnp.max(sh, axis=1, keepdims=True))
        probs.append((e * (1.0 / jnp.sum(e, axis=1, keepdims=True))).astype(BF16))
    mem_out = _dot(jnp.concatenate(probs, axis=1), mvbd_ref[0, 0])

    y = _dot(mem_out.astype(BF16), wout_mem_ref[...])
    y = y + _dot_tn(mix_ref[0], wout_mix_ref[...])
    x1 = x_ref[0] + y

    hn = _rms(x1, ffn_g_ref[...]).astype(BF16)
    n_chunks = D_FF // FF_CHUNK
    chunk_cols = [slice(c * FF_CHUNK, (c + 1) * FF_CHUNK) for c in range(n_chunks)]

    def up(c):
        return _dot(hn, wu_ref[:, chunk_cols[c]]), _dot(hn, wg_ref[:, chunk_cols[c]])

    u, g = up(0)
    for c in range(n_chunks):
        cols = chunk_cols[c]
        nxt = up(c + 1) if c + 1 < n_chunks else None
        prev = halo_ref[:, cols]
        halo_ref[:, cols] = g[tm - HALO:, :]
        gc = _causal_taps(g, prev, cw_ref[:, cols]) + cb_ref[:, cols]
        act_ref[:, cols] = (_silu(gc) * u).astype(BF16)
        if nxt is not None:
            u, g = nxt
    x2 = x1 + _dot(act_ref[...], wd_ref[...])

    if last:
        out_ref[0] = _rms(x2, final_g_ref[...])
    else:
        out_ref[0] = x2
        hkv = _rms(x2, kv_g_ref[...]).astype(BF16)
        k_ref[0] = _dot(hkv, wk_ref[...]).astype(BF16)
        vt_ref[0] = _dot_nt(wvt_ref[...], hkv).astype(BF16)
        hq = _rms(x2, nxt_g_ref[...]).astype(BF16)
        qnt_ref[0] = (_dot_nt(wqt_ref[...], hq) * (B_HEAD_DIM ** -0.5)).astype(BF16)
        qmemn_ref[0] = _dot(hq, wqmem_ref[...]).astype(BF16)


def _post_mixer(x, mix, qmem, mkbdt, mvbd, layer, wout_mix, wout_mem, ffn_g, wu, wg, cw, cb, wd, tail, last):
    bsz, seq, _ = x.shape
    tm = TM_POST
    row_spec = lambda width: pl.BlockSpec((1, tm, width), lambda b, j: (b, j, 0))
    col_spec = lambda width: pl.BlockSpec((1, width, tm), lambda b, j: (b, 0, j))
    mix_spec = col_spec(mix.shape[1])
    weights = (wout_mix, wout_mem, ffn_g, wu, wg, cw, cb, wd) + tuple(tail)
    in_specs = [
        row_spec(D_MODEL), mix_spec, row_spec(MEM_WIDTH),
        pl.BlockSpec((1, 1) + mkbdt.shape[2:], lambda b, j: (layer, b, 0, 0)),
        pl.BlockSpec((1, 1) + mvbd.shape[2:], lambda b, j: (layer, b, 0, 0)),
    ] + [_resident(w.shape) for w in weights]
    if last:
        out_specs = row_spec(D_MODEL)
        out_shape = jax.ShapeDtypeStruct((bsz, seq, D_MODEL), F32)
    else:
        out_specs = [row_spec(D_MODEL), row_spec(B_WIDTH), col_spec(B_WIDTH), col_spec(B_WIDTH), row_spec(MEM_WIDTH)]
        out_shape = [
            jax.ShapeDtypeStruct((bsz, seq, D_MODEL), F32),
            jax.ShapeDtypeStruct((bsz, seq, B_WIDTH), BF16),
            jax.ShapeDtypeStruct((bsz, B_WIDTH, seq), BF16),
            jax.ShapeDtypeStruct((bsz, B_WIDTH, seq), BF16),
            jax.ShapeDtypeStruct((bsz, seq, MEM_WIDTH), BF16),
        ]
    return pl.pallas_call(
        functools.partial(_post_mixer_kernel, last=last),
        grid=(bsz, seq // tm),
        in_specs=in_specs,
        out_specs=out_specs,
        out_shape=out_shape,
        scratch_shapes=[pltpu.VMEM((HALO, D_FF), F32), pltpu.VMEM((tm, D_FF), BF16)],
        compiler_params=pltpu.CompilerParams(
            dimension_semantics=("arbitrary", "arbitrary"), vmem_limit_bytes=VMEM_LIMIT),
        name="post_mixer_last" if last else "post_mixer",
    )(x, mix, qmem, mkbdt, mvbd, *weights)


def _band_bias_kernel(tbl_ref, out_ref):
    h = pl.program_id(0)
    b = lax.broadcasted_iota(jnp.int32, (8, BIAS_LANES), 0)
    c = lax.broadcasted_iota(jnp.int32, (8, BIAS_LANES), 1)
    r_signed = jnp.where(c < TQ_HALF, c, c - BIAS_LANES)
    idx = jnp.clip(BAND_PREV + r_signed - b, -(CHUNK - 1), MAX_REL) + (CHUNK - 1)

    def body(e, acc):
        return jnp.where(idx == e, tbl_ref[h, e], acc)

    base = lax.fori_loop(0, REL_SIZE, body, jnp.zeros((8, BIAS_LANES), F32), unroll=8)
    kb = lax.broadcasted_iota(jnp.int32, (8, TQ_HALF), 0)
    r = lax.broadcasted_iota(jnp.int32, (8, TQ_HALF), 1)
    for a in range(BIAS_ROWS // 8):
        first = (-8 * a) % BIAS_LANES
        lo = first // TQ_HALF * TQ_HALF
        hi = (lo + TQ_HALF) % BIAS_LANES
        window = jnp.concatenate([base[:, lo:lo + TQ_HALF], base[:, hi:hi + TQ_HALF]], axis=1)
        if first != lo:
            window = pltpu.roll(window, 2 * TQ_HALF - (first - lo), axis=1)
        dchunk = (BAND_PREV + r) // CHUNK - (8 * a + kb) // CHUNK
        valid = (dchunk >= 0) & (dchunk < BAND_CHUNKS)
        out_ref[0, 8 * a:8 * a + 8, :] = jnp.where(valid, window[:, :TQ_HALF], NEG_BIG)


def _band_bias(rel_table):
    return pl.pallas_call(
        _band_bias_kernel,
        grid=(B_HEADS,),
        in_specs=[pl.BlockSpec(memory_space=pltpu.SMEM)],
        out_specs=pl.BlockSpec((1, BIAS_ROWS, TQ_HALF), lambda h: (h, 0, 0)),
        out_shape=jax.ShapeDtypeStruct((B_HEADS, BIAS_ROWS, TQ_HALF), F32),
        compiler_params=pltpu.CompilerParams(dimension_semantics=("arbitrary",)),
        name="band_bias",
    )(rel_table)


def _band_attn_kernel(qt_ref, k_ref, vt_ref, bias_ref, out_ref):
    j = pl.program_id(1)
    first_half = 2 * j - HALVES_BACK
    start = pl.multiple_of(jnp.maximum(first_half, 0) * TQ_HALF, TQ_HALF)
    pair_width = 2 * B_HEAD_DIM
    pair_row = lax.broadcasted_iota(jnp.int32, (pair_width, TQ_BAND), 0)
    zeros_half = jnp.zeros((TQ_HALF, TQ_HALF), BF16)
    masked_half = jnp.full((TQ_HALF, TQ_HALF), NEG_BIG, F32)

    def scores(pair):
        lo = pair * pair_width
        k_pair = k_ref[0, pl.ds(start, TK_BAND), lo:lo + pair_width]
        q_pair = qt_ref[0, lo:lo + pair_width, :]
        zero = jnp.zeros_like(q_pair)
        q_sel = jnp.concatenate([jnp.where(pair_row < B_HEAD_DIM, q_pair, zero),
                                 jnp.where(pair_row >= B_HEAD_DIM, q_pair, zero)], axis=1)
        return _dot(k_pair, q_sel)

    def softmax_half(t, row0, c0):
        blocks = [t[row0 + r:row0 + r + SOFTMAX_ROWS, c0:c0 + TQ_HALF] for r in range(0, TK_HALF, SOFTMAX_ROWS)]
        m_acc = blocks[0]
        for blk in blocks[1:]:
            m_acc = jnp.maximum(m_acc, blk)
        m = jnp.max(m_acc, axis=0, keepdims=True)
        ps = []
        l_acc = None
        for blk in blocks:
            e = jnp.exp(blk - m)
            ps.append(e.astype(BF16))
            l_acc = e if l_acc is None else l_acc + e
        return jnp.concatenate(ps, axis=0), jnp.sum(l_acc, axis=0, keepdims=True)

    def tile(off_b, bias_row_a, bias_row_b):
        def bias_tile(pair):
            cols = []
            for h in (2 * pair, 2 * pair + 1):
                bias_a = bias_ref[h, pl.ds(bias_row_a, TK_HALF), :]
                bias_b = bias_ref[h, pl.ds(bias_row_b, TK_HALF), :]
                cols.append(jnp.concatenate([bias_a, masked_half], axis=0))
                cols.append(jnp.concatenate([masked_half, bias_b] if off_b else [bias_b, masked_half], axis=0))
            return jnp.concatenate(cols, axis=1)

        t_next = scores(0) + bias_tile(0)
        for pair in range(B_HEADS // 2):
            t = t_next
            if pair + 1 < B_HEADS // 2:
                t_next = scores(pair + 1) + bias_tile(pair + 1)
            cols, sums = [], []
            for parity in range(2):
                c0 = parity * TQ_BAND
                p_a, l_a = softmax_half(t, 0, c0)
                p_b, l_b = softmax_half(t, off_b, c0 + TQ_HALF)
                cols.append(jnp.concatenate([p_a, zeros_half], axis=0))
                cols.append(jnp.concatenate([zeros_half, p_b] if off_b else [p_b, zeros_half], axis=0))
                sums += [l_a, l_b]
            lo = pair * pair_width
            vt = vt_ref[0, lo:lo + pair_width, pl.ds(start, TK_BAND)]
            o = _dot(vt, jnp.concatenate(cols, axis=1))
            inv = 1.0 / jnp.concatenate(sums, axis=1)
            out_ref[0, lo:lo + B_HEAD_DIM, :] = (o[:B_HEAD_DIM, :TQ_BAND] * inv[:, :TQ_BAND]).astype(BF16)
            out_ref[0, lo + B_HEAD_DIM:lo + pair_width, :] = (
                o[B_HEAD_DIM:, TQ_BAND:] * inv[:, TQ_BAND:]).astype(BF16)

    pl.when(first_half >= 0)(functools.partial(tile, TQ_HALF, 0, 0))

    @pl.when(first_half < 0)
    def _():
        row_a = pl.multiple_of(-first_half * TQ_HALF, TQ_HALF)
        tile(0, row_a, pl.multiple_of(row_a - TQ_HALF, TQ_HALF))


def _band_attn(qt, k, vt, bias):
    bsz, seq, _ = k.shape
    return pl.pallas_call(
        _band_attn_kernel,
        grid=(bsz, seq // TQ_BAND),
        in_specs=[
            pl.BlockSpec((1, B_WIDTH, TQ_BAND), lambda b, j: (b, 0, j)),
            pl.BlockSpec((1, seq, B_WIDTH), lambda b, j: (b, 0, 0)),
            pl.BlockSpec((1, B_WIDTH, seq), lambda b, j: (b, 0, 0)),
            _resident(bias.shape),
        ],
        out_specs=pl.BlockSpec((1, B_WIDTH, TQ_BAND), lambda b, j: (b, 0, j)),
        out_shape=jax.ShapeDtypeStruct((bsz, B_WIDTH, seq), BF16),
        compiler_params=pltpu.CompilerParams(
            dimension_semantics=("arbitrary", "arbitrary"), vmem_limit_bytes=VMEM_LIMIT),
        name="band_attn",
    )(qt, k, vt, bias)


def kernel(x, mem, norm_mix_g, norm_ffn_g, a_w_in, a_gate_b, a_conv_w, a_conv_b, a_head_g, a_w_out, kv_norm_g, w_kv,
           b_w_in, b_rel_bias, b_w_out, mem_w_kv, ffn_w_up, ffn_conv_w, ffn_conv_b, ffn_w_down, final_g):
    bsz, seq, d = x.shape
    assert d == D_MODEL and seq % max(TM_IN, L_MLSTM, TM_POST, TQ_BAND) == 0
    assert a_w_in.shape[0] == 1 and b_w_in.shape[0] == 1, "one mLSTM layer followed by one band-attention layer"
    assert TM_IN == L_MLSTM, "the input projection emits per-chunk decay terms, one tile per mLSTM chunk"
    row = lambda g: g.reshape(1, -1).astype(F32)

    wkt = jnp.swapaxes(mem_w_kv[:, :, :MEM_WIDTH], 1, 2).astype(BF16)
    wmv = mem_w_kv[:, :, MEM_WIDTH:].astype(BF16)
    mkbdt, mvbd = _mem_kv(mem.astype(BF16), wkt, wmv)

    w = a_w_in[0]
    wqk = w[:, 0:2 * A_WIDTH].astype(BF16)
    wvt = jnp.swapaxes(w[:, 2 * A_WIDTH:3 * A_WIDTH], 0, 1).reshape(A_HEADS, A_HEAD_DIM, D_MODEL)
    wvt = jnp.pad(wvt, ((0, 0), (0, V_AUG - A_HEAD_DIM), (0, 0))).reshape(A_HEADS * V_AUG, D_MODEL).astype(BF16)
    wot = jnp.swapaxes(w[:, 3 * A_WIDTH:4 * A_WIDTH], 0, 1).astype(BF16)
    gate_pad = ((0, 0), (0, GATE_LANES - A_HEADS))
    g0 = 4 * A_WIDTH
    wgm = jnp.concatenate([jnp.pad(w[:, g0:g0 + A_HEADS], gate_pad), jnp.pad(w[:, g0 + A_HEADS:g0 + 2 * A_HEADS], gate_pad),
                           w[:, g0 + 2 * A_HEADS:]], axis=1).astype(BF16)
    gate_b = jnp.pad(a_gate_b[0].astype(F32).reshape(2, A_HEADS), gate_pad).reshape(1, 2 * GATE_LANES)
    vone = (jnp.arange(A_HEADS * V_AUG) % V_AUG == A_HEAD_DIM).astype(F32).reshape(A_HEADS * V_AUG, 1)
    q, k, vt, ot_half, gate_rows, a_cols, decay, qmem = _a_in(
        x, row(norm_mix_g[0]), wqk, wvt, wot, wgm, gate_b, vone, a_conv_w[0].astype(F32), row(a_conv_b[0]))

    head_g_half = 0.5 * a_head_g[0].reshape(A_HEADS, A_HEAD_DIM, 1).astype(F32)
    mix = _mlstm(q, k, vt, ot_half, gate_rows, a_cols, decay, head_g_half)

    def ffn_weights(l):
        return (row(norm_ffn_g[l]), ffn_w_up[l][:, :D_FF].astype(BF16), ffn_w_up[l][:, D_FF:].astype(BF16),
                ffn_conv_w[l].astype(F32), row(ffn_conv_b[l]), ffn_w_down[l].astype(BF16))

    wout = a_w_out[0]
    tail = (row(kv_norm_g), w_kv[:, :B_WIDTH].astype(BF16), jnp.swapaxes(w_kv[:, B_WIDTH:], 0, 1).astype(BF16),
            row(norm_mix_g[1]), jnp.swapaxes(b_w_in[0][:, :B_WIDTH], 0, 1).astype(BF16),
            b_w_in[0][:, B_WIDTH:].astype(BF16))
    x1, kb, vbt, qbt, qmem_b = _post_mixer(
        x, mix, qmem, mkbdt, mvbd, 0,
        wout[:A_WIDTH].astype(BF16), wout[A_WIDTH:].astype(BF16),
        *ffn_weights(0), tail=tail, last=False)

    bias = _band_bias(b_rel_bias[0].astype(F32))
    mix_b = _band_attn(qbt, kb, vbt, bias)
    wout = b_w_out[0]
    return _post_mixer(
        x1, mix_b, qmem_b, mkbdt, mvbd, 1,
        wout[:B_WIDTH].astype(BF16), wout[B_WIDTH:].astype(BF16),
        *ffn_weights(1), tail=(row(final_g),), last=True)
```

```python
import functools
import math

import jax
import jax.numpy as jnp
from jax import lax
from jax.experimental import pallas as pl
from jax.experimental.pallas import tpu as pltpu

F32 = jnp.float32
BF16 = jnp.bfloat16

D_MODEL = 1024
CHUNK = 64
MEM_TOKENS = 256
MEM_HEADS = 4
MEM_WIDTH = 256
MEM_HEAD_DIM = 64
A_WIDTH = 768
A_HEADS = 4
A_HEAD_DIM = 192
A_CONV = 4
B_HEADS = 12
B_HEAD_DIM = 64
B_WIDTH = 768
BAND_CHUNKS = 9
MAX_REL = 128
REL_SIZE = MAX_REL + CHUNK
D_FF = 2816
FFN_CONV = 3
EPS = 1e-6

V_AUG = 256
GATE_LANES = 128
HALO = 8
NEG_BIG = -1e30

TM_IN = 512
L_MLSTM = 256
MLSTM_BATCH = 2
TM_POST = 512
FF_CHUNK = 256
TQ_HALF = 128
TQ_BAND = 2 * TQ_HALF
BAND_PREV = (BAND_CHUNKS - 1) * CHUNK
TK_HALF = BAND_PREV + TQ_HALF
TK_BAND = BAND_PREV + TQ_BAND
HALVES_BACK = BAND_PREV // TQ_HALF
BIAS_ROWS = TK_HALF + BAND_PREV
BIAS_LANES = BIAS_ROWS + TQ_HALF
SOFTMAX_ROWS = 64

VMEM_LIMIT = 56 * 1024 * 1024


def _resident(shape):
    nd = len(shape)
    return pl.BlockSpec(shape, lambda *_: (0,) * nd, pipeline_mode=pl.Buffered(1))


def _rms(x, g):
    return x * lax.rsqrt(jnp.mean(x * x, axis=-1, keepdims=True) + EPS) * g


def _dot(a, b):
    return jnp.dot(a, b, preferred_element_type=F32)


def _dot_nt(a, b):
    return lax.dot_general(a, b, (((1,), (1,)), ((), ())), preferred_element_type=F32)


def _dot_tn(a, b):
    return lax.dot_general(a, b, (((0,), (0,)), ((), ())), preferred_element_type=F32)


def _sigmoid(x):
    return 0.5 + 0.5 * jnp.tanh(0.5 * x)


def _silu(x):
    half = 0.5 * x
    return half + half * jnp.tanh(half)


def _mem_kv_kernel(mem_ref, wkt_ref, wv_ref, mkbdt_ref, mvbd_ref):
    m = mem_ref[0]
    mkt = _dot_nt(wkt_ref[0], m)
    mv = _dot(m, wv_ref[0])
    f_idx = lax.broadcasted_iota(jnp.int32, (MEM_WIDTH, MEM_HEADS * MEM_TOKENS), 0)
    c_idx = lax.broadcasted_iota(jnp.int32, (MEM_WIDTH, MEM_HEADS * MEM_TOKENS), 1)
    mkt4 = jnp.concatenate([mkt] * MEM_HEADS, axis=1)
    mkbdt_ref[0, 0] = jnp.where(f_idx // MEM_HEAD_DIM == c_idx // MEM_TOKENS, mkt4, 0.0).astype(BF16)
    r_idx = lax.broadcasted_iota(jnp.int32, (MEM_HEADS * MEM_TOKENS, MEM_WIDTH), 0)
    g_idx = lax.broadcasted_iota(jnp.int32, (MEM_HEADS * MEM_TOKENS, MEM_WIDTH), 1)
    mv4 = jnp.concatenate([mv] * MEM_HEADS, axis=0)
    mvbd_ref[0, 0] = jnp.where(r_idx // MEM_TOKENS == g_idx // MEM_HEAD_DIM, mv4, 0.0).astype(BF16)


def _mem_kv(mem_bf, wkt, wv):
    depth = wkt.shape[0]
    bsz = mem_bf.shape[0]
    return pl.pallas_call(
        _mem_kv_kernel,
        grid=(depth, bsz),
        in_specs=[
            pl.BlockSpec((1, MEM_TOKENS, D_MODEL), lambda l, b: (b, 0, 0)),
            pl.BlockSpec((1, MEM_WIDTH, D_MODEL), lambda l, b: (l, 0, 0)),
            pl.BlockSpec((1, D_MODEL, MEM_WIDTH), lambda l, b: (l, 0, 0)),
        ],
        out_specs=[
            pl.BlockSpec((1, 1, MEM_WIDTH, MEM_HEADS * MEM_TOKENS), lambda l, b: (l, b, 0, 0)),
            pl.BlockSpec((1, 1, MEM_HEADS * MEM_TOKENS, MEM_WIDTH), lambda l, b: (l, b, 0, 0)),
        ],
        out_shape=[
            jax.ShapeDtypeStruct((depth, bsz, MEM_WIDTH, MEM_HEADS * MEM_TOKENS), BF16),
            jax.ShapeDtypeStruct((depth, bsz, MEM_HEADS * MEM_TOKENS, MEM_WIDTH), BF16),
        ],
        compiler_params=pltpu.CompilerParams(dimension_semantics=("arbitrary", "arbitrary")),
        name="mem_kv",
    )(mem_bf, wkt, wv)


def _causal_taps(x, prev, taps):
    n_taps = taps.shape[0]
    halo_row = lax.broadcasted_iota(jnp.int32, (HALO, x.shape[1]), 0)
    y = taps[n_taps - 1:n_taps, :] * x
    for back in range(1, n_taps):
        head = jnp.where(halo_row < back, pltpu.roll(prev, back, axis=0), pltpu.roll(x[:HALO, :], back, axis=0))
        shifted = jnp.concatenate([head, pltpu.roll(x, back, axis=0)[HALO:, :]], axis=0)
        y = y + taps[n_taps - 1 - back:n_taps - back, :] * shifted
    return y


def _lane_scan(x, op, identity):
    lane = lax.broadcasted_iota(jnp.int32, x.shape, 1)
    step = 1
    while step < x.shape[1]:
        x = op(x, jnp.where(lane >= step, pltpu.roll(x, step, axis=1), identity))
        step *= 2
    return x


def _a_in_kernel(x_ref, g_ref, wqk_ref, wvt_ref, wot_ref, wgm_ref, gate_b_ref, vone_ref, cw_ref, cb_ref,
                 q_ref, k_ref, vt_ref, ot_ref, rows_ref, acol_ref, decay_ref, qmem_ref, halo_ref, m_ref):
    L = L_MLSTM
    n_chunks = x_ref.shape[1] // L

    @pl.when(pl.program_id(1) == 0)
    def _():
        halo_ref[...] = jnp.zeros(halo_ref.shape, F32)
        m_ref[...] = jnp.zeros(m_ref.shape, F32)

    def projections(c):
        tok = slice(c * L, (c + 1) * L)
        hn = _rms(x_ref[0, tok, :], g_ref[...]).astype(BF16)

        gm = _dot(hn, wgm_ref[...])
        qmem_ref[0, tok, :] = gm[:, 2 * GATE_LANES:].astype(BF16)
        gi = (gm[:, :GATE_LANES] + gate_b_ref[:, :GATE_LANES]).T[0:8, :]
        fg = (gm[:, GATE_LANES:2 * GATE_LANES] + gate_b_ref[:, GATE_LANES:]).T[0:8, :]
        head_row = lax.broadcasted_iota(jnp.int32, (8, L), 0) < A_HEADS
        logf = jnp.where(head_row, jnp.minimum(fg, 0.0) - jnp.log(1.0 + jnp.exp(-jnp.abs(fg))), 0.0)
        b = _lane_scan(logf, jnp.add, 0.0)
        a = gi - b
        m_prev = m_ref[:, 0:1]
        g = jnp.maximum(_lane_scan(a, jnp.maximum, -jnp.inf), m_prev)
        g_last = g[:, L - 1:L]
        m_ref[...] = jnp.broadcast_to(b[:, L - 1:L] + g_last, m_ref.shape)
        decay_ref[0, c] = jnp.broadcast_to(jnp.exp(m_prev - g_last), decay_ref.shape[2:])
        rows_ref[0, :, tok] = jnp.concatenate(
            [g, jnp.exp(m_prev - g), jnp.exp(-(b + g)), jnp.exp(a - g_last)], axis=0)
        acol_ref[0, tok, :] = jnp.concatenate([a, jnp.zeros((GATE_LANES - 8, L), F32)], axis=0).T

        raw = _dot(hn, wqk_ref[...])
        vt_ref[0, :, tok] = (_dot_nt(wvt_ref[...], hn) + vone_ref[...]).astype(BF16)
        ot_ref[0, :, tok] = 0.5 * _dot_nt(wot_ref[...], hn)
        return raw

    def conv(c, raw):
        tok = slice(c * L, (c + 1) * L)
        prev = halo_ref[...]
        halo_ref[...] = raw[L - HALO:, :]
        qk = _silu(_causal_taps(raw, prev, cw_ref[...]) + cb_ref[...])
        for h in range(A_HEADS):
            lo = h * A_HEAD_DIM
            q_ref[0, h, tok, :] = qk[:, lo:lo + A_HEAD_DIM].astype(BF16)
            k_ref[0, h, tok, :] = (qk[:, A_WIDTH + lo:A_WIDTH + lo + A_HEAD_DIM]
                                   * (A_HEAD_DIM ** -0.5)).astype(BF16)

    raws = {}
    for slot in range(n_chunks + 1):
        if slot < n_chunks:
            raws[slot] = projections(slot)
        if slot >= 1:
            conv(slot - 1, raws.pop(slot - 1))


def _a_in(x, g, wqk, wvt, wot, wgm, gate_b, vone, cw, cb):
    bsz, seq, _ = x.shape
    tm = TM_IN
    head_spec = lambda width: pl.BlockSpec((1, A_HEADS, tm, width), lambda b, j: (b, 0, j, 0))
    row_spec = lambda width: pl.BlockSpec((1, tm, width), lambda b, j: (b, j, 0))
    col_spec = lambda width: pl.BlockSpec((1, width, tm), lambda b, j: (b, 0, j))
    weights = (g, wqk, wvt, wot, wgm, gate_b, vone, cw, cb)
    return pl.pallas_call(
        _a_in_kernel,
        grid=(bsz, seq // tm),
        in_specs=[row_spec(D_MODEL)] + [_resident(w.shape) for w in weights],
        out_specs=[
            head_spec(A_HEAD_DIM), head_spec(A_HEAD_DIM), col_spec(A_HEADS * V_AUG), col_spec(A_WIDTH),
            col_spec(4 * 8),
            row_spec(GATE_LANES),
            pl.BlockSpec((1, tm // L_MLSTM, 8, GATE_LANES), lambda b, j: (b, j, 0, 0)),
            row_spec(MEM_WIDTH),
        ],
        out_shape=[
            jax.ShapeDtypeStruct((bsz, A_HEADS, seq, A_HEAD_DIM), BF16),
            jax.ShapeDtypeStruct((bsz, A_HEADS, seq, A_HEAD_DIM), BF16),
            jax.ShapeDtypeStruct((bsz, A_HEADS * V_AUG, seq), BF16),
            jax.ShapeDtypeStruct((bsz, A_WIDTH, seq), F32),
            jax.ShapeDtypeStruct((bsz, 4 * 8, seq), F32),
            jax.ShapeDtypeStruct((bsz, seq, GATE_LANES), F32),
            jax.ShapeDtypeStruct((bsz, seq // L_MLSTM, 8, GATE_LANES), F32),
            jax.ShapeDtypeStruct((bsz, seq, MEM_WIDTH), BF16),
        ],
        scratch_shapes=[pltpu.VMEM((HALO, 2 * A_WIDTH), F32), pltpu.VMEM((8, GATE_LANES), F32)],
        compiler_params=pltpu.CompilerParams(
            dimension_semantics=("arbitrary", "arbitrary"), vmem_limit_bytes=VMEM_LIMIT),
        name="a_in",
    )(x, *weights)


def _mlstm_kernel(q_ref, k_ref, vt_ref, ot_ref, rows_ref, acol_ref, decay_ref, hg_ref, out_ref, ct_ref):
    L = q_ref.shape[2]
    half = L // 2

    @pl.when(pl.program_id(1) == 0)
    def _():
        ct_ref[...] = jnp.zeros(ct_ref.shape, F32)

    upper = (lax.broadcasted_iota(jnp.int32, (half, half), 0)
             <= lax.broadcasted_iota(jnp.int32, (half, half), 1))
    zero_block = jnp.zeros((half, half), BF16)
    items = [(n, h) for n in range(q_ref.shape[0]) for h in range(A_HEADS)]
    live = [dict() for _ in items]

    def scores(i):
        n, h = items[i]
        live[i]["st"] = _dot_nt(k_ref[n, h], q_ref[n, h])

    def decay_weight(i):
        n, h = items[i]
        st = live[i].pop("st")
        a_col = acol_ref[n, :, h:h + 1]
        g_row = rows_ref[n, h:h + 1, :]
        p00 = jnp.exp(jnp.where(upper, a_col[:half] - g_row[:, :half], -jnp.inf))
        p01 = jnp.exp(a_col[:half] - g_row[:, half:])
        p11 = jnp.exp(jnp.where(upper, a_col[half:] - g_row[:, half:], -jnp.inf))
        live[i]["s"] = jnp.concatenate([
            jnp.concatenate([(st[:half, :half] * p00).astype(BF16), (st[:half, half:] * p01).astype(BF16)], axis=1),
            jnp.concatenate([zero_block, (st[half:, half:] * p11).astype(BF16)], axis=1)], axis=0)

    def numerator(i):
        n, h = items[i]
        vt = vt_ref[n, h * V_AUG:(h + 1) * V_AUG, :]
        ct_prev = ct_ref[n, h]
        live[i]["num"] = (_dot(vt, live[i].pop("s"))
                          + _dot_nt(ct_prev.astype(BF16), q_ref[n, h]) * rows_ref[n, 8 + h:9 + h, :])
        wvt = (vt.astype(F32) * rows_ref[n, 24 + h:25 + h, :]).astype(BF16)
        ct_ref[n, h] = decay_ref[n, 0, h:h + 1, 0:1] * ct_prev + _dot(wvt, k_ref[n, h])

    def normalise(i):
        n, h = items[i]
        numt = live[i].pop("num")
        den = numt[A_HEAD_DIM:A_HEAD_DIM + 1, :]
        r = 1.0 / jnp.maximum(jnp.abs(den), rows_ref[n, 16 + h:17 + h, :])
        body = numt[:A_HEAD_DIM, :]
        ssq = jnp.sum(body * body, axis=0, keepdims=True)
        factor = r * lax.rsqrt(r * r * ssq * (1.0 / A_HEAD_DIM) + EPS)
        lo = h * A_HEAD_DIM
        gate = hg_ref[h] + hg_ref[h] * jnp.tanh(ot_ref[n, lo:lo + A_HEAD_DIM, :])
        out_ref[n, lo:lo + A_HEAD_DIM, :] = (body * factor * gate).astype(BF16)

    stages = (scores, decay_weight, numerator, normalise)
    for slot in range(len(items) + len(stages) - 1):
        for depth, stage in enumerate(stages):
            if 0 <= slot - depth < len(items):
                stage(slot - depth)


def _mlstm(q, k, vt, ot_half, rows, a_cols, decay, head_g_half):
    bsz, nh, seq, dh = q.shape
    L = L_MLSTM
    nb = MLSTM_BATCH
    head_spec = pl.BlockSpec((nb, nh, L, dh), lambda b, j: (b, 0, j, 0))
    col_spec = lambda width: pl.BlockSpec((nb, width, L), lambda b, j: (b, 0, j))
    return pl.pallas_call(
        _mlstm_kernel,
        grid=(bsz // nb, seq // L),
        in_specs=[
            head_spec, head_spec, col_spec(nh * V_AUG), col_spec(nh * dh), col_spec(rows.shape[1]),
            pl.BlockSpec((nb, L, GATE_LANES), lambda b, j: (b, j, 0)),
            pl.BlockSpec((nb, 1, 8, GATE_LANES), lambda b, j: (b, j, 0, 0)),
            _resident(head_g_half.shape),
        ],
        out_specs=col_spec(nh * dh),
        out_shape=jax.ShapeDtypeStruct((bsz, nh * dh, seq), BF16),
        scratch_shapes=[pltpu.VMEM((nb, nh, V_AUG, dh), F32)],
        compiler_params=pltpu.CompilerParams(
            dimension_semantics=("arbitrary", "arbitrary"), vmem_limit_bytes=VMEM_LIMIT),
        name="mlstm",
    )(q, k, vt, ot_half, rows, a_cols, decay, head_g_half)


def _post_mixer_kernel(*refs, last):
    if last:
        (x_ref, mix_ref, qmem_ref, mkbdt_ref, mvbd_ref, wout_mix_ref, wout_mem_ref, ffn_g_ref, wu_ref, wg_ref,
         cw_ref, cb_ref, wd_ref, final_g_ref, out_ref, halo_ref, act_ref) = refs
    else:
        (x_ref, mix_ref, qmem_ref, mkbdt_ref, mvbd_ref, wout_mix_ref, wout_mem_ref, ffn_g_ref, wu_ref, wg_ref,
         cw_ref, cb_ref, wd_ref, kv_g_ref, wk_ref, wvt_ref, nxt_g_ref, wqt_ref, wqmem_ref,
         out_ref, k_ref, vt_ref, qnt_ref, qmemn_ref, halo_ref, act_ref) = refs
    tm = x_ref.shape[1]

    @pl.when(pl.program_id(1) == 0)
    def _():
        halo_ref[...] = jnp.zeros((HALO, D_FF), F32)

    s = _dot(qmem_ref[0], mkbdt_ref[0, 0]) * (MEM_HEAD_DIM ** -0.5)
    probs = []
    for h in range(MEM_HEADS):
        sh = s[:, h * MEM_TOKENS:(h + 1) * MEM_TOKENS]
        e = jnp.exp(sh - jnp.max(sh, axis=1, keepdims=True))
        probs.append((e * (1.0 / jnp.sum(e, axis=1, keepdims=True))).astype(BF16))
    mem_out = _dot(jnp.concatenate(probs, axis=1), mvbd_ref[0, 0])

    y = _dot(mem_out.astype(BF16), wout_mem_ref[...])
    y = y + _dot_tn(mix_ref[0], wout_mix_ref[...])
    x1 = x_ref[0] + y

    hn = _rms(x1, ffn_g_ref[...]).astype(BF16)
    n_chunks = D_FF // FF_CHUNK
    chunk_cols = [slice(c * FF_CHUNK, (c + 1) * FF_CHUNK) for c in range(n_chunks)]

    def up(c):
        return _dot(hn, wu_ref[:, chunk_cols[c]]), _dot(hn, wg_ref[:, chunk_cols[c]])

    u, g = up(0)
    for c in range(n_chunks):
        cols = chunk_cols[c]
        nxt = up(c + 1) if c + 1 < n_chunks else None
        prev = halo_ref[:, cols]
        halo_ref[:, cols] = g[tm - HALO:, :]
        gc = _causal_taps(g, prev, cw_ref[:, cols]) + cb_ref[:, cols]
        act_ref[:, cols] = (_silu(gc) * u).astype(BF16)
        if nxt is not None:
            u, g = nxt
    x2 = x1 + _dot(act_ref[...], wd_ref[...])

    if last:
        out_ref[0] = _rms(x2, final_g_ref[...])
    else:
        out_ref[0] = x2
        hkv = _rms(x2, kv_g_ref[...]).astype(BF16)
        k_ref[0] = _dot(hkv, wk_ref[...]).astype(BF16)
        vt_ref[0] = _dot_nt(wvt_ref[...], hkv).astype(BF16)
        hq = _rms(x2, nxt_g_ref[...]).astype(BF16)
        qnt_ref[0] = (_dot_nt(wqt_ref[...], hq) * (B_HEAD_DIM ** -0.5)).astype(BF16)
        qmemn_ref[0] = _dot(hq, wqmem_ref[...]).astype(BF16)


def _post_mixer(x, mix, qmem, mkbdt, mvbd, layer, wout_mix, wout_mem, ffn_g, wu, wg, cw, cb, wd, tail, last):
    bsz, seq, _ = x.shape
    tm = TM_POST
    row_spec = lambda width: pl.BlockSpec((1, tm, width), lambda b, j: (b, j, 0))
    col_spec = lambda width: pl.BlockSpec((1, width, tm), lambda b, j: (b, 0, j))
    mix_spec = col_spec(mix.shape[1])
    weights = (wout_mix, wout_mem, ffn_g, wu, wg, cw, cb, wd) + tuple(tail)
    in_specs = [
        row_spec(D_MODEL), mix_spec, row_spec(MEM_WIDTH),
        pl.BlockSpec((1, 1) + mkbdt.shape[2:], lambda b, j: (layer, b, 0, 0)),
        pl.BlockSpec((1, 1) + mvbd.shape[2:], lambda b, j: (layer, b, 0, 0)),
    ] + [_resident(w.shape) for w in weights]
    if last:
        out_specs = row_spec(D_MODEL)
        out_shape = jax.ShapeDtypeStruct((bsz, seq, D_MODEL), F32)
    else:
        out_specs = [row_spec(D_MODEL), row_spec(B_WIDTH), col_spec(B_WIDTH), col_spec(B_WIDTH), row_spec(MEM_WIDTH)]
        out_shape = [
            jax.ShapeDtypeStruct((bsz, seq, D_MODEL), F32),
            jax.ShapeDtypeStruct((bsz, seq, B_WIDTH), BF16),
            jax.ShapeDtypeStruct((bsz, B_WIDTH, seq), BF16),
            jax.ShapeDtypeStruct((bsz, B_WIDTH, seq), BF16),
            jax.ShapeDtypeStruct((bsz, seq, MEM_WIDTH), BF16),
        ]
    return pl.pallas_call(
        functools.partial(_post_mixer_kernel, last=last),
        grid=(bsz, seq // tm),
        in_specs=in_specs,
        out_specs=out_specs,
        out_shape=out_shape,
        scratch_shapes=[pltpu.VMEM((HALO, D_FF), F32), pltpu.VMEM((tm, D_FF), BF16)],
        compiler_params=pltpu.CompilerParams(
            dimension_semantics=("arbitrary", "arbitrary"), vmem_limit_bytes=VMEM_LIMIT),
        name="post_mixer_last" if last else "post_mixer",
    )(x, mix, qmem, mkbdt, mvbd, *weights)


def _band_bias_kernel(tbl_ref, out_ref):
    h = pl.program_id(0)
    b = lax.broadcasted_iota(jnp.int32, (8, BIAS_LANES), 0)
    c = lax.broadcasted_iota(jnp.int32, (8, BIAS_LANES), 1)
    r_signed = jnp.where(c < TQ_HALF, c, c - BIAS_LANES)
    idx = jnp.clip(BAND_PREV + r_signed - b, -(CHUNK - 1), MAX_REL) + (CHUNK - 1)

    def body(e, acc):
        return jnp.where(idx == e, tbl_ref[h, e], acc)

    base = lax.fori_loop(0, REL_SIZE, body, jnp.zeros((8, BIAS_LANES), F32), unroll=8)
    kb = lax.broadcasted_iota(jnp.int32, (8, TQ_HALF), 0)
    r = lax.broadcasted_iota(jnp.int32, (8, TQ_HALF), 1)
    for a in range(BIAS_ROWS // 8):
        first = (-8 * a) % BIAS_LANES
        lo = first // TQ_HALF * TQ_HALF
        hi = (lo + TQ_HALF) % BIAS_LANES
        window = jnp.concatenate([base[:, lo:lo + TQ_HALF], base[:, hi:hi + TQ_HALF]], axis=1)
        if first != lo:
            window = pltpu.roll(window, 2 * TQ_HALF - (first - lo), axis=1)
        dchunk = (BAND_PREV + r) // CHUNK - (8 * a + kb) // CHUNK
        valid = (dchunk >= 0) & (dchunk < BAND_CHUNKS)
        out_ref[0, 8 * a:8 * a + 8, :] = jnp.where(valid, window[:, :TQ_HALF], NEG_BIG)


def _band_bias(rel_table):
    return pl.pallas_call(
        _band_bias_kernel,
        grid=(B_HEADS,),
        in_specs=[pl.BlockSpec(memory_space=pltpu.SMEM)],
        out_specs=pl.BlockSpec((1, BIAS_ROWS, TQ_HALF), lambda h: (h, 0, 0)),
        out_shape=jax.ShapeDtypeStruct((B_HEADS, BIAS_ROWS, TQ_HALF), F32),
        compiler_params=pltpu.CompilerParams(dimension_semantics=("arbitrary",)),
        name="band_bias",
    )(rel_table)


def _band_attn_kernel(qt_ref, k_ref, vt_ref, bias_ref, out_ref):
    j = pl.program_id(1)
    first_half = 2 * j - HALVES_BACK
    start = pl.multiple_of(jnp.maximum(first_half, 0) * TQ_HALF, TQ_HALF)
    pair_width = 2 * B_HEAD_DIM
    pair_row = lax.broadcasted_iota(jnp.int32, (pair_width, TQ_BAND), 0)
    zeros_half = jnp.zeros((TQ_HALF, TQ_HALF), BF16)
    masked_half = jnp.full((TQ_HALF, TQ_HALF), NEG_BIG, F32)

    def scores(pair):
        lo = pair * pair_width
        k_pair = k_ref[0, pl.ds(start, TK_BAND), lo:lo + pair_width]
        q_pair = qt_ref[0, lo:lo + pair_width, :]
        zero = jnp.zeros_like(q_pair)
        q_sel = jnp.concatenate([jnp.where(pair_row < B_HEAD_DIM, q_pair, zero),
                                 jnp.where(pair_row >= B_HEAD_DIM, q_pair, zero)], axis=1)
        return _dot(k_pair, q_sel)

    def softmax_half(t, row0, c0):
        blocks = [t[row0 + r:row0 + r + SOFTMAX_ROWS, c0:c0 + TQ_HALF] for r in range(0, TK_HALF, SOFTMAX_ROWS)]
        m_acc = blocks[0]
        for blk in blocks[1:]:
            m_acc = jnp.maximum(m_acc, blk)
        m = jnp.max(m_acc, axis=0, keepdims=True)
        ps = []
        l_acc = None
        for blk in blocks:
            e = jnp.exp(blk - m)
            ps.append(e.astype(BF16))
            l_acc = e if l_acc is None else l_acc + e
        return jnp.concatenate(ps, axis=0), jnp.sum(l_acc, axis=0, keepdims=True)

    def tile(off_b, bias_row_a, bias_row_b):
        def bias_tile(pair):
            cols = []
            for h in (2 * pair, 2 * pair + 1):
                bias_a = bias_ref[h, pl.ds(bias_row_a, TK_HALF), :]
                bias_b = bias_ref[h, pl.ds(bias_row_b, TK_HALF), :]
                cols.append(jnp.concatenate([bias_a, masked_half], axis=0))
                cols.append(jnp.concatenate([masked_half, bias_b] if off_b else [bias_b, masked_half], axis=0))
            return jnp.concatenate(cols, axis=1)

        n_pairs = B_HEADS // 2
        live = [dict() for _ in range(n_pairs)]

        def biased_scores(pair):
            live[pair]["t"] = scores(pair) + bias_tile(pair)

        def softmax(pair):
            t = live[pair].pop("t")
            cols, sums = [], []
            for parity in range(2):
                c0 = parity * TQ_BAND
                p_a, l_a = softmax_half(t, 0, c0)
                p_b, l_b = softmax_half(t, off_b, c0 + TQ_HALF)
                cols.append(jnp.concatenate([p_a, zeros_half], axis=0))
                cols.append(jnp.concatenate([zeros_half, p_b] if off_b else [p_b, zeros_half], axis=0))
                sums += [l_a, l_b]
            live[pair]["p"] = jnp.concatenate(cols, axis=1)
            live[pair]["inv"] = 1.0 / jnp.concatenate(sums, axis=1)

        def values(pair):
            lo = pair * pair_width
            vt = vt_ref[0, lo:lo + pair_width, pl.ds(start, TK_BAND)]
            o = _dot(vt, live[pair].pop("p"))
            inv = live[pair].pop("inv")
            out_ref[0, lo:lo + B_HEAD_DIM, :] = (o[:B_HEAD_DIM, :TQ_BAND] * inv[:, :TQ_BAND]).astype(BF16)
            out_ref[0, lo + B_HEAD_DIM:lo + pair_width, :] = (
                o[B_HEAD_DIM:, TQ_BAND:] * inv[:, TQ_BAND:]).astype(BF16)

        stages = (biased_scores, softmax, values)
        for slot in range(n_pairs + len(stages) - 1):
            for depth in (0, 2, 1):
                if 0 <= slot - depth < n_pairs:
                    stages[depth](slot - depth)

    pl.when(first_half >= 0)(functools.partial(tile, TQ_HALF, 0, 0))

    @pl.when(first_half < 0)
    def _():
        row_a = pl.multiple_of(-first_half * TQ_HALF, TQ_HALF)
        tile(0, row_a, pl.multiple_of(row_a - TQ_HALF, TQ_HALF))


def _band_attn(qt, k, vt, bias):
    bsz, seq, _ = k.shape
    return pl.pallas_call(
        _band_attn_kernel,
        grid=(bsz, seq // TQ_BAND),
        in_specs=[
            pl.BlockSpec((1, B_WIDTH, TQ_BAND), lambda b, j: (b, 0, j)),
            pl.BlockSpec((1, seq, B_WIDTH), lambda b, j: (b, 0, 0)),
            pl.BlockSpec((1, B_WIDTH, seq), lambda b, j: (b, 0, 0)),
            _resident(bias.shape),
        ],
        out_specs=pl.BlockSpec((1, B_WIDTH, TQ_BAND), lambda b, j: (b, 0, j)),
        out_shape=jax.ShapeDtypeStruct((bsz, B_WIDTH, seq), BF16),
        compiler_params=pltpu.CompilerParams(
            dimension_semantics=("arbitrary", "arbitrary"), vmem_limit_bytes=VMEM_LIMIT),
        name="band_attn",
    )(qt, k, vt, bias)


def kernel(x, mem, norm_mix_g, norm_ffn_g, a_w_in, a_gate_b, a_conv_w, a_conv_b, a_head_g, a_w_out, kv_norm_g, w_kv,
           b_w_in, b_rel_bias, b_w_out, mem_w_kv, ffn_w_up, ffn_conv_w, ffn_conv_b, ffn_w_down, final_g):
    bsz, seq, d = x.shape
    assert d == D_MODEL and seq % max(TM_IN, L_MLSTM, TM_POST, TQ_BAND) == 0
    assert a_w_in.shape[0] == 1 and b_w_in.shape[0] == 1, "one mLSTM layer followed by one band-attention layer"
    assert TM_IN % L_MLSTM == 0, "the input projection emits per-chunk decay terms: whole chunks per tile"
    assert bsz % MLSTM_BATCH == 0
    row = lambda g: g.reshape(1, -1).astype(F32)

    wkt = jnp.swapaxes(mem_w_kv[:, :, :MEM_WIDTH], 1, 2).astype(BF16)
    wmv = mem_w_kv[:, :, MEM_WIDTH:].astype(BF16)
    mkbdt, mvbd = _mem_kv(mem.astype(BF16), wkt, wmv)

    w = a_w_in[0]
    wqk = w[:, 0:2 * A_WIDTH].astype(BF16)
    wvt = jnp.swapaxes(w[:, 2 * A_WIDTH:3 * A_WIDTH], 0, 1).reshape(A_HEADS, A_HEAD_DIM, D_MODEL)
    wvt = jnp.pad(wvt, ((0, 0), (0, V_AUG - A_HEAD_DIM), (0, 0))).reshape(A_HEADS * V_AUG, D_MODEL).astype(BF16)
    wot = jnp.swapaxes(w[:, 3 * A_WIDTH:4 * A_WIDTH], 0, 1).astype(BF16)
    gate_pad = ((0, 0), (0, GATE_LANES - A_HEADS))
    g0 = 4 * A_WIDTH
    wgm = jnp.concatenate([jnp.pad(w[:, g0:g0 + A_HEADS], gate_pad), jnp.pad(w[:, g0 + A_HEADS:g0 + 2 * A_HEADS], gate_pad),
                           w[:, g0 + 2 * A_HEADS:]], axis=1).astype(BF16)
    gate_b = jnp.pad(a_gate_b[0].astype(F32).reshape(2, A_HEADS), gate_pad).reshape(1, 2 * GATE_LANES)
    vone = (jnp.arange(A_HEADS * V_AUG) % V_AUG == A_HEAD_DIM).astype(F32).reshape(A_HEADS * V_AUG, 1)
    q, k, vt, ot_half, gate_rows, a_cols, decay, qmem = _a_in(
        x, row(norm_mix_g[0]), wqk, wvt, wot, wgm, gate_b, vone, a_conv_w[0].astype(F32), row(a_conv_b[0]))

    head_g_half = 0.5 * a_head_g[0].reshape(A_HEADS, A_HEAD_DIM, 1).astype(F32)
    mix = _mlstm(q, k, vt, ot_half, gate_rows, a_cols, decay, head_g_half)

    def ffn_weights(l):
        return (row(norm_ffn_g[l]), ffn_w_up[l][:, :D_FF].astype(BF16), ffn_w_up[l][:, D_FF:].astype(BF16),
                ffn_conv_w[l].astype(F32), row(ffn_conv_b[l]), ffn_w_down[l].astype(BF16))

    wout = a_w_out[0]
    tail = (row(kv_norm_g), w_kv[:, :B_WIDTH].astype(BF16), jnp.swapaxes(w_kv[:, B_WIDTH:], 0, 1).astype(BF16),
            row(norm_mix_g[1]), jnp.swapaxes(b_w_in[0][:, :B_WIDTH], 0, 1).astype(BF16),
            b_w_in[0][:, B_WIDTH:].astype(BF16))
    x1, kb, vbt, qbt, qmem_b = _post_mixer(
        x, mix, qmem, mkbdt, mvbd, 0,
        wout[:A_WIDTH].astype(BF16), wout[A_WIDTH:].astype(BF16),
        *ffn_weights(0), tail=tail, last=False)

    bias = _band_bias(b_rel_bias[0].astype(F32))
    mix_b = _band_attn(qbt, kb, vbt, bias)
    wout = b_w_out[0]
    return _post_mixer(
        x1, mix_b, qmem_b, mkbdt, mvbd, 1,
        wout[:B_WIDTH].astype(BF16), wout[B_WIDTH:].astype(BF16),
        *ffn_weights(1), tail=(row(final_g),), last=True)
```

```python
import functools
import math

import jax
import jax.numpy as jnp
from jax import lax
from jax.experimental import pallas as pl
from jax.experimental.pallas import tpu as pltpu

F32 = jnp.float32
BF16 = jnp.bfloat16

D_MODEL = 1024
CHUNK = 64
MEM_TOKENS = 256
MEM_HEADS = 4
MEM_WIDTH = 256
MEM_HEAD_DIM = 64
A_WIDTH = 768
A_HEADS = 4
A_HEAD_DIM = 192
A_CONV = 4
B_HEADS = 12
B_HEAD_DIM = 64
B_WIDTH = 768
BAND_CHUNKS = 9
MAX_REL = 128
REL_SIZE = MAX_REL + CHUNK
D_FF = 2816
FFN_CONV = 3
EPS = 1e-6

V_AUG = 256
GATE_LANES = 128
HALO = 8
NEG_BIG = -1e30

TM_IN = 512
L_MLSTM = 256
MLSTM_BATCH = 2
TM_POST = 512
FF_CHUNK = 256
TQ_HALF = 128
TQ_BAND = 2 * TQ_HALF
BAND_PREV = (BAND_CHUNKS - 1) * CHUNK
TK_HALF = BAND_PREV + TQ_HALF
TK_BAND = BAND_PREV + TQ_BAND
HALVES_BACK = BAND_PREV // TQ_HALF
BIAS_ROWS = TK_HALF + BAND_PREV
BIAS_LANES = BIAS_ROWS + TQ_HALF
SOFTMAX_ROWS = 64

VMEM_LIMIT = 56 * 1024 * 1024


def _resident(shape):
    nd = len(shape)
    return pl.BlockSpec(shape, lambda *_: (0,) * nd, pipeline_mode=pl.Buffered(1))


def _rms(x, g):
    return x * lax.rsqrt(jnp.mean(x * x, axis=-1, keepdims=True) + EPS) * g


def _dot(a, b):
    return jnp.dot(a, b, preferred_element_type=F32)


def _dot_nt(a, b):
    return lax.dot_general(a, b, (((1,), (1,)), ((), ())), preferred_element_type=F32)


def _dot_tn(a, b):
    return lax.dot_general(a, b, (((0,), (0,)), ((), ())), preferred_element_type=F32)


def _sigmoid(x):
    return 0.5 + 0.5 * jnp.tanh(0.5 * x)


def _silu(x):
    half = 0.5 * x
    return half + half * jnp.tanh(half)


def _mem_kv_kernel(mem_ref, wkt_ref, wv_ref, mkbdt_ref, mvbd_ref):
    m = mem_ref[0]
    mkt = _dot_nt(wkt_ref[0], m)
    mv = _dot(m, wv_ref[0])
    f_idx = lax.broadcasted_iota(jnp.int32, (MEM_WIDTH, MEM_HEADS * MEM_TOKENS), 0)
    c_idx = lax.broadcasted_iota(jnp.int32, (MEM_WIDTH, MEM_HEADS * MEM_TOKENS), 1)
    mkt4 = jnp.concatenate([mkt] * MEM_HEADS, axis=1)
    mkbdt_ref[0, 0] = jnp.where(f_idx // MEM_HEAD_DIM == c_idx // MEM_TOKENS, mkt4, 0.0).astype(BF16)
    r_idx = lax.broadcasted_iota(jnp.int32, (MEM_HEADS * MEM_TOKENS, MEM_WIDTH), 0)
    g_idx = lax.broadcasted_iota(jnp.int32, (MEM_HEADS * MEM_TOKENS, MEM_WIDTH), 1)
    mv4 = jnp.concatenate([mv] * MEM_HEADS, axis=0)
    mvbd_ref[0, 0] = jnp.where(r_idx // MEM_TOKENS == g_idx // MEM_HEAD_DIM, mv4, 0.0).astype(BF16)


def _mem_kv(mem_bf, wkt, wv):
    depth = wkt.shape[0]
    bsz = mem_bf.shape[0]
    return pl.pallas_call(
        _mem_kv_kernel,
        grid=(depth, bsz),
        in_specs=[
            pl.BlockSpec((1, MEM_TOKENS, D_MODEL), lambda l, b: (b, 0, 0)),
            pl.BlockSpec((1, MEM_WIDTH, D_MODEL), lambda l, b: (l, 0, 0)),
            pl.BlockSpec((1, D_MODEL, MEM_WIDTH), lambda l, b: (l, 0, 0)),
        ],
        out_specs=[
            pl.BlockSpec((1, 1, MEM_WIDTH, MEM_HEADS * MEM_TOKENS), lambda l, b: (l, b, 0, 0)),
            pl.BlockSpec((1, 1, MEM_HEADS * MEM_TOKENS, MEM_WIDTH), lambda l, b: (l, b, 0, 0)),
        ],
        out_shape=[
            jax.ShapeDtypeStruct((depth, bsz, MEM_WIDTH, MEM_HEADS * MEM_TOKENS), BF16),
            jax.ShapeDtypeStruct((depth, bsz, MEM_HEADS * MEM_TOKENS, MEM_WIDTH), BF16),
        ],
        compiler_params=pltpu.CompilerParams(dimension_semantics=("arbitrary", "arbitrary")),
        name="mem_kv",
    )(mem_bf, wkt, wv)


def _causal_taps(x, prev, taps):
    n_taps = taps.shape[0]
    halo_row = lax.broadcasted_iota(jnp.int32, (HALO, x.shape[1]), 0)
    y = taps[n_taps - 1:n_taps, :] * x
    for back in range(1, n_taps):
        head = jnp.where(halo_row < back, pltpu.roll(prev, back, axis=0), pltpu.roll(x[:HALO, :], back, axis=0))
        shifted = jnp.concatenate([head, pltpu.roll(x, back, axis=0)[HALO:, :]], axis=0)
        y = y + taps[n_taps - 1 - back:n_taps - back, :] * shifted
    return y


def _lane_scan(x, op, identity):
    lane = lax.broadcasted_iota(jnp.int32, x.shape, 1)
    step = 1
    while step < x.shape[1]:
        x = op(x, jnp.where(lane >= step, pltpu.roll(x, step, axis=1), identity))
        step *= 2
    return x


def _a_in_kernel(x_ref, g_ref, wqk_ref, wvt_ref, wot_ref, wgm_ref, gate_b_ref, vone_ref, cw_ref, cb_ref,
                 q_ref, k_ref, vt_ref, ot_ref, rows_ref, acol_ref, decay_ref, qmem_ref, halo_ref, m_ref):
    L = L_MLSTM
    n_chunks = x_ref.shape[1] // L

    @pl.when(pl.program_id(1) == 0)
    def _():
        halo_ref[...] = jnp.zeros(halo_ref.shape, F32)
        m_ref[...] = jnp.zeros(m_ref.shape, F32)

    def projections(c):
        tok = slice(c * L, (c + 1) * L)
        hn = _rms(x_ref[0, tok, :], g_ref[...]).astype(BF16)

        gm = _dot(hn, wgm_ref[...])
        qmem_ref[0, tok, :] = gm[:, 2 * GATE_LANES:].astype(BF16)
        gi = (gm[:, :GATE_LANES] + gate_b_ref[:, :GATE_LANES]).T[0:8, :]
        fg = (gm[:, GATE_LANES:2 * GATE_LANES] + gate_b_ref[:, GATE_LANES:]).T[0:8, :]
        head_row = lax.broadcasted_iota(jnp.int32, (8, L), 0) < A_HEADS
        logf = jnp.where(head_row, jnp.minimum(fg, 0.0) - jnp.log(1.0 + jnp.exp(-jnp.abs(fg))), 0.0)
        b = _lane_scan(logf, jnp.add, 0.0)
        a = gi - b
        m_prev = m_ref[:, 0:1]
        g = jnp.maximum(_lane_scan(a, jnp.maximum, -jnp.inf), m_prev)
        g_last = g[:, L - 1:L]
        m_ref[...] = jnp.broadcast_to(b[:, L - 1:L] + g_last, m_ref.shape)
        decay_ref[0, c] = jnp.broadcast_to(jnp.exp(m_prev - g_last), decay_ref.shape[2:])
        rows_ref[0, :, tok] = jnp.concatenate(
            [g, jnp.exp(m_prev - g), jnp.exp(-(b + g)), jnp.exp(a - g_last)], axis=0)
        acol_ref[0, tok, :] = jnp.concatenate([a, jnp.zeros((GATE_LANES - 8, L), F32)], axis=0).T

        raw = _dot(hn, wqk_ref[...])
        vt_ref[0, c] = (_dot_nt(wvt_ref[...], hn) + vone_ref[...]).astype(BF16)
        ot_ref[0, c] = 0.5 * _dot_nt(wot_ref[...], hn)
        return raw

    def conv(c, raw):
        tok = slice(c * L, (c + 1) * L)
        prev = halo_ref[...]
        halo_ref[...] = raw[L - HALO:, :]
        qk = _silu(_causal_taps(raw, prev, cw_ref[...]) + cb_ref[...])
        for h in range(A_HEADS):
            lo = h * A_HEAD_DIM
            q_ref[0, h, tok, :] = qk[:, lo:lo + A_HEAD_DIM].astype(BF16)
            k_ref[0, h, tok, :] = (qk[:, A_WIDTH + lo:A_WIDTH + lo + A_HEAD_DIM]
                                   * (A_HEAD_DIM ** -0.5)).astype(BF16)

    raws = {}
    for slot in range(n_chunks + 1):
        if slot < n_chunks:
            raws[slot] = projections(slot)
        if slot >= 1:
            conv(slot - 1, raws.pop(slot - 1))


def _a_in(x, g, wqk, wvt, wot, wgm, gate_b, vone, cw, cb):
    bsz, seq, _ = x.shape
    tm = TM_IN
    head_spec = lambda width: pl.BlockSpec((1, A_HEADS, tm, width), lambda b, j: (b, 0, j, 0))
    row_spec = lambda width: pl.BlockSpec((1, tm, width), lambda b, j: (b, j, 0))
    col_spec = lambda width: pl.BlockSpec((1, width, tm), lambda b, j: (b, 0, j))
    tile_spec = lambda width: pl.BlockSpec((1, tm // L_MLSTM, width, L_MLSTM), lambda b, j: (b, j, 0, 0))
    weights = (g, wqk, wvt, wot, wgm, gate_b, vone, cw, cb)
    return pl.pallas_call(
        _a_in_kernel,
        grid=(bsz, seq // tm),
        in_specs=[row_spec(D_MODEL)] + [_resident(w.shape) for w in weights],
        out_specs=[
            head_spec(A_HEAD_DIM), head_spec(A_HEAD_DIM), tile_spec(A_HEADS * V_AUG), tile_spec(A_WIDTH),
            col_spec(4 * 8),
            row_spec(GATE_LANES),
            pl.BlockSpec((1, tm // L_MLSTM, 8, GATE_LANES), lambda b, j: (b, j, 0, 0)),
            row_spec(MEM_WIDTH),
        ],
        out_shape=[
            jax.ShapeDtypeStruct((bsz, A_HEADS, seq, A_HEAD_DIM), BF16),
            jax.ShapeDtypeStruct((bsz, A_HEADS, seq, A_HEAD_DIM), BF16),
            jax.ShapeDtypeStruct((bsz, seq // L_MLSTM, A_HEADS * V_AUG, L_MLSTM), BF16),
            jax.ShapeDtypeStruct((bsz, seq // L_MLSTM, A_WIDTH, L_MLSTM), F32),
            jax.ShapeDtypeStruct((bsz, 4 * 8, seq), F32),
            jax.ShapeDtypeStruct((bsz, seq, GATE_LANES), F32),
            jax.ShapeDtypeStruct((bsz, seq // L_MLSTM, 8, GATE_LANES), F32),
            jax.ShapeDtypeStruct((bsz, seq, MEM_WIDTH), BF16),
        ],
        scratch_shapes=[pltpu.VMEM((HALO, 2 * A_WIDTH), F32), pltpu.VMEM((8, GATE_LANES), F32)],
        compiler_params=pltpu.CompilerParams(
            dimension_semantics=("arbitrary", "arbitrary"), vmem_limit_bytes=VMEM_LIMIT),
        name="a_in",
    )(x, *weights)


def _mlstm_kernel(q_ref, k_ref, vt_ref, ot_ref, rows_ref, acol_ref, decay_ref, hg_ref, out_ref, ct_ref):
    L = q_ref.shape[2]
    half = L // 2

    @pl.when(pl.program_id(1) == 0)
    def _():
        ct_ref[...] = jnp.zeros(ct_ref.shape, F32)

    upper = (lax.broadcasted_iota(jnp.int32, (half, half), 0)
             <= lax.broadcasted_iota(jnp.int32, (half, half), 1))
    zero_block = jnp.zeros((half, half), BF16)
    items = [(n, h) for n in range(q_ref.shape[0]) for h in range(A_HEADS)]
    live = [dict() for _ in items]

    def scores(i):
        n, h = items[i]
        live[i]["st"] = _dot_nt(k_ref[n, h], q_ref[n, h])

    def decay_weight(i):
        n, h = items[i]
        st = live[i].pop("st")
        a_col = acol_ref[n, :, h:h + 1]
        g_row = rows_ref[n, h:h + 1, :]
        p00 = jnp.exp(jnp.where(upper, a_col[:half] - g_row[:, :half], -jnp.inf))
        p01 = jnp.exp(a_col[:half] - g_row[:, half:])
        p11 = jnp.exp(jnp.where(upper, a_col[half:] - g_row[:, half:], -jnp.inf))
        live[i]["s"] = jnp.concatenate([
            jnp.concatenate([(st[:half, :half] * p00).astype(BF16), (st[:half, half:] * p01).astype(BF16)], axis=1),
            jnp.concatenate([zero_block, (st[half:, half:] * p11).astype(BF16)], axis=1)], axis=0)

    def numerator(i):
        n, h = items[i]
        vt = vt_ref[n, 0, h * V_AUG:(h + 1) * V_AUG, :]
        ct_prev = ct_ref[n, h]
        live[i]["num"] = (_dot(vt, live[i].pop("s"))
                          + _dot_nt(ct_prev.astype(BF16), q_ref[n, h]) * rows_ref[n, 8 + h:9 + h, :])
        wvt = (vt.astype(F32) * rows_ref[n, 24 + h:25 + h, :]).astype(BF16)
        ct_ref[n, h] = decay_ref[n, 0, h:h + 1, 0:1] * ct_prev + _dot(wvt, k_ref[n, h])

    def normalise(i):
        n, h = items[i]
        numt = live[i].pop("num")
        den = numt[A_HEAD_DIM:A_HEAD_DIM + 1, :]
        r = 1.0 / jnp.maximum(jnp.abs(den), rows_ref[n, 16 + h:17 + h, :])
        body = numt[:A_HEAD_DIM, :]
        ssq = jnp.sum(body * body, axis=0, keepdims=True)
        factor = r * lax.rsqrt(r * r * ssq * (1.0 / A_HEAD_DIM) + EPS)
        lo = h * A_HEAD_DIM
        gate = hg_ref[h] + hg_ref[h] * jnp.tanh(ot_ref[n, 0, lo:lo + A_HEAD_DIM, :])
        out_ref[n, 0, lo:lo + A_HEAD_DIM, :] = (body * factor * gate).astype(BF16)

    stages = (scores, decay_weight, numerator, normalise)
    for slot in range(len(items) + len(stages) - 1):
        for depth, stage in enumerate(stages):
            if 0 <= slot - depth < len(items):
                stage(slot - depth)


def _mlstm(q, k, vt, ot_half, rows, a_cols, decay, head_g_half):
    bsz, nh, seq, dh = q.shape
    L = L_MLSTM
    nb = MLSTM_BATCH
    head_spec = pl.BlockSpec((nb, nh, L, dh), lambda b, j: (b, 0, j, 0))
    col_spec = lambda width: pl.BlockSpec((nb, width, L), lambda b, j: (b, 0, j))
    tile_spec = lambda width: pl.BlockSpec((nb, 1, width, L), lambda b, j: (b, j, 0, 0))
    return pl.pallas_call(
        _mlstm_kernel,
        grid=(bsz // nb, seq // L),
        in_specs=[
            head_spec, head_spec, tile_spec(nh * V_AUG), tile_spec(nh * dh), col_spec(rows.shape[1]),
            pl.BlockSpec((nb, L, GATE_LANES), lambda b, j: (b, j, 0)),
            pl.BlockSpec((nb, 1, 8, GATE_LANES), lambda b, j: (b, j, 0, 0)),
            _resident(head_g_half.shape),
        ],
        out_specs=tile_spec(nh * dh),
        out_shape=jax.ShapeDtypeStruct((bsz, seq // L, nh * dh, L), BF16),
        scratch_shapes=[pltpu.VMEM((nb, nh, V_AUG, dh), F32)],
        compiler_params=pltpu.CompilerParams(
            dimension_semantics=("arbitrary", "arbitrary"), vmem_limit_bytes=VMEM_LIMIT),
        name="mlstm",
    )(q, k, vt, ot_half, rows, a_cols, decay, head_g_half)


def _post_mixer_kernel(*refs, last):
    if last:
        (x_ref, mix_ref, qmem_ref, mkbdt_ref, mvbd_ref, wout_mix_ref, wout_mem_ref, ffn_g_ref, wu_ref, wg_ref,
         cw_ref, cb_ref, wd_ref, final_g_ref, out_ref, halo_ref, act_ref) = refs
    else:
        (x_ref, mix_ref, qmem_ref, mkbdt_ref, mvbd_ref, wout_mix_ref, wout_mem_ref, ffn_g_ref, wu_ref, wg_ref,
         cw_ref, cb_ref, wd_ref, kv_g_ref, wk_ref, wvt_ref, nxt_g_ref, wqt_ref, wqmem_ref,
         out_ref, k_ref, vt_ref, qnt_ref, qmemn_ref, halo_ref, act_ref) = refs
    tm = x_ref.shape[1]

    @pl.when(pl.program_id(1) == 0)
    def _():
        halo_ref[...] = jnp.zeros((HALO, D_FF), F32)

    s = _dot(qmem_ref[0], mkbdt_ref[0, 0]) * (MEM_HEAD_DIM ** -0.5)
    probs = []
    for h in range(MEM_HEADS):
        sh = s[:, h * MEM_TOKENS:(h + 1) * MEM_TOKENS]
        e = jnp.exp(sh - jnp.max(sh, axis=1, keepdims=True))
        probs.append((e * (1.0 / jnp.sum(e, axis=1, keepdims=True))).astype(BF16))
    mem_out = _dot(jnp.concatenate(probs, axis=1), mvbd_ref[0, 0])

    y = _dot(mem_out.astype(BF16), wout_mem_ref[...])
    y = y + jnp.concatenate([_dot_tn(mix_ref[0, i], wout_mix_ref[...]) for i in range(mix_ref.shape[1])], axis=0)
    x1 = x_ref[0] + y

    hn = _rms(x1, ffn_g_ref[...]).astype(BF16)
    n_chunks = D_FF // FF_CHUNK
    chunk_cols = [slice(c * FF_CHUNK, (c + 1) * FF_CHUNK) for c in range(n_chunks)]

    def up(c):
        return _dot(hn, wu_ref[:, chunk_cols[c]]), _dot(hn, wg_ref[:, chunk_cols[c]])

    u, g = up(0)
    for c in range(n_chunks):
        cols = chunk_cols[c]
        nxt = up(c + 1) if c + 1 < n_chunks else None
        prev = halo_ref[:, cols]
        halo_ref[:, cols] = g[tm - HALO:, :]
        gc = _causal_taps(g, prev, cw_ref[:, cols]) + cb_ref[:, cols]
        act_ref[:, cols] = (_silu(gc) * u).astype(BF16)
        if nxt is not None:
            u, g = nxt
    x2 = x1 + _dot(act_ref[...], wd_ref[...])

    if last:
        out_ref[0] = _rms(x2, final_g_ref[...])
    else:
        out_ref[0] = x2
        hkv = _rms(x2, kv_g_ref[...]).astype(BF16)
        k_ref[0] = _dot(hkv, wk_ref[...]).astype(BF16)
        vt_ref[0] = _dot_nt(wvt_ref[...], hkv).astype(BF16)
        hq = _rms(x2, nxt_g_ref[...]).astype(BF16)
        qnt = (_dot_nt(wqt_ref[...], hq) * (B_HEAD_DIM ** -0.5)).astype(BF16)
        for i in range(qnt_ref.shape[1]):
            qnt_ref[0, i] = qnt[:, i * TQ_BAND:(i + 1) * TQ_BAND]
        qmemn_ref[0] = _dot(hq, wqmem_ref[...]).astype(BF16)


def _post_mixer(x, mix, qmem, mkbdt, mvbd, layer, wout_mix, wout_mem, ffn_g, wu, wg, cw, cb, wd, tail, last):
    bsz, seq, _ = x.shape
    tm = TM_POST
    row_spec = lambda width: pl.BlockSpec((1, tm, width), lambda b, j: (b, j, 0))
    col_spec = lambda width: pl.BlockSpec((1, width, tm), lambda b, j: (b, 0, j))
    tile_spec = lambda width, tile: pl.BlockSpec((1, tm // tile, width, tile), lambda b, j: (b, j, 0, 0))
    mix_spec = tile_spec(mix.shape[2], mix.shape[3])
    weights = (wout_mix, wout_mem, ffn_g, wu, wg, cw, cb, wd) + tuple(tail)
    in_specs = [
        row_spec(D_MODEL), mix_spec, row_spec(MEM_WIDTH),
        pl.BlockSpec((1, 1) + mkbdt.shape[2:], lambda b, j: (layer, b, 0, 0)),
        pl.BlockSpec((1, 1) + mvbd.shape[2:], lambda b, j: (layer, b, 0, 0)),
    ] + [_resident(w.shape) for w in weights]
    if last:
        out_specs = row_spec(D_MODEL)
        out_shape = jax.ShapeDtypeStruct((bsz, seq, D_MODEL), F32)
    else:
        out_specs = [row_spec(D_MODEL), row_spec(B_WIDTH), col_spec(B_WIDTH), tile_spec(B_WIDTH, TQ_BAND),
                     row_spec(MEM_WIDTH)]
        out_shape = [
            jax.ShapeDtypeStruct((bsz, seq, D_MODEL), F32),
            jax.ShapeDtypeStruct((bsz, seq, B_WIDTH), BF16),
            jax.ShapeDtypeStruct((bsz, B_WIDTH, seq), BF16),
            jax.ShapeDtypeStruct((bsz, seq // TQ_BAND, B_WIDTH, TQ_BAND), BF16),
            jax.ShapeDtypeStruct((bsz, seq, MEM_WIDTH), BF16),
        ]
    return pl.pallas_call(
        functools.partial(_post_mixer_kernel, last=last),
        grid=(bsz, seq // tm),
        in_specs=in_specs,
        out_specs=out_specs,
        out_shape=out_shape,
        scratch_shapes=[pltpu.VMEM((HALO, D_FF), F32), pltpu.VMEM((tm, D_FF), BF16)],
        compiler_params=pltpu.CompilerParams(
            dimension_semantics=("arbitrary", "arbitrary"), vmem_limit_bytes=VMEM_LIMIT),
        name="post_mixer_last" if last else "post_mixer",
    )(x, mix, qmem, mkbdt, mvbd, *weights)


def _band_bias_kernel(tbl_ref, out_ref):
    h = pl.program_id(0)
    b = lax.broadcasted_iota(jnp.int32, (8, BIAS_LANES), 0)
    c = lax.broadcasted_iota(jnp.int32, (8, BIAS_LANES), 1)
    r_signed = jnp.where(c < TQ_HALF, c, c - BIAS_LANES)
    idx = jnp.clip(BAND_PREV + r_signed - b, -(CHUNK - 1), MAX_REL) + (CHUNK - 1)

    def body(e, acc):
        return jnp.where(idx == e, tbl_ref[h, e], acc)

    base = lax.fori_loop(0, REL_SIZE, body, jnp.zeros((8, BIAS_LANES), F32), unroll=8)
    kb = lax.broadcasted_iota(jnp.int32, (8, TQ_HALF), 0)
    r = lax.broadcasted_iota(jnp.int32, (8, TQ_HALF), 1)
    for a in range(BIAS_ROWS // 8):
        first = (-8 * a) % BIAS_LANES
        lo = first // TQ_HALF * TQ_HALF
        hi = (lo + TQ_HALF) % BIAS_LANES
        window = jnp.concatenate([base[:, lo:lo + TQ_HALF], base[:, hi:hi + TQ_HALF]], axis=1)
        if first != lo:
            window = pltpu.roll(window, 2 * TQ_HALF - (first - lo), axis=1)
        dchunk = (BAND_PREV + r) // CHUNK - (8 * a + kb) // CHUNK
        valid = (dchunk >= 0) & (dchunk < BAND_CHUNKS)
        out_ref[0, 8 * a:8 * a + 8, :] = jnp.where(valid, window[:, :TQ_HALF], NEG_BIG)


def _band_bias(rel_table):
    return pl.pallas_call(
        _band_bias_kernel,
        grid=(B_HEADS,),
        in_specs=[pl.BlockSpec(memory_space=pltpu.SMEM)],
        out_specs=pl.BlockSpec((1, BIAS_ROWS, TQ_HALF), lambda h: (h, 0, 0)),
        out_shape=jax.ShapeDtypeStruct((B_HEADS, BIAS_ROWS, TQ_HALF), F32),
        compiler_params=pltpu.CompilerParams(dimension_semantics=("arbitrary",)),
        name="band_bias",
    )(rel_table)


def _band_attn_kernel(qt_ref, k_ref, vt_ref, bias_ref, out_ref):
    j = pl.program_id(1)
    first_half = 2 * j - HALVES_BACK
    start = pl.multiple_of(jnp.maximum(first_half, 0) * TQ_HALF, TQ_HALF)
    pair_width = 2 * B_HEAD_DIM
    pair_row = lax.broadcasted_iota(jnp.int32, (pair_width, TQ_BAND), 0)
    zeros_half = jnp.zeros((TQ_HALF, TQ_HALF), BF16)
    masked_half = jnp.full((TQ_HALF, TQ_HALF), NEG_BIG, F32)

    def scores(pair):
        lo = pair * pair_width
        k_pair = k_ref[0, pl.ds(start, TK_BAND), lo:lo + pair_width]
        q_pair = qt_ref[0, 0, lo:lo + pair_width, :]
        zero = jnp.zeros_like(q_pair)
        q_sel = jnp.concatenate([jnp.where(pair_row < B_HEAD_DIM, q_pair, zero),
                                 jnp.where(pair_row >= B_HEAD_DIM, q_pair, zero)], axis=1)
        return _dot(k_pair, q_sel)

    def softmax_half(t, row0, c0):
        blocks = [t[row0 + r:row0 + r + SOFTMAX_ROWS, c0:c0 + TQ_HALF] for r in range(0, TK_HALF, SOFTMAX_ROWS)]
        m_acc = blocks[0]
        for blk in blocks[1:]:
            m_acc = jnp.maximum(m_acc, blk)
        m = jnp.max(m_acc, axis=0, keepdims=True)
        ps = []
        l_acc = None
        for blk in blocks:
            e = jnp.exp(blk - m)
            ps.append(e.astype(BF16))
            l_acc = e if l_acc is None else l_acc + e
        return jnp.concatenate(ps, axis=0), jnp.sum(l_acc, axis=0, keepdims=True)

    def tile(off_b, bias_row_a, bias_row_b):
        def bias_tile(pair):
            cols = []
            for h in (2 * pair, 2 * pair + 1):
                bias_a = bias_ref[h, pl.ds(bias_row_a, TK_HALF), :]
                bias_b = bias_ref[h, pl.ds(bias_row_b, TK_HALF), :]
                cols.append(jnp.concatenate([bias_a, masked_half], axis=0))
                cols.append(jnp.concatenate([masked_half, bias_b] if off_b else [bias_b, masked_half], axis=0))
            return jnp.concatenate(cols, axis=1)

        n_pairs = B_HEADS // 2
        live = [dict() for _ in range(n_pairs)]

        def biased_scores(pair):
            live[pair]["t"] = scores(pair) + bias_tile(pair)

        def softmax(pair):
            t = live[pair].pop("t")
            cols, sums = [], []
            for parity in range(2):
                c0 = parity * TQ_BAND
                p_a, l_a = softmax_half(t, 0, c0)
                p_b, l_b = softmax_half(t, off_b, c0 + TQ_HALF)
                cols.append(jnp.concatenate([p_a, zeros_half], axis=0))
                cols.append(jnp.concatenate([zeros_half, p_b] if off_b else [p_b, zeros_half], axis=0))
                sums += [l_a, l_b]
            live[pair]["p"] = jnp.concatenate(cols, axis=1)
            live[pair]["inv"] = 1.0 / jnp.concatenate(sums, axis=1)

        def values(pair):
            lo = pair * pair_width
            vt = vt_ref[0, lo:lo + pair_width, pl.ds(start, TK_BAND)]
            o = _dot(vt, live[pair].pop("p"))
            inv = live[pair].pop("inv")
            out_ref[0, 0, lo:lo + B_HEAD_DIM, :] = (o[:B_HEAD_DIM, :TQ_BAND] * inv[:, :TQ_BAND]).astype(BF16)
            out_ref[0, 0, lo + B_HEAD_DIM:lo + pair_width, :] = (
                o[B_HEAD_DIM:, TQ_BAND:] * inv[:, TQ_BAND:]).astype(BF16)

        stages = (biased_scores, softmax, values)
        for slot in range(n_pairs + len(stages) - 1):
            for depth in (0, 2, 1):
                if 0 <= slot - depth < n_pairs:
                    stages[depth](slot - depth)

    pl.when(first_half >= 0)(functools.partial(tile, TQ_HALF, 0, 0))

    @pl.when(first_half < 0)
    def _():
        row_a = pl.multiple_of(-first_half * TQ_HALF, TQ_HALF)
        tile(0, row_a, pl.multiple_of(row_a - TQ_HALF, TQ_HALF))


def _band_attn(qt, k, vt, bias):
    bsz, seq, _ = k.shape
    return pl.pallas_call(
        _band_attn_kernel,
        grid=(bsz, seq // TQ_BAND),
        in_specs=[
            pl.BlockSpec((1, 1, B_WIDTH, TQ_BAND), lambda b, j: (b, j, 0, 0)),
            pl.BlockSpec((1, seq, B_WIDTH), lambda b, j: (b, 0, 0)),
            pl.BlockSpec((1, B_WIDTH, seq), lambda b, j: (b, 0, 0)),
            _resident(bias.shape),
        ],
        out_specs=pl.BlockSpec((1, 1, B_WIDTH, TQ_BAND), lambda b, j: (b, j, 0, 0)),
        out_shape=jax.ShapeDtypeStruct((bsz, seq // TQ_BAND, B_WIDTH, TQ_BAND), BF16),
        compiler_params=pltpu.CompilerParams(
            dimension_semantics=("arbitrary", "arbitrary"), vmem_limit_bytes=VMEM_LIMIT),
        name="band_attn",
    )(qt, k, vt, bias)


def kernel(x, mem, norm_mix_g, norm_ffn_g, a_w_in, a_gate_b, a_conv_w, a_conv_b, a_head_g, a_w_out, kv_norm_g, w_kv,
           b_w_in, b_rel_bias, b_w_out, mem_w_kv, ffn_w_up, ffn_conv_w, ffn_conv_b, ffn_w_down, final_g):
    bsz, seq, d = x.shape
    assert d == D_MODEL and seq % max(TM_IN, L_MLSTM, TM_POST, TQ_BAND) == 0
    assert a_w_in.shape[0] == 1 and b_w_in.shape[0] == 1, "one mLSTM layer followed by one band-attention layer"
    assert TM_IN % L_MLSTM == 0, "the input projection emits per-chunk decay terms: whole chunks per tile"
    assert bsz % MLSTM_BATCH == 0
    row = lambda g: g.reshape(1, -1).astype(F32)

    wkt = jnp.swapaxes(mem_w_kv[:, :, :MEM_WIDTH], 1, 2).astype(BF16)
    wmv = mem_w_kv[:, :, MEM_WIDTH:].astype(BF16)
    mkbdt, mvbd = _mem_kv(mem.astype(BF16), wkt, wmv)

    w = a_w_in[0]
    wqk = w[:, 0:2 * A_WIDTH].astype(BF16)
    wvt = jnp.swapaxes(w[:, 2 * A_WIDTH:3 * A_WIDTH], 0, 1).reshape(A_HEADS, A_HEAD_DIM, D_MODEL)
    wvt = jnp.pad(wvt, ((0, 0), (0, V_AUG - A_HEAD_DIM), (0, 0))).reshape(A_HEADS * V_AUG, D_MODEL).astype(BF16)
    wot = jnp.swapaxes(w[:, 3 * A_WIDTH:4 * A_WIDTH], 0, 1).astype(BF16)
    gate_pad = ((0, 0), (0, GATE_LANES - A_HEADS))
    g0 = 4 * A_WIDTH
    wgm = jnp.concatenate([jnp.pad(w[:, g0:g0 + A_HEADS], gate_pad), jnp.pad(w[:, g0 + A_HEADS:g0 + 2 * A_HEADS], gate_pad),
                           w[:, g0 + 2 * A_HEADS:]], axis=1).astype(BF16)
    gate_b = jnp.pad(a_gate_b[0].astype(F32).reshape(2, A_HEADS), gate_pad).reshape(1, 2 * GATE_LANES)
    vone = (jnp.arange(A_HEADS * V_AUG) % V_AUG == A_HEAD_DIM).astype(F32).reshape(A_HEADS * V_AUG, 1)
    q, k, vt, ot_half, gate_rows, a_cols, decay, qmem = _a_in(
        x, row(norm_mix_g[0]), wqk, wvt, wot, wgm, gate_b, vone, a_conv_w[0].astype(F32), row(a_conv_b[0]))

    head_g_half = 0.5 * a_head_g[0].reshape(A_HEADS, A_HEAD_DIM, 1).astype(F32)
    mix = _mlstm(q, k, vt, ot_half, gate_rows, a_cols, decay, head_g_half)

    def ffn_weights(l):
        return (row(norm_ffn_g[l]), ffn_w_up[l][:, :D_FF].astype(BF16), ffn_w_up[l][:, D_FF:].astype(BF16),
                ffn_conv_w[l].astype(F32), row(ffn_conv_b[l]), ffn_w_down[l].astype(BF16))

    wout = a_w_out[0]
    tail = (row(kv_norm_g), w_kv[:, :B_WIDTH].astype(BF16), jnp.swapaxes(w_kv[:, B_WIDTH:], 0, 1).astype(BF16),
            row(norm_mix_g[1]), jnp.swapaxes(b_w_in[0][:, :B_WIDTH], 0, 1).astype(BF16),
            b_w_in[0][:, B_WIDTH:].astype(BF16))
    x1, kb, vbt, qbt, qmem_b = _post_mixer(
        x, mix, qmem, mkbdt, mvbd, 0,
        wout[:A_WIDTH].astype(BF16), wout[A_WIDTH:].astype(BF16),
        *ffn_weights(0), tail=tail, last=False)

    bias = _band_bias(b_rel_bias[0].astype(F32))
    mix_b = _band_attn(qbt, kb, vbt, bias)
    wout = b_w_out[0]
    return _post_mixer(
        x1, mix_b, qmem_b, mkbdt, mvbd, 1,
        wout[:B_WIDTH].astype(BF16), wout[B_WIDTH:].astype(BF16),
        *ffn_weights(1), tail=(row(final_g),), last=True)
```

```python
import functools
import math

import jax
import jax.numpy as jnp
from jax import lax
from jax.experimental import pallas as pl
from jax.experimental.pallas import tpu as pltpu

F32 = jnp.float32
BF16 = jnp.bfloat16

D_MODEL = 1024
CHUNK = 64
MEM_TOKENS = 256
MEM_HEADS = 4
MEM_WIDTH = 256
MEM_HEAD_DIM = 64
A_WIDTH = 768
A_HEADS = 4
A_HEAD_DIM = 192
A_CONV = 4
B_HEADS = 12
B_HEAD_DIM = 64
B_WIDTH = 768
BAND_CHUNKS = 9
MAX_REL = 128
REL_SIZE = MAX_REL + CHUNK
D_FF = 2816
FFN_CONV = 3
EPS = 1e-6

V_AUG = 208
GATE_LANES = 128
HALO = 8
NEG_BIG = -1e30

TM_IN = 512
L_MLSTM = 256
MLSTM_BATCH = 4
TM_POST = 512
FF_CHUNK = 256
TQ_HALF = 128
TQ_BAND = 2 * TQ_HALF
BAND_PREV = (BAND_CHUNKS - 1) * CHUNK
TK_HALF = BAND_PREV + TQ_HALF
TK_BAND = BAND_PREV + TQ_BAND
HALVES_BACK = BAND_PREV // TQ_HALF
BIAS_ROWS = TK_HALF + BAND_PREV
BIAS_LANES = BIAS_ROWS + TQ_HALF
SOFTMAX_ROWS = 64

VMEM_LIMIT = 56 * 1024 * 1024


def _resident(shape):
    nd = len(shape)
    return pl.BlockSpec(shape, lambda *_: (0,) * nd, pipeline_mode=pl.Buffered(1))


def _rms(x, g):
    return x * lax.rsqrt(jnp.mean(x * x, axis=-1, keepdims=True) + EPS) * g


def _dot(a, b):
    return jnp.dot(a, b, preferred_element_type=F32)


def _dot_nt(a, b):
    return lax.dot_general(a, b, (((1,), (1,)), ((), ())), preferred_element_type=F32)


def _dot_tn(a, b):
    return lax.dot_general(a, b, (((0,), (0,)), ((), ())), preferred_element_type=F32)


def _sigmoid(x):
    return 0.5 + 0.5 * jnp.tanh(0.5 * x)


def _silu(x):
    half = 0.5 * x
    return half + half * jnp.tanh(half)


def _mem_kv_kernel(mem_ref, wkt_ref, wv_ref, mkbdt_ref, mvbd_ref):
    m = mem_ref[0]
    mkt = _dot_nt(wkt_ref[0], m)
    mv = _dot(m, wv_ref[0])
    f_idx = lax.broadcasted_iota(jnp.int32, (MEM_WIDTH, MEM_HEADS * MEM_TOKENS), 0)
    c_idx = lax.broadcasted_iota(jnp.int32, (MEM_WIDTH, MEM_HEADS * MEM_TOKENS), 1)
    mkt4 = jnp.concatenate([mkt] * MEM_HEADS, axis=1)
    mkbdt_ref[0, 0] = jnp.where(f_idx // MEM_HEAD_DIM == c_idx // MEM_TOKENS, mkt4, 0.0).astype(BF16)
    r_idx = lax.broadcasted_iota(jnp.int32, (MEM_HEADS * MEM_TOKENS, MEM_WIDTH), 0)
    g_idx = lax.broadcasted_iota(jnp.int32, (MEM_HEADS * MEM_TOKENS, MEM_WIDTH), 1)
    mv4 = jnp.concatenate([mv] * MEM_HEADS, axis=0)
    mvbd_ref[0, 0] = jnp.where(r_idx // MEM_TOKENS == g_idx // MEM_HEAD_DIM, mv4, 0.0).astype(BF16)


def _mem_kv(mem_bf, wkt, wv):
    depth = wkt.shape[0]
    bsz = mem_bf.shape[0]
    return pl.pallas_call(
        _mem_kv_kernel,
        grid=(depth, bsz),
        in_specs=[
            pl.BlockSpec((1, MEM_TOKENS, D_MODEL), lambda l, b: (b, 0, 0)),
            pl.BlockSpec((1, MEM_WIDTH, D_MODEL), lambda l, b: (l, 0, 0)),
            pl.BlockSpec((1, D_MODEL, MEM_WIDTH), lambda l, b: (l, 0, 0)),
        ],
        out_specs=[
            pl.BlockSpec((1, 1, MEM_WIDTH, MEM_HEADS * MEM_TOKENS), lambda l, b: (l, b, 0, 0)),
            pl.BlockSpec((1, 1, MEM_HEADS * MEM_TOKENS, MEM_WIDTH), lambda l, b: (l, b, 0, 0)),
        ],
        out_shape=[
            jax.ShapeDtypeStruct((depth, bsz, MEM_WIDTH, MEM_HEADS * MEM_TOKENS), BF16),
            jax.ShapeDtypeStruct((depth, bsz, MEM_HEADS * MEM_TOKENS, MEM_WIDTH), BF16),
        ],
        compiler_params=pltpu.CompilerParams(dimension_semantics=("arbitrary", "arbitrary")),
        name="mem_kv",
    )(mem_bf, wkt, wv)


def _causal_taps(x, prev, taps):
    n_taps = taps.shape[0]
    halo_row = lax.broadcasted_iota(jnp.int32, (HALO, x.shape[1]), 0)
    y = taps[n_taps - 1:n_taps, :] * x
    for back in range(1, n_taps):
        head = jnp.where(halo_row < back, pltpu.roll(prev, back, axis=0), pltpu.roll(x[:HALO, :], back, axis=0))
        shifted = jnp.concatenate([head, pltpu.roll(x, back, axis=0)[HALO:, :]], axis=0)
        y = y + taps[n_taps - 1 - back:n_taps - back, :] * shifted
    return y


def _lane_scan(x, op, identity):
    lane = lax.broadcasted_iota(jnp.int32, x.shape, 1)
    step = 1
    while step < x.shape[1]:
        x = op(x, jnp.where(lane >= step, pltpu.roll(x, step, axis=1), identity))
        step *= 2
    return x


def _a_in_kernel(x_ref, g_ref, wqk_ref, wvt_ref, wot_ref, wgm_ref, gate_b_ref, vone_ref, cw_ref, cb_ref,
                 q_ref, k_ref, vt_ref, ot_ref, rows_ref, acol_ref, decay_ref, qmem_ref, halo_ref, m_ref):
    L = L_MLSTM
    n_chunks = x_ref.shape[1] // L

    @pl.when(pl.program_id(1) == 0)
    def _():
        halo_ref[...] = jnp.zeros(halo_ref.shape, F32)
        m_ref[...] = jnp.zeros(m_ref.shape, F32)

    def projections(c):
        tok = slice(c * L, (c + 1) * L)
        hn = _rms(x_ref[0, tok, :], g_ref[...]).astype(BF16)

        gm = _dot(hn, wgm_ref[...])
        qmem_ref[0, tok, :] = gm[:, 2 * GATE_LANES:].astype(BF16)
        gi = (gm[:, :GATE_LANES] + gate_b_ref[:, :GATE_LANES]).T[0:8, :]
        fg = (gm[:, GATE_LANES:2 * GATE_LANES] + gate_b_ref[:, GATE_LANES:]).T[0:8, :]
        head_row = lax.broadcasted_iota(jnp.int32, (8, L), 0) < A_HEADS
        logf = jnp.where(head_row, jnp.minimum(fg, 0.0) - jnp.log(1.0 + jnp.exp(-jnp.abs(fg))), 0.0)
        b = _lane_scan(logf, jnp.add, 0.0)
        a = gi - b
        m_prev = m_ref[:, 0:1]
        g = jnp.maximum(_lane_scan(a, jnp.maximum, -jnp.inf), m_prev)
        g_last = g[:, L - 1:L]
        m_ref[...] = jnp.broadcast_to(b[:, L - 1:L] + g_last, m_ref.shape)
        decay_ref[0, c] = jnp.broadcast_to(jnp.exp(m_prev - g_last), decay_ref.shape[2:])
        rows_ref[0, :, tok] = jnp.concatenate(
            [g, jnp.exp(m_prev - g), jnp.exp(-(b + g)), jnp.exp(a - g_last)], axis=0)
        acol_ref[0, tok, :] = jnp.concatenate([a, jnp.zeros((GATE_LANES - 8, L), F32)], axis=0).T

        raw = _dot(hn, wqk_ref[...])
        vt_ref[0, c] = (_dot_nt(wvt_ref[...], hn) + vone_ref[...]).astype(BF16)
        ot_ref[0, c] = _dot_nt(wot_ref[...], hn)
        return raw

    def conv(c, raw):
        tok = slice(c * L, (c + 1) * L)
        prev = halo_ref[...]
        halo_ref[...] = raw[L - HALO:, :]
        qk = _silu(_causal_taps(raw, prev, cw_ref[...]) + cb_ref[...])
        for h in range(A_HEADS):
            lo = h * A_HEAD_DIM
            q_ref[0, h, tok, :] = qk[:, lo:lo + A_HEAD_DIM].astype(BF16)
            k_ref[0, h, tok, :] = (qk[:, A_WIDTH + lo:A_WIDTH + lo + A_HEAD_DIM]
                                   * (A_HEAD_DIM ** -0.5)).astype(BF16)

    raws = {}
    for slot in range(n_chunks + 1):
        if slot < n_chunks:
            raws[slot] = projections(slot)
        if slot >= 1:
            conv(slot - 1, raws.pop(slot - 1))


def _a_in(x, g, wqk, wvt, wot, wgm, gate_b, vone, cw, cb):
    bsz, seq, _ = x.shape
    tm = TM_IN
    head_spec = lambda width: pl.BlockSpec((1, A_HEADS, tm, width), lambda b, j: (b, 0, j, 0))
    row_spec = lambda width: pl.BlockSpec((1, tm, width), lambda b, j: (b, j, 0))
    col_spec = lambda width: pl.BlockSpec((1, width, tm), lambda b, j: (b, 0, j))
    tile_spec = lambda width: pl.BlockSpec((1, tm // L_MLSTM, width, L_MLSTM), lambda b, j: (b, j, 0, 0))
    weights = (g, wqk, wvt, wot, wgm, gate_b, vone, cw, cb)
    return pl.pallas_call(
        _a_in_kernel,
        grid=(bsz, seq // tm),
        in_specs=[row_spec(D_MODEL)] + [_resident(w.shape) for w in weights],
        out_specs=[
            head_spec(A_HEAD_DIM), head_spec(A_HEAD_DIM), tile_spec(A_HEADS * V_AUG), tile_spec(A_WIDTH),
            col_spec(4 * 8),
            row_spec(GATE_LANES),
            pl.BlockSpec((1, tm // L_MLSTM, 8, GATE_LANES), lambda b, j: (b, j, 0, 0)),
            row_spec(MEM_WIDTH),
        ],
        out_shape=[
            jax.ShapeDtypeStruct((bsz, A_HEADS, seq, A_HEAD_DIM), BF16),
            jax.ShapeDtypeStruct((bsz, A_HEADS, seq, A_HEAD_DIM), BF16),
            jax.ShapeDtypeStruct((bsz, seq // L_MLSTM, A_HEADS * V_AUG, L_MLSTM), BF16),
            jax.ShapeDtypeStruct((bsz, seq // L_MLSTM, A_WIDTH, L_MLSTM), F32),
            jax.ShapeDtypeStruct((bsz, 4 * 8, seq), F32),
            jax.ShapeDtypeStruct((bsz, seq, GATE_LANES), F32),
            jax.ShapeDtypeStruct((bsz, seq // L_MLSTM, 8, GATE_LANES), F32),
            jax.ShapeDtypeStruct((bsz, seq, MEM_WIDTH), BF16),
        ],
        scratch_shapes=[pltpu.VMEM((HALO, 2 * A_WIDTH), F32), pltpu.VMEM((8, GATE_LANES), F32)],
        compiler_params=pltpu.CompilerParams(
            dimension_semantics=("arbitrary", "arbitrary"), vmem_limit_bytes=VMEM_LIMIT),
        name="a_in",
    )(x, *weights)


def _mlstm_kernel(q_ref, k_ref, vt_ref, ot_ref, rows_ref, acol_ref, decay_ref, hg_ref, out_ref, ct_ref):
    L = q_ref.shape[2]
    half = L // 2

    @pl.when(pl.program_id(1) == 0)
    def _():
        ct_ref[...] = jnp.zeros(ct_ref.shape, F32)

    upper = (lax.broadcasted_iota(jnp.int32, (half, half), 0)
             <= lax.broadcasted_iota(jnp.int32, (half, half), 1))
    zero_block = jnp.zeros((half, half), BF16)
    items = [(n, h) for n in range(q_ref.shape[0]) for h in range(A_HEADS)]
    live = [dict() for _ in items]

    def scores(i):
        n, h = items[i]
        live[i]["st"] = _dot_nt(k_ref[n, h], q_ref[n, h])

    def decay_weight(i):
        n, h = items[i]
        st = live[i].pop("st")
        a_col = acol_ref[n, :, h:h + 1]
        g_row = rows_ref[n, h:h + 1, :]
        p00 = jnp.exp(jnp.where(upper, a_col[:half] - g_row[:, :half], -jnp.inf))
        p01 = jnp.exp(a_col[:half] - g_row[:, half:])
        p11 = jnp.exp(jnp.where(upper, a_col[half:] - g_row[:, half:], -jnp.inf))
        live[i]["s"] = jnp.concatenate([
            jnp.concatenate([(st[:half, :half] * p00).astype(BF16), (st[:half, half:] * p01).astype(BF16)], axis=1),
            jnp.concatenate([zero_block, (st[half:, half:] * p11).astype(BF16)], axis=1)], axis=0)

    def numerator(i):
        n, h = items[i]
        vt = vt_ref[n, 0, h * V_AUG:(h + 1) * V_AUG, :]
        ct_prev = ct_ref[n, h]
        live[i]["num"] = (_dot(vt, live[i].pop("s"))
                          + _dot_nt(ct_prev.astype(BF16), q_ref[n, h]) * rows_ref[n, 8 + h:9 + h, :])
        wvt = (vt.astype(F32) * rows_ref[n, 24 + h:25 + h, :]).astype(BF16)
        ct_ref[n, h] = decay_ref[n, 0, h:h + 1, 0:1] * ct_prev + _dot(wvt, k_ref[n, h])

    def normalise(i):
        n, h = items[i]
        numt = live[i].pop("num")
        den = numt[A_HEAD_DIM:A_HEAD_DIM + 1, :]
        r = 1.0 / jnp.maximum(jnp.abs(den), rows_ref[n, 16 + h:17 + h, :])
        body = numt[:A_HEAD_DIM, :]
        ssq = jnp.sum(body * body, axis=0, keepdims=True)
        factor = r * lax.rsqrt(r * r * ssq * (1.0 / A_HEAD_DIM) + EPS)
        lo = h * A_HEAD_DIM
        gate = hg_ref[h] + hg_ref[h] * jnp.tanh(ot_ref[n, 0, lo:lo + A_HEAD_DIM, :])
        out_ref[n, 0, lo:lo + A_HEAD_DIM, :] = (body * factor * gate).astype(BF16)

    stages = (scores, decay_weight, numerator, normalise)
    for slot in range(len(items) + len(stages) - 1):
        for depth, stage in enumerate(stages):
            if 0 <= slot - depth < len(items):
                stage(slot - depth)


def _mlstm(q, k, vt, ot_half, rows, a_cols, decay, head_g_half):
    bsz, nh, seq, dh = q.shape
    L = L_MLSTM
    nb = MLSTM_BATCH
    head_spec = pl.BlockSpec((nb, nh, L, dh), lambda b, j: (b, 0, j, 0))
    col_spec = lambda width: pl.BlockSpec((nb, width, L), lambda b, j: (b, 0, j))
    tile_spec = lambda width: pl.BlockSpec((nb, 1, width, L), lambda b, j: (b, j, 0, 0))
    return pl.pallas_call(
        _mlstm_kernel,
        grid=(bsz // nb, seq // L),
        in_specs=[
            head_spec, head_spec, tile_spec(nh * V_AUG), tile_spec(nh * dh), col_spec(rows.shape[1]),
            pl.BlockSpec((nb, L, GATE_LANES), lambda b, j: (b, j, 0)),
            pl.BlockSpec((nb, 1, 8, GATE_LANES), lambda b, j: (b, j, 0, 0)),
            _resident(head_g_half.shape),
        ],
        out_specs=tile_spec(nh * dh),
        out_shape=jax.ShapeDtypeStruct((bsz, seq // L, nh * dh, L), BF16),
        scratch_shapes=[pltpu.VMEM((nb, nh, V_AUG, dh), F32)],
        compiler_params=pltpu.CompilerParams(
            dimension_semantics=("arbitrary", "arbitrary"), vmem_limit_bytes=VMEM_LIMIT),
        name="mlstm",
    )(q, k, vt, ot_half, rows, a_cols, decay, head_g_half)


def _post_mixer_kernel(*refs, last):
    if last:
        (x_ref, mix_ref, qmem_ref, mkbdt_ref, mvbd_ref, wout_mix_ref, wout_mem_ref, ffn_g_ref, wu_ref, wg_ref,
         cw_ref, cb_ref, wd_ref, final_g_ref, out_ref, halo_ref, act_ref) = refs
    else:
        (x_ref, mix_ref, qmem_ref, mkbdt_ref, mvbd_ref, wout_mix_ref, wout_mem_ref, ffn_g_ref, wu_ref, wg_ref,
         cw_ref, cb_ref, wd_ref, kv_g_ref, wk_ref, wvt_ref, nxt_g_ref, wqt_ref, wqmem_ref,
         out_ref, k_ref, vt_ref, qnt_ref, qmemn_ref, halo_ref, act_ref) = refs
    tm = x_ref.shape[1]

    @pl.when(pl.program_id(1) == 0)
    def _():
        halo_ref[...] = jnp.zeros((HALO, D_FF), F32)

    s = _dot(qmem_ref[0], mkbdt_ref[0, 0])
    probs = []
    for h in range(MEM_HEADS):
        sh = s[:, h * MEM_TOKENS:(h + 1) * MEM_TOKENS]
        e = jnp.exp(sh - jnp.max(sh, axis=1, keepdims=True))
        probs.append((e * (1.0 / jnp.sum(e, axis=1, keepdims=True))).astype(BF16))
    mem_out = _dot(jnp.concatenate(probs, axis=1), mvbd_ref[0, 0])

    y = _dot(mem_out.astype(BF16), wout_mem_ref[...])
    y = y + jnp.concatenate([_dot_tn(mix_ref[0, i], wout_mix_ref[...]) for i in range(mix_ref.shape[1])], axis=0)
    x1 = x_ref[0] + y

    hn = _rms(x1, ffn_g_ref[...]).astype(BF16)
    chunk_cols = [slice(c0, min(c0 + FF_CHUNK, D_FF)) for c0 in range(0, D_FF, FF_CHUNK)]
    n_chunks = len(chunk_cols)

    def up(c):
        return _dot(hn, wu_ref[:, chunk_cols[c]]), _dot(hn, wg_ref[:, chunk_cols[c]])

    u, g = up(0)
    for c in range(n_chunks):
        cols = chunk_cols[c]
        nxt = up(c + 1) if c + 1 < n_chunks else None
        prev = halo_ref[:, cols]
        halo_ref[:, cols] = g[tm - HALO:, :]
        gc = _causal_taps(g, prev, cw_ref[:, cols]) + cb_ref[:, cols]
        act_ref[:, cols] = (_silu(gc) * u).astype(BF16)
        if nxt is not None:
            u, g = nxt
    x2 = x1 + _dot(act_ref[...], wd_ref[...])

    if last:
        out_ref[0] = _rms(x2, final_g_ref[...])
    else:
        out_ref[0] = x2
        hkv = _rms(x2, kv_g_ref[...]).astype(BF16)
        k_ref[0] = _dot(hkv, wk_ref[...]).astype(BF16)
        vt_ref[0] = _dot_nt(wvt_ref[...], hkv).astype(BF16)
        hq = _rms(x2, nxt_g_ref[...]).astype(BF16)
        qnt = _dot_nt(wqt_ref[...], hq).astype(BF16)
        for i in range(qnt_ref.shape[1]):
            qnt_ref[0, i] = qnt[:, i * TQ_BAND:(i + 1) * TQ_BAND]
        qmemn_ref[0] = _dot(hq, wqmem_ref[...]).astype(BF16)


def _post_mixer(x, mix, qmem, mkbdt, mvbd, layer, wout_mix, wout_mem, ffn_g, wu, wg, cw, cb, wd, tail, last):
    bsz, seq, _ = x.shape
    tm = TM_POST
    row_spec = lambda width: pl.BlockSpec((1, tm, width), lambda b, j: (b, j, 0))
    col_spec = lambda width: pl.BlockSpec((1, width, tm), lambda b, j: (b, 0, j))
    tile_spec = lambda width, tile: pl.BlockSpec((1, tm // tile, width, tile), lambda b, j: (b, j, 0, 0))
    mix_spec = tile_spec(mix.shape[2], mix.shape[3])
    weights = (wout_mix, wout_mem, ffn_g, wu, wg, cw, cb, wd) + tuple(tail)
    in_specs = [
        row_spec(D_MODEL), mix_spec, row_spec(MEM_WIDTH),
        pl.BlockSpec((1, 1) + mkbdt.shape[2:], lambda b, j: (layer, b, 0, 0)),
        pl.BlockSpec((1, 1) + mvbd.shape[2:], lambda b, j: (layer, b, 0, 0)),
    ] + [_resident(w.shape) for w in weights]
    if last:
        out_specs = row_spec(D_MODEL)
        out_shape = jax.ShapeDtypeStruct((bsz, seq, D_MODEL), F32)
    else:
        out_specs = [row_spec(D_MODEL), row_spec(B_WIDTH), col_spec(B_WIDTH), tile_spec(B_WIDTH, TQ_BAND),
                     row_spec(MEM_WIDTH)]
        out_shape = [
            jax.ShapeDtypeStruct((bsz, seq, D_MODEL), F32),
            jax.ShapeDtypeStruct((bsz, seq, B_WIDTH), BF16),
            jax.ShapeDtypeStruct((bsz, B_WIDTH, seq), BF16),
            jax.ShapeDtypeStruct((bsz, seq // TQ_BAND, B_WIDTH, TQ_BAND), BF16),
            jax.ShapeDtypeStruct((bsz, seq, MEM_WIDTH), BF16),
        ]
    return pl.pallas_call(
        functools.partial(_post_mixer_kernel, last=last),
        grid=(bsz, seq // tm),
        in_specs=in_specs,
        out_specs=out_specs,
        out_shape=out_shape,
        scratch_shapes=[pltpu.VMEM((HALO, D_FF), F32), pltpu.VMEM((tm, D_FF), BF16)],
        compiler_params=pltpu.CompilerParams(
            dimension_semantics=("arbitrary", "arbitrary"), vmem_limit_bytes=VMEM_LIMIT),
        name="post_mixer_last" if last else "post_mixer",
    )(x, mix, qmem, mkbdt, mvbd, *weights)


def _band_bias_kernel(tbl_ref, out_ref):
    h = pl.program_id(0)
    b = lax.broadcasted_iota(jnp.int32, (8, BIAS_LANES), 0)
    c = lax.broadcasted_iota(jnp.int32, (8, BIAS_LANES), 1)
    r_signed = jnp.where(c < TQ_HALF, c, c - BIAS_LANES)
    idx = jnp.clip(BAND_PREV + r_signed - b, -(CHUNK - 1), MAX_REL) + (CHUNK - 1)

    def body(e, acc):
        return jnp.where(idx == e, tbl_ref[h, e], acc)

    base = lax.fori_loop(0, REL_SIZE, body, jnp.zeros((8, BIAS_LANES), F32), unroll=8)
    kb = lax.broadcasted_iota(jnp.int32, (8, TQ_HALF), 0)
    r = lax.broadcasted_iota(jnp.int32, (8, TQ_HALF), 1)
    for a in range(BIAS_ROWS // 8):
        first = (-8 * a) % BIAS_LANES
        lo = first // TQ_HALF * TQ_HALF
        hi = (lo + TQ_HALF) % BIAS_LANES
        window = jnp.concatenate([base[:, lo:lo + TQ_HALF], base[:, hi:hi + TQ_HALF]], axis=1)
        if first != lo:
            window = pltpu.roll(window, 2 * TQ_HALF - (first - lo), axis=1)
        dchunk = (BAND_PREV + r) // CHUNK - (8 * a + kb) // CHUNK
        valid = (dchunk >= 0) & (dchunk < BAND_CHUNKS)
        out_ref[0, 8 * a:8 * a + 8, :] = jnp.where(valid, window[:, :TQ_HALF], NEG_BIG)


def _band_bias(rel_table):
    return pl.pallas_call(
        _band_bias_kernel,
        grid=(B_HEADS,),
        in_specs=[pl.BlockSpec(memory_space=pltpu.SMEM)],
        out_specs=pl.BlockSpec((1, BIAS_ROWS, TQ_HALF), lambda h: (h, 0, 0)),
        out_shape=jax.ShapeDtypeStruct((B_HEADS, BIAS_ROWS, TQ_HALF), F32),
        compiler_params=pltpu.CompilerParams(dimension_semantics=("arbitrary",)),
        name="band_bias",
    )(rel_table)


def _band_attn_kernel(qt_ref, k_ref, vt_ref, bias_ref, out_ref):
    j = pl.program_id(1)
    first_half = 2 * j - HALVES_BACK
    start = pl.multiple_of(jnp.maximum(first_half, 0) * TQ_HALF, TQ_HALF)
    pair_width = 2 * B_HEAD_DIM
    pair_row = lax.broadcasted_iota(jnp.int32, (pair_width, TQ_BAND), 0)
    zeros_half = jnp.zeros((TQ_HALF, TQ_HALF), BF16)
    masked_half = jnp.full((TQ_HALF, TQ_HALF), NEG_BIG, F32)

    def scores(pair):
        lo = pair * pair_width
        k_pair = k_ref[0, pl.ds(start, TK_BAND), lo:lo + pair_width]
        q_pair = qt_ref[0, 0, lo:lo + pair_width, :]
        zero = jnp.zeros_like(q_pair)
        q_sel = jnp.concatenate([jnp.where(pair_row < B_HEAD_DIM, q_pair, zero),
                                 jnp.where(pair_row >= B_HEAD_DIM, q_pair, zero)], axis=1)
        return _dot(k_pair, q_sel)

    def softmax_half(t, row0, c0):
        blocks = [t[row0 + r:row0 + r + SOFTMAX_ROWS, c0:c0 + TQ_HALF] for r in range(0, TK_HALF, SOFTMAX_ROWS)]
        m_acc = blocks[0]
        for blk in blocks[1:]:
            m_acc = jnp.maximum(m_acc, blk)
        m = jnp.max(m_acc, axis=0, keepdims=True)
        ps = []
        l_acc = None
        for blk in blocks:
            e = jnp.exp(blk - m)
            ps.append(e.astype(BF16))
            l_acc = e if l_acc is None else l_acc + e
        return jnp.concatenate(ps, axis=0), jnp.sum(l_acc, axis=0, keepdims=True)

    def tile(off_b, bias_row_a, bias_row_b):
        def bias_tile(pair):
            cols = []
            for h in (2 * pair, 2 * pair + 1):
                bias_a = bias_ref[h, pl.ds(bias_row_a, TK_HALF), :]
                bias_b = bias_ref[h, pl.ds(bias_row_b, TK_HALF), :]
                cols.append(jnp.concatenate([bias_a, masked_half], axis=0))
                cols.append(jnp.concatenate([masked_half, bias_b] if off_b else [bias_b, masked_half], axis=0))
            return jnp.concatenate(cols, axis=1)

        n_pairs = B_HEADS // 2
        live = [dict() for _ in range(n_pairs)]

        def biased_scores(pair):
            live[pair]["t"] = scores(pair) + bias_tile(pair)

        def softmax(pair):
            t = live[pair].pop("t")
            cols, sums = [], []
            for parity in range(2):
                c0 = parity * TQ_BAND
                p_a, l_a = softmax_half(t, 0, c0)
                p_b, l_b = softmax_half(t, off_b, c0 + TQ_HALF)
                cols.append(jnp.concatenate([p_a, zeros_half], axis=0))
                cols.append(jnp.concatenate([zeros_half, p_b] if off_b else [p_b, zeros_half], axis=0))
                sums += [l_a, l_b]
            live[pair]["p"] = jnp.concatenate(cols, axis=1)
            live[pair]["inv"] = 1.0 / jnp.concatenate(sums, axis=1)

        def values(pair):
            lo = pair * pair_width
            vt = vt_ref[0, lo:lo + pair_width, pl.ds(start, TK_BAND)]
            o = _dot(vt, live[pair].pop("p"))
            inv = live[pair].pop("inv")
            out_ref[0, 0, lo:lo + B_HEAD_DIM, :] = (o[:B_HEAD_DIM, :TQ_BAND] * inv[:, :TQ_BAND]).astype(BF16)
            out_ref[0, 0, lo + B_HEAD_DIM:lo + pair_width, :] = (
                o[B_HEAD_DIM:, TQ_BAND:] * inv[:, TQ_BAND:]).astype(BF16)

        stages = (biased_scores, softmax, values)
        for slot in range(n_pairs + len(stages) - 1):
            for depth in (0, 2, 1):
                if 0 <= slot - depth < n_pairs:
                    stages[depth](slot - depth)

    pl.when(first_half >= 0)(functools.partial(tile, TQ_HALF, 0, 0))

    @pl.when(first_half < 0)
    def _():
        row_a = pl.multiple_of(-first_half * TQ_HALF, TQ_HALF)
        tile(0, row_a, pl.multiple_of(row_a - TQ_HALF, TQ_HALF))


def _band_attn(qt, k, vt, bias):
    bsz, seq, _ = k.shape
    return pl.pallas_call(
        _band_attn_kernel,
        grid=(bsz, seq // TQ_BAND),
        in_specs=[
            pl.BlockSpec((1, 1, B_WIDTH, TQ_BAND), lambda b, j: (b, j, 0, 0)),
            pl.BlockSpec((1, seq, B_WIDTH), lambda b, j: (b, 0, 0)),
            pl.BlockSpec((1, B_WIDTH, seq), lambda b, j: (b, 0, 0)),
            _resident(bias.shape),
        ],
        out_specs=pl.BlockSpec((1, 1, B_WIDTH, TQ_BAND), lambda b, j: (b, j, 0, 0)),
        out_shape=jax.ShapeDtypeStruct((bsz, seq // TQ_BAND, B_WIDTH, TQ_BAND), BF16),
        compiler_params=pltpu.CompilerParams(
            dimension_semantics=("arbitrary", "arbitrary"), vmem_limit_bytes=VMEM_LIMIT),
        name="band_attn",
    )(qt, k, vt, bias)


def kernel(x, mem, norm_mix_g, norm_ffn_g, a_w_in, a_gate_b, a_conv_w, a_conv_b, a_head_g, a_w_out, kv_norm_g, w_kv,
           b_w_in, b_rel_bias, b_w_out, mem_w_kv, ffn_w_up, ffn_conv_w, ffn_conv_b, ffn_w_down, final_g):
    bsz, seq, d = x.shape
    assert d == D_MODEL and seq % max(TM_IN, L_MLSTM, TM_POST, TQ_BAND) == 0
    assert a_w_in.shape[0] == 1 and b_w_in.shape[0] == 1, "one mLSTM layer followed by one band-attention layer"
    assert TM_IN % L_MLSTM == 0, "the input projection emits per-chunk decay terms: whole chunks per tile"
    assert bsz % MLSTM_BATCH == 0
    row = lambda g: g.reshape(1, -1).astype(F32)

    wkt = jnp.swapaxes(mem_w_kv[:, :, :MEM_WIDTH], 1, 2).astype(BF16)
    wmv = mem_w_kv[:, :, MEM_WIDTH:].astype(BF16)
    mkbdt, mvbd = _mem_kv(mem.astype(BF16), wkt, wmv)

    mem_scale = MEM_HEAD_DIM ** -0.5
    assert math.log2(mem_scale).is_integer() and math.log2(B_HEAD_DIM ** -0.5).is_integer()
    w = a_w_in[0]
    wqk = w[:, 0:2 * A_WIDTH].astype(BF16)
    wvt = jnp.swapaxes(w[:, 2 * A_WIDTH:3 * A_WIDTH], 0, 1).reshape(A_HEADS, A_HEAD_DIM, D_MODEL)
    wvt = jnp.pad(wvt, ((0, 0), (0, V_AUG - A_HEAD_DIM), (0, 0))).reshape(A_HEADS * V_AUG, D_MODEL).astype(BF16)
    wot = (0.5 * jnp.swapaxes(w[:, 3 * A_WIDTH:4 * A_WIDTH], 0, 1)).astype(BF16)
    gate_pad = ((0, 0), (0, GATE_LANES - A_HEADS))
    g0 = 4 * A_WIDTH
    wgm = jnp.concatenate([jnp.pad(w[:, g0:g0 + A_HEADS], gate_pad), jnp.pad(w[:, g0 + A_HEADS:g0 + 2 * A_HEADS], gate_pad),
                           mem_scale * w[:, g0 + 2 * A_HEADS:]], axis=1).astype(BF16)
    gate_b = jnp.pad(a_gate_b[0].astype(F32).reshape(2, A_HEADS), gate_pad).reshape(1, 2 * GATE_LANES)
    vone = (jnp.arange(A_HEADS * V_AUG) % V_AUG == A_HEAD_DIM).astype(F32).reshape(A_HEADS * V_AUG, 1)
    q, k, vt, ot_half, gate_rows, a_cols, decay, qmem = _a_in(
        x, row(norm_mix_g[0]), wqk, wvt, wot, wgm, gate_b, vone, a_conv_w[0].astype(F32), row(a_conv_b[0]))

    head_g_half = 0.5 * a_head_g[0].reshape(A_HEADS, A_HEAD_DIM, 1).astype(F32)
    mix = _mlstm(q, k, vt, ot_half, gate_rows, a_cols, decay, head_g_half)

    def ffn_weights(l):
        return (row(norm_ffn_g[l]), ffn_w_up[l][:, :D_FF].astype(BF16), ffn_w_up[l][:, D_FF:].astype(BF16),
                ffn_conv_w[l].astype(F32), row(ffn_conv_b[l]), ffn_w_down[l].astype(BF16))

    wout = a_w_out[0]
    tail = (row(kv_norm_g), w_kv[:, :B_WIDTH].astype(BF16), jnp.swapaxes(w_kv[:, B_WIDTH:], 0, 1).astype(BF16),
            row(norm_mix_g[1]), (B_HEAD_DIM ** -0.5 * jnp.swapaxes(b_w_in[0][:, :B_WIDTH], 0, 1)).astype(BF16),
            (mem_scale * b_w_in[0][:, B_WIDTH:]).astype(BF16))
    x1, kb, vbt, qbt, qmem_b = _post_mixer(
        x, mix, qmem, mkbdt, mvbd, 0,
        wout[:A_WIDTH].astype(BF16), wout[A_WIDTH:].astype(BF16),
        *ffn_weights(0), tail=tail, last=False)

    bias = _band_bias(b_rel_bias[0].astype(F32))
    mix_b = _band_attn(qbt, kb, vbt, bias)
    wout = b_w_out[0]
    return _post_mixer(
        x1, mix_b, qmem_b, mkbdt, mvbd, 1,
        wout[:B_WIDTH].astype(BF16), wout[B_WIDTH:].astype(BF16),
        *ffn_weights(1), tail=(row(final_g),), last=True)
```

```python
import functools
import math

import jax
import jax.numpy as jnp
from jax import lax
from jax.experimental import pallas as pl
from jax.experimental.pallas import tpu as pltpu

F32 = jnp.float32
BF16 = jnp.bfloat16

LANES = 128
SUBLANES = 8
BF16_SUBLANES = 16
VMEM_BYTES = 64 * 1024 * 1024
VMEM_LIMIT = VMEM_BYTES // 8 * 7

D_MODEL = 1024
CHUNK = 64
MEM_TOKENS = 256
MEM_HEADS = 4
MEM_WIDTH = 256
MEM_HEAD_DIM = MEM_WIDTH // MEM_HEADS
A_WIDTH = 768
A_HEADS = 4
A_HEAD_DIM = A_WIDTH // A_HEADS
A_CONV = 4
B_HEADS = 12
B_HEAD_DIM = 64
B_WIDTH = B_HEADS * B_HEAD_DIM
BAND_CHUNKS = 9
MAX_REL = 128
REL_SIZE = MAX_REL + CHUNK
D_FF = 2816
FFN_CONV = 3
EPS = 1e-6

V_AUG = -(-(A_HEAD_DIM + 1) // BF16_SUBLANES) * BF16_SUBLANES
GATE_LANES = LANES
GATE_TERMS = 4
TERM_G, TERM_INTER, TERM_EN, TERM_WEXP = range(GATE_TERMS)
HALO = SUBLANES
NEG_BIG = -1e30

TM_IN = 512
L_MLSTM = 256
MLSTM_BATCH = 4
TM_POST = 512
FF_CHUNK = 256
TQ_HALF = LANES
TQ_BAND = 2 * TQ_HALF
BAND_PREV = (BAND_CHUNKS - 1) * CHUNK
TK_HALF = BAND_PREV + TQ_HALF
TK_BAND = BAND_PREV + TQ_BAND
HALVES_BACK = BAND_PREV // TQ_HALF
BIAS_ROWS = TK_HALF + BAND_PREV
BIAS_LANES = BIAS_ROWS + TQ_HALF
SOFTMAX_ROWS = 8 * SUBLANES


def _resident(shape):
    nd = len(shape)
    return pl.BlockSpec(shape, lambda *_: (0,) * nd, pipeline_mode=pl.Buffered(1))


def _rms(x, g):
    return x * lax.rsqrt(jnp.mean(x * x, axis=-1, keepdims=True) + EPS) * g


def _dot(a, b):
    return jnp.dot(a, b, preferred_element_type=F32)


def _dot_nt(a, b):
    return lax.dot_general(a, b, (((1,), (1,)), ((), ())), preferred_element_type=F32)


def _dot_tn(a, b):
    return lax.dot_general(a, b, (((0,), (0,)), ((), ())), preferred_element_type=F32)


def _silu(x):
    half = 0.5 * x
    return half + half * jnp.tanh(half)


def _mem_kv_kernel(mem_ref, wkt_ref, wv_ref, mkbdt_ref, mvbd_ref):
    m = mem_ref[0]
    mkt = _dot_nt(wkt_ref[0], m)
    mv = _dot(m, wv_ref[0])
    f_idx = lax.broadcasted_iota(jnp.int32, (MEM_WIDTH, MEM_HEADS * MEM_TOKENS), 0)
    c_idx = lax.broadcasted_iota(jnp.int32, (MEM_WIDTH, MEM_HEADS * MEM_TOKENS), 1)
    mkt4 = jnp.concatenate([mkt] * MEM_HEADS, axis=1)
    mkbdt_ref[0, 0] = jnp.where(f_idx // MEM_HEAD_DIM == c_idx // MEM_TOKENS, mkt4, 0.0).astype(BF16)
    r_idx = lax.broadcasted_iota(jnp.int32, (MEM_HEADS * MEM_TOKENS, MEM_WIDTH), 0)
    g_idx = lax.broadcasted_iota(jnp.int32, (MEM_HEADS * MEM_TOKENS, MEM_WIDTH), 1)
    mv4 = jnp.concatenate([mv] * MEM_HEADS, axis=0)
    mvbd_ref[0, 0] = jnp.where(r_idx // MEM_TOKENS == g_idx // MEM_HEAD_DIM, mv4, 0.0).astype(BF16)


def _mem_kv(mem_bf, wkt, wv):
    depth = wkt.shape[0]
    bsz = mem_bf.shape[0]
    return pl.pallas_call(
        _mem_kv_kernel,
        grid=(depth, bsz),
        in_specs=[
            pl.BlockSpec((1, MEM_TOKENS, D_MODEL), lambda l, b: (b, 0, 0)),
            pl.BlockSpec((1, MEM_WIDTH, D_MODEL), lambda l, b: (l, 0, 0)),
            pl.BlockSpec((1, D_MODEL, MEM_WIDTH), lambda l, b: (l, 0, 0)),
        ],
        out_specs=[
            pl.BlockSpec((1, 1, MEM_WIDTH, MEM_HEADS * MEM_TOKENS), lambda l, b: (l, b, 0, 0)),
            pl.BlockSpec((1, 1, MEM_HEADS * MEM_TOKENS, MEM_WIDTH), lambda l, b: (l, b, 0, 0)),
        ],
        out_shape=[
            jax.ShapeDtypeStruct((depth, bsz, MEM_WIDTH, MEM_HEADS * MEM_TOKENS), BF16),
            jax.ShapeDtypeStruct((depth, bsz, MEM_HEADS * MEM_TOKENS, MEM_WIDTH), BF16),
        ],
        compiler_params=pltpu.CompilerParams(dimension_semantics=("arbitrary", "arbitrary")),
        name="mem_kv",
    )(mem_bf, wkt, wv)


def _causal_taps(x, prev, taps):
    n_taps = taps.shape[0]
    halo_row = lax.broadcasted_iota(jnp.int32, (HALO, x.shape[1]), 0)
    y = taps[n_taps - 1:n_taps, :] * x
    for back in range(1, n_taps):
        head = jnp.where(halo_row < back, pltpu.roll(prev, back, axis=0), pltpu.roll(x[:HALO, :], back, axis=0))
        shifted = jnp.concatenate([head, pltpu.roll(x, back, axis=0)[HALO:, :]], axis=0)
        y = y + taps[n_taps - 1 - back:n_taps - back, :] * shifted
    return y


def _lane_scan(x, op, identity):
    lane = lax.broadcasted_iota(jnp.int32, x.shape, 1)
    step = 1
    while step < x.shape[1]:
        x = op(x, jnp.where(lane >= step, pltpu.roll(x, step, axis=1), identity))
        step *= 2
    return x


def _a_in_kernel(x_ref, g_ref, wqk_ref, wvot_ref, wgm_ref, gate_b_ref, vone_ref, cw_ref, cb_ref,
                 q_ref, k_ref, vt_ref, ot_ref, rows_ref, acol_ref, decay_ref, qmem_ref, halo_ref, m_ref):
    L = L_MLSTM
    n_chunks = x_ref.shape[1] // L

    @pl.when(pl.program_id(1) == 0)
    def _():
        halo_ref[...] = jnp.zeros(halo_ref.shape, F32)
        m_ref[...] = jnp.zeros(m_ref.shape, F32)

    def projections(c):
        tok = slice(c * L, (c + 1) * L)
        hn = _rms(x_ref[0, tok, :], g_ref[...]).astype(BF16)

        gm = _dot(hn, wgm_ref[...])
        qmem_ref[0, tok, :] = gm[:, 2 * GATE_LANES:].astype(BF16)
        gi = (gm[:, :GATE_LANES] + gate_b_ref[:, :GATE_LANES]).T[0:SUBLANES, :]
        fg = (gm[:, GATE_LANES:2 * GATE_LANES] + gate_b_ref[:, GATE_LANES:]).T[0:SUBLANES, :]
        head_row = lax.broadcasted_iota(jnp.int32, (SUBLANES, L), 0) < A_HEADS
        logf = jnp.where(head_row, jnp.minimum(fg, 0.0) - jnp.log(1.0 + jnp.exp(-jnp.abs(fg))), 0.0)
        b = _lane_scan(logf, jnp.add, 0.0)
        a = gi - b
        m_prev = m_ref[:, 0:1]
        g = jnp.maximum(_lane_scan(a, jnp.maximum, -jnp.inf), m_prev)
        g_last = g[:, L - 1:L]
        m_ref[...] = jnp.broadcast_to(b[:, L - 1:L] + g_last, m_ref.shape)
        decay_ref[0, c] = jnp.broadcast_to(jnp.exp(m_prev - g_last), decay_ref.shape[2:])
        terms = [None] * GATE_TERMS
        terms[TERM_G], terms[TERM_INTER] = g, jnp.exp(m_prev - g)
        terms[TERM_EN], terms[TERM_WEXP] = jnp.exp(-(b + g)), jnp.exp(a - g_last)
        rows_ref[0, :, tok] = jnp.concatenate(terms, axis=0)
        acol_ref[0, tok, :] = jnp.concatenate([a, jnp.zeros((GATE_LANES - SUBLANES, L), F32)], axis=0).T

        raw = _dot(hn, wqk_ref[...])
        vo = _dot_nt(wvot_ref[...], hn)
        n_v = vt_ref.shape[2]
        vt_ref[0, c] = (vo[:n_v] + vone_ref[...]).astype(BF16)
        ot_ref[0, c] = vo[n_v:]
        return raw

    def conv(c, raw):
        tok = slice(c * L, (c + 1) * L)
        prev = halo_ref[...]
        halo_ref[...] = raw[L - HALO:, :]
        qk = _silu(_causal_taps(raw, prev, cw_ref[...]) + cb_ref[...])
        for h in range(A_HEADS):
            lo = h * A_HEAD_DIM
            q_ref[0, h, tok, :] = qk[:, lo:lo + A_HEAD_DIM].astype(BF16)
            k_ref[0, h, tok, :] = (qk[:, A_WIDTH + lo:A_WIDTH + lo + A_HEAD_DIM]
                                   * (A_HEAD_DIM ** -0.5)).astype(BF16)

    raws = {}
    for slot in range(n_chunks + 1):
        if slot < n_chunks:
            raws[slot] = projections(slot)
        if slot >= 1:
            conv(slot - 1, raws.pop(slot - 1))


def _a_in(x, g, wqk, wvot, wgm, gate_b, vone, cw, cb):
    bsz, seq, _ = x.shape
    tm = TM_IN
    head_spec = lambda width: pl.BlockSpec((1, A_HEADS, tm, width), lambda b, j: (b, 0, j, 0))
    row_spec = lambda width: pl.BlockSpec((1, tm, width), lambda b, j: (b, j, 0))
    col_spec = lambda width: pl.BlockSpec((1, width, tm), lambda b, j: (b, 0, j))
    tile_spec = lambda width: pl.BlockSpec((1, tm // L_MLSTM, width, L_MLSTM), lambda b, j: (b, j, 0, 0))
    weights = (g, wqk, wvot, wgm, gate_b, vone, cw, cb)
    return pl.pallas_call(
        _a_in_kernel,
        grid=(bsz, seq // tm),
        in_specs=[row_spec(D_MODEL)] + [_resident(w.shape) for w in weights],
        out_specs=[
            head_spec(A_HEAD_DIM), head_spec(A_HEAD_DIM), tile_spec(A_HEADS * V_AUG), tile_spec(A_WIDTH),
            col_spec(GATE_TERMS * SUBLANES),
            row_spec(GATE_LANES),
            pl.BlockSpec((1, tm // L_MLSTM, SUBLANES, GATE_LANES), lambda b, j: (b, j, 0, 0)),
            row_spec(MEM_WIDTH),
        ],
        out_shape=[
            jax.ShapeDtypeStruct((bsz, A_HEADS, seq, A_HEAD_DIM), BF16),
            jax.ShapeDtypeStruct((bsz, A_HEADS, seq, A_HEAD_DIM), BF16),
            jax.ShapeDtypeStruct((bsz, seq // L_MLSTM, A_HEADS * V_AUG, L_MLSTM), BF16),
            jax.ShapeDtypeStruct((bsz, seq // L_MLSTM, A_WIDTH, L_MLSTM), F32),
            jax.ShapeDtypeStruct((bsz, GATE_TERMS * SUBLANES, seq), F32),
            jax.ShapeDtypeStruct((bsz, seq, GATE_LANES), F32),
            jax.ShapeDtypeStruct((bsz, seq // L_MLSTM, SUBLANES, GATE_LANES), F32),
            jax.ShapeDtypeStruct((bsz, seq, MEM_WIDTH), BF16),
        ],
        scratch_shapes=[pltpu.VMEM((HALO, 2 * A_WIDTH), F32), pltpu.VMEM((SUBLANES, GATE_LANES), F32)],
        compiler_params=pltpu.CompilerParams(
            dimension_semantics=("arbitrary", "arbitrary"), vmem_limit_bytes=VMEM_LIMIT),
        name="a_in",
    )(x, *weights)


def _mlstm_kernel(q_ref, k_ref, vt_ref, ot_ref, rows_ref, acol_ref, decay_ref, hg_ref, out_ref, ct_ref):
    L = q_ref.shape[2]
    half = L // 2

    @pl.when(pl.program_id(1) == 0)
    def _():
        ct_ref[...] = jnp.zeros(ct_ref.shape, F32)

    upper = (lax.broadcasted_iota(jnp.int32, (half, half), 0)
             <= lax.broadcasted_iota(jnp.int32, (half, half), 1))
    zero_block = jnp.zeros((half, half), BF16)
    items = [(n, h) for n in range(q_ref.shape[0]) for h in range(A_HEADS)]
    live = [dict() for _ in items]

    def term(n, which, h):
        return rows_ref[n, SUBLANES * which + h:SUBLANES * which + h + 1, :]

    def scores(i):
        n, h = items[i]
        live[i]["st"] = _dot_nt(k_ref[n, h], q_ref[n, h])

    def decay_weight(i):
        n, h = items[i]
        st = live[i].pop("st")
        a_col = acol_ref[n, :, h:h + 1]
        g_row = term(n, TERM_G, h)
        p00 = jnp.exp(jnp.where(upper, a_col[:half] - g_row[:, :half], -jnp.inf))
        p01 = jnp.exp(a_col[:half] - g_row[:, half:])
        p11 = jnp.exp(jnp.where(upper, a_col[half:] - g_row[:, half:], -jnp.inf))
        live[i]["s"] = jnp.concatenate([
            jnp.concatenate([(st[:half, :half] * p00).astype(BF16), (st[:half, half:] * p01).astype(BF16)], axis=1),
            jnp.concatenate([zero_block, (st[half:, half:] * p11).astype(BF16)], axis=1)], axis=0)

    def numerator(i):
        n, h = items[i]
        vt = vt_ref[n, 0, h * V_AUG:(h + 1) * V_AUG, :]
        ct_prev = ct_ref[n, h]
        live[i]["num"] = (_dot(vt, live[i].pop("s"))
                          + _dot_nt(ct_prev.astype(BF16), q_ref[n, h]) * term(n, TERM_INTER, h))
        wvt = (vt.astype(F32) * term(n, TERM_WEXP, h)).astype(BF16)
        ct_ref[n, h] = decay_ref[n, 0, h:h + 1, 0:1] * ct_prev + _dot(wvt, k_ref[n, h])

    def normalise(i):
        n, h = items[i]
        numt = live[i].pop("num")
        den = numt[A_HEAD_DIM:A_HEAD_DIM + 1, :]
        r = 1.0 / jnp.maximum(jnp.abs(den), term(n, TERM_EN, h))
        body = numt[:A_HEAD_DIM, :]
        ssq = jnp.sum(body * body, axis=0, keepdims=True)
        factor = r * lax.rsqrt(r * r * ssq * (1.0 / A_HEAD_DIM) + EPS)
        lo = h * A_HEAD_DIM
        gate = hg_ref[h] + hg_ref[h] * jnp.tanh(ot_ref[n, 0, lo:lo + A_HEAD_DIM, :])
        out_ref[n, 0, lo:lo + A_HEAD_DIM, :] = (body * factor * gate).astype(BF16)

    stages = (scores, decay_weight, numerator, normalise)
    for slot in range(len(items) + len(stages) - 1):
        for depth, stage in enumerate(stages):
            if 0 <= slot - depth < len(items):
                stage(slot - depth)


def _mlstm(q, k, vt, ot_half, rows, a_cols, decay, head_g_half):
    bsz, nh, seq, dh = q.shape
    L = L_MLSTM
    nb = MLSTM_BATCH
    head_spec = pl.BlockSpec((nb, nh, L, dh), lambda b, j: (b, 0, j, 0))
    col_spec = lambda width: pl.BlockSpec((nb, width, L), lambda b, j: (b, 0, j))
    tile_spec = lambda width: pl.BlockSpec((nb, 1, width, L), lambda b, j: (b, j, 0, 0))
    return pl.pallas_call(
        _mlstm_kernel,
        grid=(bsz // nb, seq // L),
        in_specs=[
            head_spec, head_spec, tile_spec(nh * V_AUG), tile_spec(nh * dh), col_spec(rows.shape[1]),
            pl.BlockSpec((nb, L, GATE_LANES), lambda b, j: (b, j, 0)),
            pl.BlockSpec((nb, 1, SUBLANES, GATE_LANES), lambda b, j: (b, j, 0, 0)),
            _resident(head_g_half.shape),
        ],
        out_specs=tile_spec(nh * dh),
        out_shape=jax.ShapeDtypeStruct((bsz, seq // L, nh * dh, L), BF16),
        scratch_shapes=[pltpu.VMEM((nb, nh, V_AUG, dh), F32)],
        compiler_params=pltpu.CompilerParams(
            dimension_semantics=("arbitrary", "arbitrary"), vmem_limit_bytes=VMEM_LIMIT),
        name="mlstm",
    )(q, k, vt, ot_half, rows, a_cols, decay, head_g_half)


def _post_mixer_kernel(*refs, last):
    if last:
        (x_ref, mix_ref, qmem_ref, mkbdt_ref, mvbd_ref, wout_mix_ref, wout_mem_ref, ffn_g_ref, wu_ref, wg_ref,
         cw_ref, cb_ref, wd_ref, final_g_ref, out_ref, halo_ref, act_ref) = refs
    else:
        (x_ref, mix_ref, qmem_ref, mkbdt_ref, mvbd_ref, wout_mix_ref, wout_mem_ref, ffn_g_ref, wu_ref, wg_ref,
         cw_ref, cb_ref, wd_ref, kv_g_ref, wk_ref, wvt_ref, nxt_g_ref, wqt_ref, wqmem_ref,
         out_ref, k_ref, vt_ref, qnt_ref, qmemn_ref, halo_ref, act_ref) = refs
    tm = x_ref.shape[1]

    @pl.when(pl.program_id(1) == 0)
    def _():
        halo_ref[...] = jnp.zeros((HALO, D_FF), F32)

    s = _dot(qmem_ref[0], mkbdt_ref[0, 0])
    probs = []
    for h in range(MEM_HEADS):
        sh = s[:, h * MEM_TOKENS:(h + 1) * MEM_TOKENS]
        e = jnp.exp(sh - jnp.max(sh, axis=1, keepdims=True))
        probs.append((e * (1.0 / jnp.sum(e, axis=1, keepdims=True))).astype(BF16))
    mem_out = _dot(jnp.concatenate(probs, axis=1), mvbd_ref[0, 0])

    y = _dot(mem_out.astype(BF16), wout_mem_ref[...])
    y = y + jnp.concatenate([_dot_tn(mix_ref[0, i], wout_mix_ref[...]) for i in range(mix_ref.shape[1])], axis=0)
    x1 = x_ref[0] + y

    hn = _rms(x1, ffn_g_ref[...]).astype(BF16)
    chunk_cols = [slice(c0, min(c0 + FF_CHUNK, D_FF)) for c0 in range(0, D_FF, FF_CHUNK)]
    n_chunks = len(chunk_cols)

    def up(c):
        return _dot(hn, wu_ref[:, chunk_cols[c]]), _dot(hn, wg_ref[:, chunk_cols[c]])

    u, g = up(0)
    for c in range(n_chunks):
        cols = chunk_cols[c]
        nxt = up(c + 1) if c + 1 < n_chunks else None
        prev = halo_ref[:, cols]
        halo_ref[:, cols] = g[tm - HALO:, :]
        gc = _causal_taps(g, prev, cw_ref[:, cols]) + cb_ref[:, cols]
        act_ref[:, cols] = (_silu(gc) * u).astype(BF16)
        if nxt is not None:
            u, g = nxt
    x2 = x1 + _dot(act_ref[...], wd_ref[...])

    if last:
        out_ref[0] = _rms(x2, final_g_ref[...])
    else:
        out_ref[0] = x2
        hkv = _rms(x2, kv_g_ref[...]).astype(BF16)
        k_ref[0] = _dot(hkv, wk_ref[...]).astype(BF16)
        vt_ref[0] = _dot_nt(wvt_ref[...], hkv).astype(BF16)
        hq = _rms(x2, nxt_g_ref[...]).astype(BF16)
        qnt = _dot_nt(wqt_ref[...], hq).astype(BF16)
        for i in range(qnt_ref.shape[1]):
            qnt_ref[0, i] = qnt[:, i * TQ_BAND:(i + 1) * TQ_BAND]
        qmemn_ref[0] = _dot(hq, wqmem_ref[...]).astype(BF16)


def _post_mixer(x, mix, qmem, mkbdt, mvbd, layer, wout_mix, wout_mem, ffn_g, wu, wg, cw, cb, wd, tail, last):
    bsz, seq, _ = x.shape
    tm = TM_POST
    row_spec = lambda width: pl.BlockSpec((1, tm, width), lambda b, j: (b, j, 0))
    col_spec = lambda width: pl.BlockSpec((1, width, tm), lambda b, j: (b, 0, j))
    tile_spec = lambda width, tile: pl.BlockSpec((1, tm // tile, width, tile), lambda b, j: (b, j, 0, 0))
    mix_spec = tile_spec(mix.shape[2], mix.shape[3])
    weights = (wout_mix, wout_mem, ffn_g, wu, wg, cw, cb, wd) + tuple(tail)
    in_specs = [
        row_spec(D_MODEL), mix_spec, row_spec(MEM_WIDTH),
        pl.BlockSpec((1, 1) + mkbdt.shape[2:], lambda b, j: (layer, b, 0, 0)),
        pl.BlockSpec((1, 1) + mvbd.shape[2:], lambda b, j: (layer, b, 0, 0)),
    ] + [_resident(w.shape) for w in weights]
    if last:
        out_specs = row_spec(D_MODEL)
        out_shape = jax.ShapeDtypeStruct((bsz, seq, D_MODEL), F32)
    else:
        out_specs = [row_spec(D_MODEL), row_spec(B_WIDTH), col_spec(B_WIDTH), tile_spec(B_WIDTH, TQ_BAND),
                     row_spec(MEM_WIDTH)]
        out_shape = [
            jax.ShapeDtypeStruct((bsz, seq, D_MODEL), F32),
            jax.ShapeDtypeStruct((bsz, seq, B_WIDTH), BF16),
            jax.ShapeDtypeStruct((bsz, B_WIDTH, seq), BF16),
            jax.ShapeDtypeStruct((bsz, seq // TQ_BAND, B_WIDTH, TQ_BAND), BF16),
            jax.ShapeDtypeStruct((bsz, seq, MEM_WIDTH), BF16),
        ]
    return pl.pallas_call(
        functools.partial(_post_mixer_kernel, last=last),
        grid=(bsz, seq // tm),
        in_specs=in_specs,
        out_specs=out_specs,
        out_shape=out_shape,
        scratch_shapes=[pltpu.VMEM((HALO, D_FF), F32), pltpu.VMEM((tm, D_FF), BF16)],
        compiler_params=pltpu.CompilerParams(
            dimension_semantics=("arbitrary", "arbitrary"), vmem_limit_bytes=VMEM_LIMIT),
        name="post_mixer_last" if last else "post_mixer",
    )(x, mix, qmem, mkbdt, mvbd, *weights)


def _band_bias_kernel(tbl_ref, out_ref):
    h = pl.program_id(0)
    b = lax.broadcasted_iota(jnp.int32, (SUBLANES, BIAS_LANES), 0)
    c = lax.broadcasted_iota(jnp.int32, (SUBLANES, BIAS_LANES), 1)
    r_signed = jnp.where(c < TQ_HALF, c, c - BIAS_LANES)
    idx = jnp.clip(BAND_PREV + r_signed - b, -(CHUNK - 1), MAX_REL) + (CHUNK - 1)

    def body(e, acc):
        return jnp.where(idx == e, tbl_ref[h, e], acc)

    base = lax.fori_loop(0, REL_SIZE, body, jnp.zeros((SUBLANES, BIAS_LANES), F32), unroll=8)
    kb = lax.broadcasted_iota(jnp.int32, (SUBLANES, TQ_HALF), 0)
    r = lax.broadcasted_iota(jnp.int32, (SUBLANES, TQ_HALF), 1)
    for k0 in range(0, BIAS_ROWS, SUBLANES):
        first = -k0 % BIAS_LANES
        lo = first // TQ_HALF * TQ_HALF
        hi = (lo + TQ_HALF) % BIAS_LANES
        window = jnp.concatenate([base[:, lo:lo + TQ_HALF], base[:, hi:hi + TQ_HALF]], axis=1)
        if first != lo:
            window = pltpu.roll(window, 2 * TQ_HALF - (first - lo), axis=1)
        dchunk = (BAND_PREV + r) // CHUNK - (k0 + kb) // CHUNK
        valid = (dchunk >= 0) & (dchunk < BAND_CHUNKS)
        out_ref[0, k0:k0 + SUBLANES, :] = jnp.where(valid, window[:, :TQ_HALF], NEG_BIG)


def _band_bias(rel_table):
    return pl.pallas_call(
        _band_bias_kernel,
        grid=(B_HEADS,),
        in_specs=[pl.BlockSpec(memory_space=pltpu.SMEM)],
        out_specs=pl.BlockSpec((1, BIAS_ROWS, TQ_HALF), lambda h: (h, 0, 0)),
        out_shape=jax.ShapeDtypeStruct((B_HEADS, BIAS_ROWS, TQ_HALF), F32),
        compiler_params=pltpu.CompilerParams(dimension_semantics=("arbitrary",)),
        name="band_bias",
    )(rel_table)


def _band_attn_kernel(qt_ref, k_ref, vt_ref, bias_ref, out_ref):
    j = pl.program_id(1)
    first_half = 2 * j - HALVES_BACK
    start = pl.multiple_of(jnp.maximum(first_half, 0) * TQ_HALF, TQ_HALF)
    pair_width = 2 * B_HEAD_DIM
    pair_row = lax.broadcasted_iota(jnp.int32, (pair_width, TQ_BAND), 0)
    zeros_half = jnp.zeros((TQ_HALF, TQ_HALF), BF16)
    masked_half = jnp.full((TQ_HALF, TQ_HALF), NEG_BIG, F32)

    def scores(pair):
        lo = pair * pair_width
        k_pair = k_ref[0, pl.ds(start, TK_BAND), lo:lo + pair_width]
        q_pair = qt_ref[0, 0, lo:lo + pair_width, :]
        zero = jnp.zeros_like(q_pair)
        q_sel = jnp.concatenate([jnp.where(pair_row < B_HEAD_DIM, q_pair, zero),
                                 jnp.where(pair_row >= B_HEAD_DIM, q_pair, zero)], axis=1)
        return _dot(k_pair, q_sel)

    def softmax_half(t, row0, c0):
        blocks = [t[row0 + r:row0 + r + SOFTMAX_ROWS, c0:c0 + TQ_HALF] for r in range(0, TK_HALF, SOFTMAX_ROWS)]
        m_acc = blocks[0]
        for blk in blocks[1:]:
            m_acc = jnp.maximum(m_acc, blk)
        m = jnp.max(m_acc, axis=0, keepdims=True)
        ps = []
        l_acc = None
        for blk in blocks:
            e = jnp.exp(blk - m)
            ps.append(e.astype(BF16))
            l_acc = e if l_acc is None else l_acc + e
        return jnp.concatenate(ps, axis=0), jnp.sum(l_acc, axis=0, keepdims=True)

    def tile(off_b, bias_row_a, bias_row_b):
        def bias_tile(pair):
            cols = []
            for h in (2 * pair, 2 * pair + 1):
                bias_a = bias_ref[h, pl.ds(bias_row_a, TK_HALF), :]
                bias_b = bias_ref[h, pl.ds(bias_row_b, TK_HALF), :]
                cols.append(jnp.concatenate([bias_a, masked_half], axis=0))
                cols.append(jnp.concatenate([masked_half, bias_b] if off_b else [bias_b, masked_half], axis=0))
            return jnp.concatenate(cols, axis=1)

        n_pairs = B_HEADS // 2
        live = [dict() for _ in range(n_pairs)]

        def biased_scores(pair):
            live[pair]["t"] = scores(pair) + bias_tile(pair)

        def softmax(pair):
            t = live[pair].pop("t")
            cols, sums = [], []
            for parity in range(2):
                c0 = parity * TQ_BAND
                p_a, l_a = softmax_half(t, 0, c0)
                p_b, l_b = softmax_half(t, off_b, c0 + TQ_HALF)
                cols.append(jnp.concatenate([p_a, zeros_half], axis=0))
                cols.append(jnp.concatenate([zeros_half, p_b] if off_b else [p_b, zeros_half], axis=0))
                sums += [l_a, l_b]
            live[pair]["p"] = jnp.concatenate(cols, axis=1)
            live[pair]["inv"] = 1.0 / jnp.concatenate(sums, axis=1)

        def values(pair):
            lo = pair * pair_width
            vt = vt_ref[0, lo:lo + pair_width, pl.ds(start, TK_BAND)]
            o = _dot(vt, live[pair].pop("p"))
            inv = live[pair].pop("inv")
            out_ref[0, 0, lo:lo + B_HEAD_DIM, :] = (o[:B_HEAD_DIM, :TQ_BAND] * inv[:, :TQ_BAND]).astype(BF16)
            out_ref[0, 0, lo + B_HEAD_DIM:lo + pair_width, :] = (
                o[B_HEAD_DIM:, TQ_BAND:] * inv[:, TQ_BAND:]).astype(BF16)

        stages = (biased_scores, softmax, values)
        for slot in range(n_pairs + len(stages) - 1):
            for depth in (0, 2, 1):
                if 0 <= slot - depth < n_pairs:
                    stages[depth](slot - depth)

    pl.when(first_half >= 0)(functools.partial(tile, TQ_HALF, 0, 0))

    @pl.when(first_half < 0)
    def _():
        row_a = pl.multiple_of(-first_half * TQ_HALF, TQ_HALF)
        tile(0, row_a, pl.multiple_of(row_a - TQ_HALF, TQ_HALF))


def _band_attn(qt, k, vt, bias):
    bsz, seq, _ = k.shape
    return pl.pallas_call(
        _band_attn_kernel,
        grid=(bsz, seq // TQ_BAND),
        in_specs=[
            pl.BlockSpec((1, 1, B_WIDTH, TQ_BAND), lambda b, j: (b, j, 0, 0)),
            pl.BlockSpec((1, seq, B_WIDTH), lambda b, j: (b, 0, 0)),
            pl.BlockSpec((1, B_WIDTH, seq), lambda b, j: (b, 0, 0)),
            _resident(bias.shape),
        ],
        out_specs=pl.BlockSpec((1, 1, B_WIDTH, TQ_BAND), lambda b, j: (b, j, 0, 0)),
        out_shape=jax.ShapeDtypeStruct((bsz, seq // TQ_BAND, B_WIDTH, TQ_BAND), BF16),
        compiler_params=pltpu.CompilerParams(
            dimension_semantics=("arbitrary", "arbitrary"), vmem_limit_bytes=VMEM_LIMIT),
        name="band_attn",
    )(qt, k, vt, bias)


def kernel(x, mem, norm_mix_g, norm_ffn_g, a_w_in, a_gate_b, a_conv_w, a_conv_b, a_head_g, a_w_out, kv_norm_g, w_kv,
           b_w_in, b_rel_bias, b_w_out, mem_w_kv, ffn_w_up, ffn_conv_w, ffn_conv_b, ffn_w_down, final_g):
    bsz, seq, d = x.shape
    assert d == D_MODEL and seq % max(TM_IN, L_MLSTM, TM_POST, TQ_BAND) == 0
    assert a_w_in.shape[0] == 1 and b_w_in.shape[0] == 1, "one mLSTM layer followed by one band-attention layer"
    assert TM_IN % L_MLSTM == 0, "the input projection emits per-chunk decay terms: whole chunks per tile"
    assert bsz % MLSTM_BATCH == 0
    row = lambda g: g.reshape(1, -1).astype(F32)

    wkt = jnp.swapaxes(mem_w_kv[:, :, :MEM_WIDTH], 1, 2).astype(BF16)
    wmv = mem_w_kv[:, :, MEM_WIDTH:].astype(BF16)
    mkbdt, mvbd = _mem_kv(mem.astype(BF16), wkt, wmv)

    mem_scale = MEM_HEAD_DIM ** -0.5
    assert math.log2(mem_scale).is_integer() and math.log2(B_HEAD_DIM ** -0.5).is_integer()
    w = a_w_in[0]
    wqk = w[:, 0:2 * A_WIDTH].astype(BF16)
    wvt = jnp.swapaxes(w[:, 2 * A_WIDTH:3 * A_WIDTH], 0, 1).reshape(A_HEADS, A_HEAD_DIM, D_MODEL)
    wvt = jnp.pad(wvt, ((0, 0), (0, V_AUG - A_HEAD_DIM), (0, 0))).reshape(A_HEADS * V_AUG, D_MODEL)
    wvot = jnp.concatenate([wvt, 0.5 * jnp.swapaxes(w[:, 3 * A_WIDTH:4 * A_WIDTH], 0, 1)], axis=0).astype(BF16)
    gate_pad = ((0, 0), (0, GATE_LANES - A_HEADS))
    g0 = 4 * A_WIDTH
    wgm = jnp.concatenate([jnp.pad(w[:, g0:g0 + A_HEADS], gate_pad), jnp.pad(w[:, g0 + A_HEADS:g0 + 2 * A_HEADS], gate_pad),
                           mem_scale * w[:, g0 + 2 * A_HEADS:]], axis=1).astype(BF16)
    gate_b = jnp.pad(a_gate_b[0].astype(F32).reshape(2, A_HEADS), gate_pad).reshape(1, 2 * GATE_LANES)
    vone = (jnp.arange(A_HEADS * V_AUG) % V_AUG == A_HEAD_DIM).astype(F32).reshape(A_HEADS * V_AUG, 1)
    q, k, vt, ot_half, gate_rows, a_cols, decay, qmem = _a_in(
        x, row(norm_mix_g[0]), wqk, wvot, wgm, gate_b, vone, a_conv_w[0].astype(F32), row(a_conv_b[0]))

    head_g_half = 0.5 * a_head_g[0].reshape(A_HEADS, A_HEAD_DIM, 1).astype(F32)
    mix = _mlstm(q, k, vt, ot_half, gate_rows, a_cols, decay, head_g_half)

    def ffn_weights(l):
        return (row(norm_ffn_g[l]), ffn_w_up[l][:, :D_FF].astype(BF16), ffn_w_up[l][:, D_FF:].astype(BF16),
                ffn_conv_w[l].astype(F32), row(ffn_conv_b[l]), ffn_w_down[l].astype(BF16))

    wout = a_w_out[0]
    tail = (row(kv_norm_g), w_kv[:, :B_WIDTH].astype(BF16), jnp.swapaxes(w_kv[:, B_WIDTH:], 0, 1).astype(BF16),
            row(norm_mix_g[1]), (B_HEAD_DIM ** -0.5 * jnp.swapaxes(b_w_in[0][:, :B_WIDTH], 0, 1)).astype(BF16),
            (mem_scale * b_w_in[0][:, B_WIDTH:]).astype(BF16))
    x1, kb, vbt, qbt, qmem_b = _post_mixer(
        x, mix, qmem, mkbdt, mvbd, 0,
        wout[:A_WIDTH].astype(BF16), wout[A_WIDTH:].astype(BF16),
        *ffn_weights(0), tail=tail, last=False)

    bias = _band_bias(b_rel_bias[0].astype(F32))
    mix_b = _band_attn(qbt, kb, vbt, bias)
    wout = b_w_out[0]
    return _post_mixer(
        x1, mix_b, qmem_b, mkbdt, mvbd, 1,
        wout[:B_WIDTH].astype(BF16), wout[B_WIDTH:].astype(BF16),
        *ffn_weights(1), tail=(row(final_g),), last=True)
```

```python
import functools
import math

import jax
import jax.numpy as jnp
from jax import lax
from jax.experimental import pallas as pl
from jax.experimental.pallas import tpu as pltpu

F32 = jnp.float32
BF16 = jnp.bfloat16

LANES = 128
SUBLANES = 8
BF16_SUBLANES = 16
VMEM_BYTES = 64 * 1024 * 1024
VMEM_LIMIT = VMEM_BYTES // 8 * 7

D_MODEL = 1024
CHUNK = 64
MEM_TOKENS = 256
MEM_HEADS = 4
MEM_WIDTH = 256
MEM_HEAD_DIM = MEM_WIDTH // MEM_HEADS
A_WIDTH = 768
A_HEADS = 4
A_HEAD_DIM = A_WIDTH // A_HEADS
A_CONV = 4
B_HEADS = 12
B_HEAD_DIM = 64
B_WIDTH = B_HEADS * B_HEAD_DIM
BAND_CHUNKS = 9
MAX_REL = 128
REL_SIZE = MAX_REL + CHUNK
D_FF = 2816
FFN_CONV = 3
EPS = 1e-6

V_AUG = -(-(A_HEAD_DIM + 1) // BF16_SUBLANES) * BF16_SUBLANES
GATE_LANES = LANES
GATE_TERMS = 4
TERM_G, TERM_INTER, TERM_EN, TERM_WEXP = range(GATE_TERMS)
HALO = SUBLANES
NEG_BIG = -1e30

TM_IN = 512
L_MLSTM = 256
MLSTM_BATCH = 4
TM_POST = 512
FF_CHUNK = 256
TQ_HALF = LANES
TQ_BAND = 2 * TQ_HALF
BAND_PREV = (BAND_CHUNKS - 1) * CHUNK
TK_HALF = BAND_PREV + TQ_HALF
TK_BAND = BAND_PREV + TQ_BAND
HALVES_BACK = BAND_PREV // TQ_HALF
BIAS_ROWS = TK_HALF + BAND_PREV
BIAS_LANES = BIAS_ROWS + TQ_HALF
SOFTMAX_ROWS = 8 * SUBLANES


def _resident(shape):
    nd = len(shape)
    return pl.BlockSpec(shape, lambda *_: (0,) * nd, pipeline_mode=pl.Buffered(1))


def _rms(x, g):
    return x * lax.rsqrt(jnp.mean(x * x, axis=-1, keepdims=True) + EPS) * g


def _dot(a, b):
    return jnp.dot(a, b, preferred_element_type=F32)


def _dot_nt(a, b):
    return lax.dot_general(a, b, (((1,), (1,)), ((), ())), preferred_element_type=F32)


def _dot_tn(a, b):
    return lax.dot_general(a, b, (((0,), (0,)), ((), ())), preferred_element_type=F32)


def _silu(x):
    half = 0.5 * x
    return half + half * jnp.tanh(half)


def _mem_kv_kernel(mem_ref, wkt_ref, wv_ref, mkbdt_ref, mvbd_ref):
    m = mem_ref[0]
    mkt = _dot_nt(wkt_ref[0], m)
    mv = _dot(m, wv_ref[0])
    f_idx = lax.broadcasted_iota(jnp.int32, (MEM_WIDTH, MEM_HEADS * MEM_TOKENS), 0)
    c_idx = lax.broadcasted_iota(jnp.int32, (MEM_WIDTH, MEM_HEADS * MEM_TOKENS), 1)
    mkt4 = jnp.concatenate([mkt] * MEM_HEADS, axis=1)
    mkbdt_ref[0, 0] = jnp.where(f_idx // MEM_HEAD_DIM == c_idx // MEM_TOKENS, mkt4, 0.0).astype(BF16)
    r_idx = lax.broadcasted_iota(jnp.int32, (MEM_HEADS * MEM_TOKENS, MEM_WIDTH), 0)
    g_idx = lax.broadcasted_iota(jnp.int32, (MEM_HEADS * MEM_TOKENS, MEM_WIDTH), 1)
    mv4 = jnp.concatenate([mv] * MEM_HEADS, axis=0)
    mvbd_ref[0, 0] = jnp.where(r_idx // MEM_TOKENS == g_idx // MEM_HEAD_DIM, mv4, 0.0).astype(BF16)


def _mem_kv(mem_bf, wkt, wv):
    depth = wkt.shape[0]
    bsz = mem_bf.shape[0]
    return pl.pallas_call(
        _mem_kv_kernel,
        grid=(depth, bsz),
        in_specs=[
            pl.BlockSpec((1, MEM_TOKENS, D_MODEL), lambda l, b: (b, 0, 0)),
            pl.BlockSpec((1, MEM_WIDTH, D_MODEL), lambda l, b: (l, 0, 0)),
            pl.BlockSpec((1, D_MODEL, MEM_WIDTH), lambda l, b: (l, 0, 0)),
        ],
        out_specs=[
            pl.BlockSpec((1, 1, MEM_WIDTH, MEM_HEADS * MEM_TOKENS), lambda l, b: (l, b, 0, 0)),
            pl.BlockSpec((1, 1, MEM_HEADS * MEM_TOKENS, MEM_WIDTH), lambda l, b: (l, b, 0, 0)),
        ],
        out_shape=[
            jax.ShapeDtypeStruct((depth, bsz, MEM_WIDTH, MEM_HEADS * MEM_TOKENS), BF16),
            jax.ShapeDtypeStruct((depth, bsz, MEM_HEADS * MEM_TOKENS, MEM_WIDTH), BF16),
        ],
        compiler_params=pltpu.CompilerParams(dimension_semantics=("arbitrary", "arbitrary")),
        name="mem_kv",
    )(mem_bf, wkt, wv)


def _prep_weights_kernel(a_ref, kv_ref, b_ref, wqk_ref, wvot_ref, wk_ref, wvtb_ref, wqt_ref, wqmem_ref):
    wqk_ref[...] = a_ref[0, :, :2 * A_WIDTH].astype(BF16)
    vt = a_ref[0, :, 2 * A_WIDTH:3 * A_WIDTH].T
    pad = jnp.zeros((V_AUG - A_HEAD_DIM, D_MODEL), BF16)
    for h in range(A_HEADS):
        wvot_ref[h * V_AUG:h * V_AUG + A_HEAD_DIM, :] = vt[h * A_HEAD_DIM:(h + 1) * A_HEAD_DIM, :].astype(BF16)
        wvot_ref[h * V_AUG + A_HEAD_DIM:(h + 1) * V_AUG, :] = pad
    wvot_ref[A_HEADS * V_AUG:, :] = (0.5 * a_ref[0, :, 3 * A_WIDTH:4 * A_WIDTH]).T.astype(BF16)
    wk_ref[...] = kv_ref[:, :B_WIDTH].astype(BF16)
    wvtb_ref[...] = kv_ref[:, B_WIDTH:].T.astype(BF16)
    wqt_ref[...] = (B_HEAD_DIM ** -0.5 * b_ref[0, :, :B_WIDTH]).T.astype(BF16)
    wqmem_ref[...] = (MEM_HEAD_DIM ** -0.5 * b_ref[0, :, B_WIDTH:]).astype(BF16)


def _prep_weights(a_w_in, w_kv, b_w_in):
    assert math.log2(MEM_HEAD_DIM ** -0.5).is_integer() and math.log2(B_HEAD_DIM ** -0.5).is_integer()
    once = lambda shape: pl.BlockSpec(shape, lambda i: (0,) * len(shape), pipeline_mode=pl.Buffered(1))
    out_shapes = [(D_MODEL, 2 * A_WIDTH), (A_HEADS * V_AUG + A_WIDTH, D_MODEL), (D_MODEL, B_WIDTH),
                  (B_WIDTH, D_MODEL), (B_WIDTH, D_MODEL), (D_MODEL, MEM_WIDTH)]
    return pl.pallas_call(
        _prep_weights_kernel,
        grid=(1,),
        in_specs=[once((1, D_MODEL, 4 * A_WIDTH)), once(w_kv.shape), once(b_w_in.shape)],
        out_specs=[once(s) for s in out_shapes],
        out_shape=[jax.ShapeDtypeStruct(s, BF16) for s in out_shapes],
        compiler_params=pltpu.CompilerParams(dimension_semantics=("arbitrary",), vmem_limit_bytes=VMEM_LIMIT),
        name="prep_weights",
    )(a_w_in, w_kv, b_w_in)


def _causal_taps(x, prev, taps):
    n_taps = taps.shape[0]
    halo_row = lax.broadcasted_iota(jnp.int32, (HALO, x.shape[1]), 0)
    y = taps[n_taps - 1:n_taps, :] * x
    for back in range(1, n_taps):
        head = jnp.where(halo_row < back, pltpu.roll(prev, back, axis=0), pltpu.roll(x[:HALO, :], back, axis=0))
        shifted = jnp.concatenate([head, pltpu.roll(x, back, axis=0)[HALO:, :]], axis=0)
        y = y + taps[n_taps - 1 - back:n_taps - back, :] * shifted
    return y


def _lane_scan(x, op, identity):
    lane = lax.broadcasted_iota(jnp.int32, x.shape, 1)
    step = 1
    while step < x.shape[1]:
        x = op(x, jnp.where(lane >= step, pltpu.roll(x, step, axis=1), identity))
        step *= 2
    return x


def _a_in_kernel(x_ref, g_ref, wqk_ref, wvot_ref, wgm_ref, gate_b_ref, vone_ref, cw_ref, cb_ref,
                 q_ref, k_ref, vt_ref, ot_ref, rows_ref, acol_ref, decay_ref, qmem_ref, halo_ref, m_ref):
    L = L_MLSTM
    n_chunks = x_ref.shape[1] // L

    @pl.when(pl.program_id(1) == 0)
    def _():
        halo_ref[...] = jnp.zeros(halo_ref.shape, F32)
        m_ref[...] = jnp.zeros(m_ref.shape, F32)

    def projections(c):
        tok = slice(c * L, (c + 1) * L)
        hn = _rms(x_ref[0, tok, :], g_ref[...]).astype(BF16)

        gm = _dot(hn, wgm_ref[...])
        qmem_ref[0, tok, :] = gm[:, 2 * GATE_LANES:].astype(BF16)
        gi = (gm[:, :GATE_LANES] + gate_b_ref[:, :GATE_LANES]).T[0:SUBLANES, :]
        fg = (gm[:, GATE_LANES:2 * GATE_LANES] + gate_b_ref[:, GATE_LANES:]).T[0:SUBLANES, :]
        head_row = lax.broadcasted_iota(jnp.int32, (SUBLANES, L), 0) < A_HEADS
        logf = jnp.where(head_row, jnp.minimum(fg, 0.0) - jnp.log(1.0 + jnp.exp(-jnp.abs(fg))), 0.0)
        b = _lane_scan(logf, jnp.add, 0.0)
        a = gi - b
        m_prev = m_ref[:, 0:1]
        g = jnp.maximum(_lane_scan(a, jnp.maximum, -jnp.inf), m_prev)
        g_last = g[:, L - 1:L]
        m_ref[...] = jnp.broadcast_to(b[:, L - 1:L] + g_last, m_ref.shape)
        decay_ref[0, c] = jnp.broadcast_to(jnp.exp(m_prev - g_last), decay_ref.shape[2:])
        terms = [None] * GATE_TERMS
        terms[TERM_G], terms[TERM_INTER] = g, jnp.exp(m_prev - g)
        terms[TERM_EN], terms[TERM_WEXP] = jnp.exp(-(b + g)), jnp.exp(a - g_last)
        rows_ref[0, :, tok] = jnp.concatenate(terms, axis=0)
        acol_ref[0, tok, :] = jnp.concatenate([a, jnp.zeros((GATE_LANES - SUBLANES, L), F32)], axis=0).T

        raw = _dot(hn, wqk_ref[...])
        vo = _dot_nt(wvot_ref[...], hn)
        n_v = vt_ref.shape[2]
        vt_ref[0, c] = (vo[:n_v] + vone_ref[...]).astype(BF16)
        ot_ref[0, c] = vo[n_v:]
        return raw

    def conv(c, raw):
        tok = slice(c * L, (c + 1) * L)
        prev = halo_ref[...]
        halo_ref[...] = raw[L - HALO:, :]
        qk = _silu(_causal_taps(raw, prev, cw_ref[...]) + cb_ref[...])
        for h in range(A_HEADS):
            lo = h * A_HEAD_DIM
            q_ref[0, h, tok, :] = qk[:, lo:lo + A_HEAD_DIM].astype(BF16)
            k_ref[0, h, tok, :] = (qk[:, A_WIDTH + lo:A_WIDTH + lo + A_HEAD_DIM]
                                   * (A_HEAD_DIM ** -0.5)).astype(BF16)

    raws = {}
    for slot in range(n_chunks + 1):
        if slot < n_chunks:
            raws[slot] = projections(slot)
        if slot >= 1:
            conv(slot - 1, raws.pop(slot - 1))


def _a_in(x, g, wqk, wvot, wgm, gate_b, vone, cw, cb):
    bsz, seq, _ = x.shape
    tm = TM_IN
    head_spec = lambda width: pl.BlockSpec((1, A_HEADS, tm, width), lambda b, j: (b, 0, j, 0))
    row_spec = lambda width: pl.BlockSpec((1, tm, width), lambda b, j: (b, j, 0))
    col_spec = lambda width: pl.BlockSpec((1, width, tm), lambda b, j: (b, 0, j))
    tile_spec = lambda width: pl.BlockSpec((1, tm // L_MLSTM, width, L_MLSTM), lambda b, j: (b, j, 0, 0))
    weights = (g, wqk, wvot, wgm, gate_b, vone, cw, cb)
    return pl.pallas_call(
        _a_in_kernel,
        grid=(bsz, seq // tm),
        in_specs=[row_spec(D_MODEL)] + [_resident(w.shape) for w in weights],
        out_specs=[
            head_spec(A_HEAD_DIM), head_spec(A_HEAD_DIM), tile_spec(A_HEADS * V_AUG), tile_spec(A_WIDTH),
            col_spec(GATE_TERMS * SUBLANES),
            row_spec(GATE_LANES),
            pl.BlockSpec((1, tm // L_MLSTM, SUBLANES, GATE_LANES), lambda b, j: (b, j, 0, 0)),
            row_spec(MEM_WIDTH),
        ],
        out_shape=[
            jax.ShapeDtypeStruct((bsz, A_HEADS, seq, A_HEAD_DIM), BF16),
            jax.ShapeDtypeStruct((bsz, A_HEADS, seq, A_HEAD_DIM), BF16),
            jax.ShapeDtypeStruct((bsz, seq // L_MLSTM, A_HEADS * V_AUG, L_MLSTM), BF16),
            jax.ShapeDtypeStruct((bsz, seq // L_MLSTM, A_WIDTH, L_MLSTM), F32),
            jax.ShapeDtypeStruct((bsz, GATE_TERMS * SUBLANES, seq), F32),
            jax.ShapeDtypeStruct((bsz, seq, GATE_LANES), F32),
            jax.ShapeDtypeStruct((bsz, seq // L_MLSTM, SUBLANES, GATE_LANES), F32),
            jax.ShapeDtypeStruct((bsz, seq, MEM_WIDTH), BF16),
        ],
        scratch_shapes=[pltpu.VMEM((HALO, 2 * A_WIDTH), F32), pltpu.VMEM((SUBLANES, GATE_LANES), F32)],
        compiler_params=pltpu.CompilerParams(
            dimension_semantics=("arbitrary", "arbitrary"), vmem_limit_bytes=VMEM_LIMIT),
        name="a_in",
    )(x, *weights)


def _mlstm_kernel(q_ref, k_ref, vt_ref, ot_ref, rows_ref, acol_ref, decay_ref, hg_ref, out_ref, ct_ref):
    L = q_ref.shape[2]
    half = L // 2

    @pl.when(pl.program_id(1) == 0)
    def _():
        ct_ref[...] = jnp.zeros(ct_ref.shape, F32)

    upper = (lax.broadcasted_iota(jnp.int32, (half, half), 0)
             <= lax.broadcasted_iota(jnp.int32, (half, half), 1))
    zero_block = jnp.zeros((half, half), BF16)
    items = [(n, h) for n in range(q_ref.shape[0]) for h in range(A_HEADS)]
    live = [dict() for _ in items]

    def term(n, which, h):
        return rows_ref[n, SUBLANES * which + h:SUBLANES * which + h + 1, :]

    def scores(i):
        n, h = items[i]
        live[i]["st"] = _dot_nt(k_ref[n, h], q_ref[n, h])

    def decay_weight(i):
        n, h = items[i]
        st = live[i].pop("st")
        a_col = acol_ref[n, :, h:h + 1]
        g_row = term(n, TERM_G, h)
        p00 = jnp.exp(jnp.where(upper, a_col[:half] - g_row[:, :half], -jnp.inf))
        p01 = jnp.exp(a_col[:half] - g_row[:, half:])
        p11 = jnp.exp(jnp.where(upper, a_col[half:] - g_row[:, half:], -jnp.inf))
        live[i]["s"] = jnp.concatenate([
            jnp.concatenate([(st[:half, :half] * p00).astype(BF16), (st[:half, half:] * p01).astype(BF16)], axis=1),
            jnp.concatenate([zero_block, (st[half:, half:] * p11).astype(BF16)], axis=1)], axis=0)

    def numerator(i):
        n, h = items[i]
        vt = vt_ref[n, 0, h * V_AUG:(h + 1) * V_AUG, :]
        ct_prev = ct_ref[n, h]
        live[i]["num"] = (_dot(vt, live[i].pop("s"))
                          + _dot_nt(ct_prev.astype(BF16), q_ref[n, h]) * term(n, TERM_INTER, h))
        wvt = (vt.astype(F32) * term(n, TERM_WEXP, h)).astype(BF16)
        ct_ref[n, h] = decay_ref[n, 0, h:h + 1, 0:1] * ct_prev + _dot(wvt, k_ref[n, h])

    def normalise(i):
        n, h = items[i]
        numt = live[i].pop("num")
        den = numt[A_HEAD_DIM:A_HEAD_DIM + 1, :]
        r = 1.0 / jnp.maximum(jnp.abs(den), term(n, TERM_EN, h))
        body = numt[:A_HEAD_DIM, :]
        ssq = jnp.sum(body * body, axis=0, keepdims=True)
        factor = r * lax.rsqrt(r * r * ssq * (1.0 / A_HEAD_DIM) + EPS)
        lo = h * A_HEAD_DIM
        gate = hg_ref[h] + hg_ref[h] * jnp.tanh(ot_ref[n, 0, lo:lo + A_HEAD_DIM, :])
        out_ref[n, 0, lo:lo + A_HEAD_DIM, :] = (body * factor * gate).astype(BF16)

    stages = (scores, decay_weight, numerator, normalise)
    for slot in range(len(items) + len(stages) - 1):
        for depth, stage in enumerate(stages):
            if 0 <= slot - depth < len(items):
                stage(slot - depth)


def _mlstm(q, k, vt, ot_half, rows, a_cols, decay, head_g_half):
    bsz, nh, seq, dh = q.shape
    L = L_MLSTM
    nb = MLSTM_BATCH
    head_spec = pl.BlockSpec((nb, nh, L, dh), lambda b, j: (b, 0, j, 0))
    col_spec = lambda width: pl.BlockSpec((nb, width, L), lambda b, j: (b, 0, j))
    tile_spec = lambda width: pl.BlockSpec((nb, 1, width, L), lambda b, j: (b, j, 0, 0))
    return pl.pallas_call(
        _mlstm_kernel,
        grid=(bsz // nb, seq // L),
        in_specs=[
            head_spec, head_spec, tile_spec(nh * V_AUG), tile_spec(nh * dh), col_spec(rows.shape[1]),
            pl.BlockSpec((nb, L, GATE_LANES), lambda b, j: (b, j, 0)),
            pl.BlockSpec((nb, 1, SUBLANES, GATE_LANES), lambda b, j: (b, j, 0, 0)),
            _resident(head_g_half.shape),
        ],
        out_specs=tile_spec(nh * dh),
        out_shape=jax.ShapeDtypeStruct((bsz, seq // L, nh * dh, L), BF16),
        scratch_shapes=[pltpu.VMEM((nb, nh, V_AUG, dh), F32)],
        compiler_params=pltpu.CompilerParams(
            dimension_semantics=("arbitrary", "arbitrary"), vmem_limit_bytes=VMEM_LIMIT),
        name="mlstm",
    )(q, k, vt, ot_half, rows, a_cols, decay, head_g_half)


def _post_mixer_kernel(*refs, last):
    if last:
        (x_ref, mix_ref, qmem_ref, mkbdt_ref, mvbd_ref, wout_mix_ref, wout_mem_ref, ffn_g_ref, wu_ref, wg_ref,
         cw_ref, cb_ref, wd_ref, final_g_ref, out_ref, halo_ref, act_ref) = refs
    else:
        (x_ref, mix_ref, qmem_ref, mkbdt_ref, mvbd_ref, wout_mix_ref, wout_mem_ref, ffn_g_ref, wu_ref, wg_ref,
         cw_ref, cb_ref, wd_ref, kv_g_ref, wk_ref, wvt_ref, nxt_g_ref, wqt_ref, wqmem_ref,
         out_ref, k_ref, vt_ref, qnt_ref, qmemn_ref, halo_ref, act_ref) = refs
    tm = x_ref.shape[1]

    @pl.when(pl.program_id(1) == 0)
    def _():
        halo_ref[...] = jnp.zeros((HALO, D_FF), F32)

    s = _dot(qmem_ref[0], mkbdt_ref[0, 0])
    probs = []
    for h in range(MEM_HEADS):
        sh = s[:, h * MEM_TOKENS:(h + 1) * MEM_TOKENS]
        e = jnp.exp(sh - jnp.max(sh, axis=1, keepdims=True))
        probs.append((e * (1.0 / jnp.sum(e, axis=1, keepdims=True))).astype(BF16))
    mem_out = _dot(jnp.concatenate(probs, axis=1), mvbd_ref[0, 0])

    y = _dot(mem_out.astype(BF16), wout_mem_ref[...])
    y = y + jnp.concatenate([_dot_tn(mix_ref[0, i], wout_mix_ref[...]) for i in range(mix_ref.shape[1])], axis=0)
    x1 = x_ref[0] + y

    hn = _rms(x1, ffn_g_ref[...]).astype(BF16)
    chunk_cols = [slice(c0, min(c0 + FF_CHUNK, D_FF)) for c0 in range(0, D_FF, FF_CHUNK)]
    n_chunks = len(chunk_cols)

    def up(c):
        return _dot(hn, wu_ref[:, chunk_cols[c]]), _dot(hn, wg_ref[:, chunk_cols[c]])

    u, g = up(0)
    for c in range(n_chunks):
        cols = chunk_cols[c]
        nxt = up(c + 1) if c + 1 < n_chunks else None
        prev = halo_ref[:, cols]
        halo_ref[:, cols] = g[tm - HALO:, :]
        gc = _causal_taps(g, prev, cw_ref[:, cols]) + cb_ref[:, cols]
        act_ref[:, cols] = (_silu(gc) * u).astype(BF16)
        if nxt is not None:
            u, g = nxt
    x2 = x1 + _dot(act_ref[...], wd_ref[...])

    if last:
        out_ref[0] = _rms(x2, final_g_ref[...])
    else:
        out_ref[0] = x2
        hkv = _rms(x2, kv_g_ref[...]).astype(BF16)
        k_ref[0] = _dot(hkv, wk_ref[...]).astype(BF16)
        vt_ref[0] = _dot_nt(wvt_ref[...], hkv).astype(BF16)
        hq = _rms(x2, nxt_g_ref[...]).astype(BF16)
        qnt = _dot_nt(wqt_ref[...], hq).astype(BF16)
        for i in range(qnt_ref.shape[1]):
            qnt_ref[0, i] = qnt[:, i * TQ_BAND:(i + 1) * TQ_BAND]
        qmemn_ref[0] = _dot(hq, wqmem_ref[...]).astype(BF16)


def _post_mixer(x, mix, qmem, mkbdt, mvbd, layer, wout_mix, wout_mem, ffn_g, wu, wg, cw, cb, wd, tail, last):
    bsz, seq, _ = x.shape
    tm = TM_POST
    row_spec = lambda width: pl.BlockSpec((1, tm, width), lambda b, j: (b, j, 0))
    col_spec = lambda width: pl.BlockSpec((1, width, tm), lambda b, j: (b, 0, j))
    tile_spec = lambda width, tile: pl.BlockSpec((1, tm // tile, width, tile), lambda b, j: (b, j, 0, 0))
    mix_spec = tile_spec(mix.shape[2], mix.shape[3])
    weights = (wout_mix, wout_mem, ffn_g, wu, wg, cw, cb, wd) + tuple(tail)
    in_specs = [
        row_spec(D_MODEL), mix_spec, row_spec(MEM_WIDTH),
        pl.BlockSpec((1, 1) + mkbdt.shape[2:], lambda b, j: (layer, b, 0, 0)),
        pl.BlockSpec((1, 1) + mvbd.shape[2:], lambda b, j: (layer, b, 0, 0)),
    ] + [_resident(w.shape) for w in weights]
    if last:
        out_specs = row_spec(D_MODEL)
        out_shape = jax.ShapeDtypeStruct((bsz, seq, D_MODEL), F32)
    else:
        out_specs = [row_spec(D_MODEL), row_spec(B_WIDTH), col_spec(B_WIDTH), tile_spec(B_WIDTH, TQ_BAND),
                     row_spec(MEM_WIDTH)]
        out_shape = [
            jax.ShapeDtypeStruct((bsz, seq, D_MODEL), F32),
            jax.ShapeDtypeStruct((bsz, seq, B_WIDTH), BF16),
            jax.ShapeDtypeStruct((bsz, B_WIDTH, seq), BF16),
            jax.ShapeDtypeStruct((bsz, seq // TQ_BAND, B_WIDTH, TQ_BAND), BF16),
            jax.ShapeDtypeStruct((bsz, seq, MEM_WIDTH), BF16),
        ]
    return pl.pallas_call(
        functools.partial(_post_mixer_kernel, last=last),
        grid=(bsz, seq // tm),
        in_specs=in_specs,
        out_specs=out_specs,
        out_shape=out_shape,
        scratch_shapes=[pltpu.VMEM((HALO, D_FF), F32), pltpu.VMEM((tm, D_FF), BF16)],
        compiler_params=pltpu.CompilerParams(
            dimension_semantics=("arbitrary", "arbitrary"), vmem_limit_bytes=VMEM_LIMIT),
        name="post_mixer_last" if last else "post_mixer",
    )(x, mix, qmem, mkbdt, mvbd, *weights)


def _band_bias_kernel(tbl_ref, out_ref):
    h = pl.program_id(0)
    b = lax.broadcasted_iota(jnp.int32, (SUBLANES, BIAS_LANES), 0)
    c = lax.broadcasted_iota(jnp.int32, (SUBLANES, BIAS_LANES), 1)
    r_signed = jnp.where(c < TQ_HALF, c, c - BIAS_LANES)
    idx = jnp.clip(BAND_PREV + r_signed - b, -(CHUNK - 1), MAX_REL) + (CHUNK - 1)

    def body(e, acc):
        return jnp.where(idx == e, tbl_ref[h, e], acc)

    base = lax.fori_loop(0, REL_SIZE, body, jnp.zeros((SUBLANES, BIAS_LANES), F32), unroll=8)
    kb = lax.broadcasted_iota(jnp.int32, (SUBLANES, TQ_HALF), 0)
    r = lax.broadcasted_iota(jnp.int32, (SUBLANES, TQ_HALF), 1)
    for k0 in range(0, BIAS_ROWS, SUBLANES):
        first = -k0 % BIAS_LANES
        lo = first // TQ_HALF * TQ_HALF
        hi = (lo + TQ_HALF) % BIAS_LANES
        window = jnp.concatenate([base[:, lo:lo + TQ_HALF], base[:, hi:hi + TQ_HALF]], axis=1)
        if first != lo:
            window = pltpu.roll(window, 2 * TQ_HALF - (first - lo), axis=1)
        dchunk = (BAND_PREV + r) // CHUNK - (k0 + kb) // CHUNK
        valid = (dchunk >= 0) & (dchunk < BAND_CHUNKS)
        out_ref[0, k0:k0 + SUBLANES, :] = jnp.where(valid, window[:, :TQ_HALF], NEG_BIG)


def _band_bias(rel_table):
    return pl.pallas_call(
        _band_bias_kernel,
        grid=(B_HEADS,),
        in_specs=[pl.BlockSpec(memory_space=pltpu.SMEM)],
        out_specs=pl.BlockSpec((1, BIAS_ROWS, TQ_HALF), lambda h: (h, 0, 0)),
        out_shape=jax.ShapeDtypeStruct((B_HEADS, BIAS_ROWS, TQ_HALF), F32),
        compiler_params=pltpu.CompilerParams(dimension_semantics=("arbitrary",)),
        name="band_bias",
    )(rel_table)


def _band_attn_kernel(qt_ref, k_ref, vt_ref, bias_ref, out_ref):
    j = pl.program_id(1)
    first_half = 2 * j - HALVES_BACK
    start = pl.multiple_of(jnp.maximum(first_half, 0) * TQ_HALF, TQ_HALF)
    pair_width = 2 * B_HEAD_DIM
    pair_row = lax.broadcasted_iota(jnp.int32, (pair_width, TQ_BAND), 0)
    zeros_half = jnp.zeros((TQ_HALF, TQ_HALF), BF16)
    masked_half = jnp.full((TQ_HALF, TQ_HALF), NEG_BIG, F32)

    def scores(pair):
        lo = pair * pair_width
        k_pair = k_ref[0, pl.ds(start, TK_BAND), lo:lo + pair_width]
        q_pair = qt_ref[0, 0, lo:lo + pair_width, :]
        zero = jnp.zeros_like(q_pair)
        q_sel = jnp.concatenate([jnp.where(pair_row < B_HEAD_DIM, q_pair, zero),
                                 jnp.where(pair_row >= B_HEAD_DIM, q_pair, zero)], axis=1)
        return _dot(k_pair, q_sel)

    def softmax_half(t, row0, c0):
        blocks = [t[row0 + r:row0 + r + SOFTMAX_ROWS, c0:c0 + TQ_HALF] for r in range(0, TK_HALF, SOFTMAX_ROWS)]
        m_acc = blocks[0]
        for blk in blocks[1:]:
            m_acc = jnp.maximum(m_acc, blk)
        m = jnp.max(m_acc, axis=0, keepdims=True)
        ps = []
        l_acc = None
        for blk in blocks:
            e = jnp.exp(blk - m)
            ps.append(e.astype(BF16))
            l_acc = e if l_acc is None else l_acc + e
        return jnp.concatenate(ps, axis=0), jnp.sum(l_acc, axis=0, keepdims=True)

    def tile(off_b, bias_row_a, bias_row_b):
        def bias_tile(pair):
            cols = []
            for h in (2 * pair, 2 * pair + 1):
                bias_a = bias_ref[h, pl.ds(bias_row_a, TK_HALF), :]
                bias_b = bias_ref[h, pl.ds(bias_row_b, TK_HALF), :]
                cols.append(jnp.concatenate([bias_a, masked_half], axis=0))
                cols.append(jnp.concatenate([masked_half, bias_b] if off_b else [bias_b, masked_half], axis=0))
            return jnp.concatenate(cols, axis=1)

        n_pairs = B_HEADS // 2
        live = [dict() for _ in range(n_pairs)]

        def biased_scores(pair):
            live[pair]["t"] = scores(pair) + bias_tile(pair)

        def softmax(pair):
            t = live[pair].pop("t")
            cols, sums = [], []
            for parity in range(2):
                c0 = parity * TQ_BAND
                p_a, l_a = softmax_half(t, 0, c0)
                p_b, l_b = softmax_half(t, off_b, c0 + TQ_HALF)
                cols.append(jnp.concatenate([p_a, zeros_half], axis=0))
                cols.append(jnp.concatenate([zeros_half, p_b] if off_b else [p_b, zeros_half], axis=0))
                sums += [l_a, l_b]
            live[pair]["p"] = jnp.concatenate(cols, axis=1)
            live[pair]["inv"] = 1.0 / jnp.concatenate(sums, axis=1)

        def values(pair):
            lo = pair * pair_width
            vt = vt_ref[0, lo:lo + pair_width, pl.ds(start, TK_BAND)]
            o = _dot(vt, live[pair].pop("p"))
            inv = live[pair].pop("inv")
            out_ref[0, 0, lo:lo + B_HEAD_DIM, :] = (o[:B_HEAD_DIM, :TQ_BAND] * inv[:, :TQ_BAND]).astype(BF16)
            out_ref[0, 0, lo + B_HEAD_DIM:lo + pair_width, :] = (
                o[B_HEAD_DIM:, TQ_BAND:] * inv[:, TQ_BAND:]).astype(BF16)

        stages = (biased_scores, softmax, values)
        for slot in range(n_pairs + len(stages) - 1):
            for depth in (0, 2, 1):
                if 0 <= slot - depth < n_pairs:
                    stages[depth](slot - depth)

    pl.when(first_half >= 0)(functools.partial(tile, TQ_HALF, 0, 0))

    @pl.when(first_half < 0)
    def _():
        row_a = pl.multiple_of(-first_half * TQ_HALF, TQ_HALF)
        tile(0, row_a, pl.multiple_of(row_a - TQ_HALF, TQ_HALF))


def _band_attn(qt, k, vt, bias):
    bsz, seq, _ = k.shape
    return pl.pallas_call(
        _band_attn_kernel,
        grid=(bsz, seq // TQ_BAND),
        in_specs=[
            pl.BlockSpec((1, 1, B_WIDTH, TQ_BAND), lambda b, j: (b, j, 0, 0)),
            pl.BlockSpec((1, seq, B_WIDTH), lambda b, j: (b, 0, 0)),
            pl.BlockSpec((1, B_WIDTH, seq), lambda b, j: (b, 0, 0)),
            _resident(bias.shape),
        ],
        out_specs=pl.BlockSpec((1, 1, B_WIDTH, TQ_BAND), lambda b, j: (b, j, 0, 0)),
        out_shape=jax.ShapeDtypeStruct((bsz, seq // TQ_BAND, B_WIDTH, TQ_BAND), BF16),
        compiler_params=pltpu.CompilerParams(
            dimension_semantics=("arbitrary", "arbitrary"), vmem_limit_bytes=VMEM_LIMIT),
        name="band_attn",
    )(qt, k, vt, bias)


def kernel(x, mem, norm_mix_g, norm_ffn_g, a_w_in, a_gate_b, a_conv_w, a_conv_b, a_head_g, a_w_out, kv_norm_g, w_kv,
           b_w_in, b_rel_bias, b_w_out, mem_w_kv, ffn_w_up, ffn_conv_w, ffn_conv_b, ffn_w_down, final_g):
    bsz, seq, d = x.shape
    assert d == D_MODEL and seq % max(TM_IN, L_MLSTM, TM_POST, TQ_BAND) == 0
    assert a_w_in.shape[0] == 1 and b_w_in.shape[0] == 1, "one mLSTM layer followed by one band-attention layer"
    assert TM_IN % L_MLSTM == 0, "the input projection emits per-chunk decay terms: whole chunks per tile"
    assert bsz % MLSTM_BATCH == 0
    row = lambda g: g.reshape(1, -1).astype(F32)

    wkt = jnp.swapaxes(mem_w_kv[:, :, :MEM_WIDTH], 1, 2).astype(BF16)
    wmv = mem_w_kv[:, :, MEM_WIDTH:].astype(BF16)
    mkbdt, mvbd = _mem_kv(mem.astype(BF16), wkt, wmv)

    wqk, wvot, wk_b, wvt_b, wqt_b, wqmem_b = _prep_weights(a_w_in, w_kv, b_w_in)

    w = a_w_in[0]
    gate_pad = ((0, 0), (0, GATE_LANES - A_HEADS))
    g0 = 4 * A_WIDTH
    wgm = jnp.concatenate([jnp.pad(w[:, g0:g0 + A_HEADS], gate_pad), jnp.pad(w[:, g0 + A_HEADS:g0 + 2 * A_HEADS], gate_pad),
                           MEM_HEAD_DIM ** -0.5 * w[:, g0 + 2 * A_HEADS:]], axis=1).astype(BF16)
    gate_b = jnp.pad(a_gate_b[0].astype(F32).reshape(2, A_HEADS), gate_pad).reshape(1, 2 * GATE_LANES)
    vone = (jnp.arange(A_HEADS * V_AUG) % V_AUG == A_HEAD_DIM).astype(F32).reshape(A_HEADS * V_AUG, 1)
    q, k, vt, ot_half, gate_rows, a_cols, decay, qmem = _a_in(
        x, row(norm_mix_g[0]), wqk, wvot, wgm, gate_b, vone, a_conv_w[0].astype(F32), row(a_conv_b[0]))

    head_g_half = 0.5 * a_head_g[0].reshape(A_HEADS, A_HEAD_DIM, 1).astype(F32)
    mix = _mlstm(q, k, vt, ot_half, gate_rows, a_cols, decay, head_g_half)

    def ffn_weights(l):
        return (row(norm_ffn_g[l]), ffn_w_up[l][:, :D_FF].astype(BF16), ffn_w_up[l][:, D_FF:].astype(BF16),
                ffn_conv_w[l].astype(F32), row(ffn_conv_b[l]), ffn_w_down[l].astype(BF16))

    wout = a_w_out[0]
    tail = (row(kv_norm_g), wk_b, wvt_b, row(norm_mix_g[1]), wqt_b, wqmem_b)
    x1, kb, vbt, qbt, qmem_b = _post_mixer(
        x, mix, qmem, mkbdt, mvbd, 0,
        wout[:A_WIDTH].astype(BF16), wout[A_WIDTH:].astype(BF16),
        *ffn_weights(0), tail=tail, last=False)

    bias = _band_bias(b_rel_bias[0].astype(F32))
    mix_b = _band_attn(qbt, kb, vbt, bias)
    wout = b_w_out[0]
    return _post_mixer(
        x1, mix_b, qmem_b, mkbdt, mvbd, 1,
        wout[:B_WIDTH].astype(BF16), wout[B_WIDTH:].astype(BF16),
        *ffn_weights(1), tail=(row(final_g),), last=True)
```

```python
import functools
import math

import jax
import jax.numpy as jnp
from jax import lax
from jax.experimental import pallas as pl
from jax.experimental.pallas import tpu as pltpu

F32 = jnp.float32
BF16 = jnp.bfloat16

LANES = 128
SUBLANES = 8
BF16_SUBLANES = 16
VMEM_BYTES = 64 * 1024 * 1024
VMEM_LIMIT = VMEM_BYTES // 8 * 7

D_MODEL = 1024
CHUNK = 64
MEM_TOKENS = 256
MEM_HEADS = 4
MEM_WIDTH = 256
MEM_HEAD_DIM = MEM_WIDTH // MEM_HEADS
A_WIDTH = 768
A_HEADS = 4
A_HEAD_DIM = A_WIDTH // A_HEADS
A_CONV = 4
B_HEADS = 12
B_HEAD_DIM = 64
B_WIDTH = B_HEADS * B_HEAD_DIM
BAND_CHUNKS = 9
MAX_REL = 128
REL_SIZE = MAX_REL + CHUNK
D_FF = 2816
FFN_CONV = 3
EPS = 1e-6

V_AUG = -(-(A_HEAD_DIM + 1) // BF16_SUBLANES) * BF16_SUBLANES
GATE_LANES = LANES
GATE_TERMS = 4
TERM_G, TERM_INTER, TERM_EN, TERM_WEXP = range(GATE_TERMS)
HALO = SUBLANES
NEG_BIG = -1e30

TM_IN = 512
L_MLSTM = 256
MLSTM_BATCH = 4
TM_POST = 512
FF_CHUNK = 256
TQ_HALF = LANES
TQ_BAND = 2 * TQ_HALF
BAND_PREV = (BAND_CHUNKS - 1) * CHUNK
TK_HALF = BAND_PREV + TQ_HALF
TK_BAND = BAND_PREV + TQ_BAND
HALVES_BACK = BAND_PREV // TQ_HALF
BIAS_ROWS = TK_HALF + BAND_PREV
BIAS_LANES = BIAS_ROWS + TQ_HALF
SOFTMAX_ROWS = 8 * SUBLANES


def _resident(shape):
    nd = len(shape)
    return pl.BlockSpec(shape, lambda *_: (0,) * nd, pipeline_mode=pl.Buffered(1))


def _rms(x, g):
    return x * lax.rsqrt(jnp.mean(x * x, axis=-1, keepdims=True) + EPS) * g


def _dot(a, b):
    return jnp.dot(a, b, preferred_element_type=F32)


def _dot_nt(a, b):
    return lax.dot_general(a, b, (((1,), (1,)), ((), ())), preferred_element_type=F32)


def _dot_tn(a, b):
    return lax.dot_general(a, b, (((0,), (0,)), ((), ())), preferred_element_type=F32)


def _silu(x):
    half = 0.5 * x
    return half + half * jnp.tanh(half)


def _mem_kv_kernel(mem_ref, w_ref, mkbdt_ref, mvbd_ref):
    m = mem_ref[0].astype(BF16)
    w = w_ref[0].astype(BF16)
    mkt = _dot(m, w[:, :MEM_WIDTH]).T
    mv = _dot(m, w[:, MEM_WIDTH:])
    f_idx = lax.broadcasted_iota(jnp.int32, (MEM_WIDTH, MEM_HEADS * MEM_TOKENS), 0)
    c_idx = lax.broadcasted_iota(jnp.int32, (MEM_WIDTH, MEM_HEADS * MEM_TOKENS), 1)
    mkt4 = jnp.concatenate([mkt] * MEM_HEADS, axis=1)
    mkbdt_ref[0, 0] = jnp.where(f_idx // MEM_HEAD_DIM == c_idx // MEM_TOKENS, mkt4, 0.0).astype(BF16)
    r_idx = lax.broadcasted_iota(jnp.int32, (MEM_HEADS * MEM_TOKENS, MEM_WIDTH), 0)
    g_idx = lax.broadcasted_iota(jnp.int32, (MEM_HEADS * MEM_TOKENS, MEM_WIDTH), 1)
    mv4 = jnp.concatenate([mv] * MEM_HEADS, axis=0)
    mvbd_ref[0, 0] = jnp.where(r_idx // MEM_TOKENS == g_idx // MEM_HEAD_DIM, mv4, 0.0).astype(BF16)


def _mem_kv(mem, mem_w_kv):
    depth = mem_w_kv.shape[0]
    bsz = mem.shape[0]
    return pl.pallas_call(
        _mem_kv_kernel,
        grid=(depth, bsz),
        in_specs=[
            pl.BlockSpec((1, MEM_TOKENS, D_MODEL), lambda l, b: (b, 0, 0)),
            pl.BlockSpec((1, D_MODEL, 2 * MEM_WIDTH), lambda l, b: (l, 0, 0)),
        ],
        out_specs=[
            pl.BlockSpec((1, 1, MEM_WIDTH, MEM_HEADS * MEM_TOKENS), lambda l, b: (l, b, 0, 0)),
            pl.BlockSpec((1, 1, MEM_HEADS * MEM_TOKENS, MEM_WIDTH), lambda l, b: (l, b, 0, 0)),
        ],
        out_shape=[
            jax.ShapeDtypeStruct((depth, bsz, MEM_WIDTH, MEM_HEADS * MEM_TOKENS), BF16),
            jax.ShapeDtypeStruct((depth, bsz, MEM_HEADS * MEM_TOKENS, MEM_WIDTH), BF16),
        ],
        compiler_params=pltpu.CompilerParams(dimension_semantics=("arbitrary", "arbitrary")),
        name="mem_kv",
    )(mem, mem_w_kv)


def _prep_weights_kernel(a_ref, kv_ref, b_ref, wqk_ref, wvot_ref, wk_ref, wvtb_ref, wqt_ref, wqmem_ref):
    wqk_ref[...] = a_ref[0, :, :2 * A_WIDTH].astype(BF16)
    vt = a_ref[0, :, 2 * A_WIDTH:3 * A_WIDTH].T
    pad = jnp.zeros((V_AUG - A_HEAD_DIM, D_MODEL), BF16)
    for h in range(A_HEADS):
        wvot_ref[h * V_AUG:h * V_AUG + A_HEAD_DIM, :] = vt[h * A_HEAD_DIM:(h + 1) * A_HEAD_DIM, :].astype(BF16)
        wvot_ref[h * V_AUG + A_HEAD_DIM:(h + 1) * V_AUG, :] = pad
    wvot_ref[A_HEADS * V_AUG:, :] = (0.5 * a_ref[0, :, 3 * A_WIDTH:4 * A_WIDTH]).T.astype(BF16)
    wk_ref[...] = kv_ref[:, :B_WIDTH].astype(BF16)
    wvtb_ref[...] = kv_ref[:, B_WIDTH:].T.astype(BF16)
    wqt_ref[...] = (B_HEAD_DIM ** -0.5 * b_ref[0, :, :B_WIDTH]).T.astype(BF16)
    wqmem_ref[...] = (MEM_HEAD_DIM ** -0.5 * b_ref[0, :, B_WIDTH:]).astype(BF16)


def _prep_weights(a_w_in, w_kv, b_w_in):
    assert math.log2(MEM_HEAD_DIM ** -0.5).is_integer() and math.log2(B_HEAD_DIM ** -0.5).is_integer()
    once = lambda shape: pl.BlockSpec(shape, lambda i: (0,) * len(shape), pipeline_mode=pl.Buffered(1))
    out_shapes = [(D_MODEL, 2 * A_WIDTH), (A_HEADS * V_AUG + A_WIDTH, D_MODEL), (D_MODEL, B_WIDTH),
                  (B_WIDTH, D_MODEL), (B_WIDTH, D_MODEL), (D_MODEL, MEM_WIDTH)]
    return pl.pallas_call(
        _prep_weights_kernel,
        grid=(1,),
        in_specs=[once((1, D_MODEL, 4 * A_WIDTH)), once(w_kv.shape), once(b_w_in.shape)],
        out_specs=[once(s) for s in out_shapes],
        out_shape=[jax.ShapeDtypeStruct(s, BF16) for s in out_shapes],
        compiler_params=pltpu.CompilerParams(dimension_semantics=("arbitrary",), vmem_limit_bytes=VMEM_LIMIT),
        name="prep_weights",
    )(a_w_in, w_kv, b_w_in)


def _causal_taps(x, prev, taps):
    n_taps = taps.shape[0]
    halo_row = lax.broadcasted_iota(jnp.int32, (HALO, x.shape[1]), 0)
    y = taps[n_taps - 1:n_taps, :] * x
    for back in range(1, n_taps):
        head = jnp.where(halo_row < back, pltpu.roll(prev, back, axis=0), pltpu.roll(x[:HALO, :], back, axis=0))
        shifted = jnp.concatenate([head, pltpu.roll(x, back, axis=0)[HALO:, :]], axis=0)
        y = y + taps[n_taps - 1 - back:n_taps - back, :] * shifted
    return y


def _lane_scan(x, op, identity):
    lane = lax.broadcasted_iota(jnp.int32, x.shape, 1)
    step = 1
    while step < x.shape[1]:
        x = op(x, jnp.where(lane >= step, pltpu.roll(x, step, axis=1), identity))
        step *= 2
    return x


def _a_in_kernel(x_ref, g_ref, wqk_ref, wvot_ref, wgm_ref, gate_b_ref, vone_ref, cw_ref, cb_ref,
                 q_ref, k_ref, vt_ref, ot_ref, rows_ref, acol_ref, decay_ref, qmem_ref, halo_ref, m_ref):
    L = L_MLSTM
    n_chunks = x_ref.shape[1] // L

    @pl.when(pl.program_id(1) == 0)
    def _():
        halo_ref[...] = jnp.zeros(halo_ref.shape, F32)
        m_ref[...] = jnp.zeros(m_ref.shape, F32)

    def projections(c):
        tok = slice(c * L, (c + 1) * L)
        hn = _rms(x_ref[0, tok, :], g_ref[...]).astype(BF16)

        gm = _dot(hn, wgm_ref[...])
        qmem_ref[0, tok, :] = gm[:, 2 * GATE_LANES:].astype(BF16)
        gi = (gm[:, :GATE_LANES] + gate_b_ref[:, :GATE_LANES]).T[0:SUBLANES, :]
        fg = (gm[:, GATE_LANES:2 * GATE_LANES] + gate_b_ref[:, GATE_LANES:]).T[0:SUBLANES, :]
        head_row = lax.broadcasted_iota(jnp.int32, (SUBLANES, L), 0) < A_HEADS
        logf = jnp.where(head_row, jnp.minimum(fg, 0.0) - jnp.log(1.0 + jnp.exp(-jnp.abs(fg))), 0.0)
        b = _lane_scan(logf, jnp.add, 0.0)
        a = gi - b
        m_prev = m_ref[:, 0:1]
        g = jnp.maximum(_lane_scan(a, jnp.maximum, -jnp.inf), m_prev)
        g_last = g[:, L - 1:L]
        m_ref[...] = jnp.broadcast_to(b[:, L - 1:L] + g_last, m_ref.shape)
        decay_ref[0, c] = jnp.broadcast_to(jnp.exp(m_prev - g_last), decay_ref.shape[2:])
        terms = [None] * GATE_TERMS
        terms[TERM_G], terms[TERM_INTER] = g, jnp.exp(m_prev - g)
        terms[TERM_EN], terms[TERM_WEXP] = jnp.exp(-(b + g)), jnp.exp(a - g_last)
        rows_ref[0, :, tok] = jnp.concatenate(terms, axis=0)
        acol_ref[0, tok, :] = jnp.concatenate([a, jnp.zeros((GATE_LANES - SUBLANES, L), F32)], axis=0).T

        raw = _dot(hn, wqk_ref[...])
        vo = _dot_nt(wvot_ref[...], hn)
        n_v = vt_ref.shape[2]
        vt_ref[0, c] = (vo[:n_v] + vone_ref[...]).astype(BF16)
        ot_ref[0, c] = vo[n_v:]
        return raw

    def conv(c, raw):
        tok = slice(c * L, (c + 1) * L)
        prev = halo_ref[...]
        halo_ref[...] = raw[L - HALO:, :]
        qk = _silu(_causal_taps(raw, prev, cw_ref[...]) + cb_ref[...])
        for h in range(A_HEADS):
            lo = h * A_HEAD_DIM
            q_ref[0, h, tok, :] = qk[:, lo:lo + A_HEAD_DIM].astype(BF16)
            k_ref[0, h, tok, :] = (qk[:, A_WIDTH + lo:A_WIDTH + lo + A_HEAD_DIM]
                                   * (A_HEAD_DIM ** -0.5)).astype(BF16)

    raws = {}
    for slot in range(n_chunks + 1):
        if slot < n_chunks:
            raws[slot] = projections(slot)
        if slot >= 1:
            conv(slot - 1, raws.pop(slot - 1))


def _a_in(x, g, wqk, wvot, wgm, gate_b, vone, cw, cb):
    bsz, seq, _ = x.shape
    tm = TM_IN
    head_spec = lambda width: pl.BlockSpec((1, A_HEADS, tm, width), lambda b, j: (b, 0, j, 0))
    row_spec = lambda width: pl.BlockSpec((1, tm, width), lambda b, j: (b, j, 0))
    col_spec = lambda width: pl.BlockSpec((1, width, tm), lambda b, j: (b, 0, j))
    tile_spec = lambda width: pl.BlockSpec((1, tm // L_MLSTM, width, L_MLSTM), lambda b, j: (b, j, 0, 0))
    weights = (g, wqk, wvot, wgm, gate_b, vone, cw, cb)
    return pl.pallas_call(
        _a_in_kernel,
        grid=(bsz, seq // tm),
        in_specs=[row_spec(D_MODEL)] + [_resident(w.shape) for w in weights],
        out_specs=[
            head_spec(A_HEAD_DIM), head_spec(A_HEAD_DIM), tile_spec(A_HEADS * V_AUG), tile_spec(A_WIDTH),
            col_spec(GATE_TERMS * SUBLANES),
            row_spec(GATE_LANES),
            pl.BlockSpec((1, tm // L_MLSTM, SUBLANES, GATE_LANES), lambda b, j: (b, j, 0, 0)),
            row_spec(MEM_WIDTH),
        ],
        out_shape=[
            jax.ShapeDtypeStruct((bsz, A_HEADS, seq, A_HEAD_DIM), BF16),
            jax.ShapeDtypeStruct((bsz, A_HEADS, seq, A_HEAD_DIM), BF16),
            jax.ShapeDtypeStruct((bsz, seq // L_MLSTM, A_HEADS * V_AUG, L_MLSTM), BF16),
            jax.ShapeDtypeStruct((bsz, seq // L_MLSTM, A_WIDTH, L_MLSTM), F32),
            jax.ShapeDtypeStruct((bsz, GATE_TERMS * SUBLANES, seq), F32),
            jax.ShapeDtypeStruct((bsz, seq, GATE_LANES), F32),
            jax.ShapeDtypeStruct((bsz, seq // L_MLSTM, SUBLANES, GATE_LANES), F32),
            jax.ShapeDtypeStruct((bsz, seq, MEM_WIDTH), BF16),
        ],
        scratch_shapes=[pltpu.VMEM((HALO, 2 * A_WIDTH), F32), pltpu.VMEM((SUBLANES, GATE_LANES), F32)],
        compiler_params=pltpu.CompilerParams(
            dimension_semantics=("arbitrary", "arbitrary"), vmem_limit_bytes=VMEM_LIMIT),
        name="a_in",
    )(x, *weights)


def _mlstm_kernel(q_ref, k_ref, vt_ref, ot_ref, rows_ref, acol_ref, decay_ref, hg_ref, out_ref, ct_ref):
    L = q_ref.shape[2]
    half = L // 2

    @pl.when(pl.program_id(1) == 0)
    def _():
        ct_ref[...] = jnp.zeros(ct_ref.shape, F32)

    upper = (lax.broadcasted_iota(jnp.int32, (half, half), 0)
             <= lax.broadcasted_iota(jnp.int32, (half, half), 1))
    zero_block = jnp.zeros((half, half), BF16)
    items = [(n, h) for n in range(q_ref.shape[0]) for h in range(A_HEADS)]
    live = [dict() for _ in items]

    def term(n, which, h):
        return rows_ref[n, SUBLANES * which + h:SUBLANES * which + h + 1, :]

    def scores(i):
        n, h = items[i]
        live[i]["st"] = _dot_nt(k_ref[n, h], q_ref[n, h])

    def decay_weight(i):
        n, h = items[i]
        st = live[i].pop("st")
        a_col = acol_ref[n, :, h:h + 1]
        g_row = term(n, TERM_G, h)
        p00 = jnp.exp(jnp.where(upper, a_col[:half] - g_row[:, :half], -jnp.inf))
        p01 = jnp.exp(a_col[:half] - g_row[:, half:])
        p11 = jnp.exp(jnp.where(upper, a_col[half:] - g_row[:, half:], -jnp.inf))
        live[i]["s"] = jnp.concatenate([
            jnp.concatenate([(st[:half, :half] * p00).astype(BF16), (st[:half, half:] * p01).astype(BF16)], axis=1),
            jnp.concatenate([zero_block, (st[half:, half:] * p11).astype(BF16)], axis=1)], axis=0)

    def numerator(i):
        n, h = items[i]
        vt = vt_ref[n, 0, h * V_AUG:(h + 1) * V_AUG, :]
        ct_prev = ct_ref[n, h]
        live[i]["num"] = (_dot(vt, live[i].pop("s"))
                          + _dot_nt(ct_prev.astype(BF16), q_ref[n, h]) * term(n, TERM_INTER, h))
        wvt = (vt.astype(F32) * term(n, TERM_WEXP, h)).astype(BF16)
        ct_ref[n, h] = decay_ref[n, 0, h:h + 1, 0:1] * ct_prev + _dot(wvt, k_ref[n, h])

    def normalise(i):
        n, h = items[i]
        numt = live[i].pop("num")
        den = numt[A_HEAD_DIM:A_HEAD_DIM + 1, :]
        r = 1.0 / jnp.maximum(jnp.abs(den), term(n, TERM_EN, h))
        body = numt[:A_HEAD_DIM, :]
        ssq = jnp.sum(body * body, axis=0, keepdims=True)
        factor = r * lax.rsqrt(r * r * ssq * (1.0 / A_HEAD_DIM) + EPS)
        lo = h * A_HEAD_DIM
        gate = hg_ref[h] + hg_ref[h] * jnp.tanh(ot_ref[n, 0, lo:lo + A_HEAD_DIM, :])
        out_ref[n, 0, lo:lo + A_HEAD_DIM, :] = (body * factor * gate).astype(BF16)

    stages = (scores, decay_weight, numerator, normalise)
    for slot in range(len(items) + len(stages) - 1):
        for depth, stage in enumerate(stages):
            if 0 <= slot - depth < len(items):
                stage(slot - depth)


def _mlstm(q, k, vt, ot_half, rows, a_cols, decay, head_g_half):
    bsz, nh, seq, dh = q.shape
    L = L_MLSTM
    nb = MLSTM_BATCH
    head_spec = pl.BlockSpec((nb, nh, L, dh), lambda b, j: (b, 0, j, 0))
    col_spec = lambda width: pl.BlockSpec((nb, width, L), lambda b, j: (b, 0, j))
    tile_spec = lambda width: pl.BlockSpec((nb, 1, width, L), lambda b, j: (b, j, 0, 0))
    return pl.pallas_call(
        _mlstm_kernel,
        grid=(bsz // nb, seq // L),
        in_specs=[
            head_spec, head_spec, tile_spec(nh * V_AUG), tile_spec(nh * dh), col_spec(rows.shape[1]),
            pl.BlockSpec((nb, L, GATE_LANES), lambda b, j: (b, j, 0)),
            pl.BlockSpec((nb, 1, SUBLANES, GATE_LANES), lambda b, j: (b, j, 0, 0)),
            _resident(head_g_half.shape),
        ],
        out_specs=tile_spec(nh * dh),
        out_shape=jax.ShapeDtypeStruct((bsz, seq // L, nh * dh, L), BF16),
        scratch_shapes=[pltpu.VMEM((nb, nh, V_AUG, dh), F32)],
        compiler_params=pltpu.CompilerParams(
            dimension_semantics=("arbitrary", "arbitrary"), vmem_limit_bytes=VMEM_LIMIT),
        name="mlstm",
    )(q, k, vt, ot_half, rows, a_cols, decay, head_g_half)


def _post_mixer_kernel(*refs, last):
    if last:
        (x_ref, mix_ref, qmem_ref, mkbdt_ref, mvbd_ref, wup_ref, wd_ref, wout_ref, ffn_g_ref, cw_ref, cb_ref,
         final_g_ref, out_ref, halo_ref, act_ref) = refs
    else:
        (x_ref, mix_ref, qmem_ref, mkbdt_ref, mvbd_ref, wup_ref, wd_ref, wout_ref, ffn_g_ref, cw_ref, cb_ref,
         kv_g_ref, wk_ref, wvt_ref, nxt_g_ref, wqt_ref, wqmem_ref,
         out_ref, k_ref, vt_ref, qnt_ref, qmemn_ref, halo_ref, act_ref) = refs
    tm = x_ref.shape[1]
    mix_width = mix_ref.shape[2]

    @pl.when(pl.program_id(1) == 0)
    def _():
        halo_ref[...] = jnp.zeros((HALO, D_FF), F32)

    s = _dot(qmem_ref[0], mkbdt_ref[0, 0])
    probs = []
    for h in range(MEM_HEADS):
        sh = s[:, h * MEM_TOKENS:(h + 1) * MEM_TOKENS]
        e = jnp.exp(sh - jnp.max(sh, axis=1, keepdims=True))
        probs.append((e * (1.0 / jnp.sum(e, axis=1, keepdims=True))).astype(BF16))
    mem_out = _dot(jnp.concatenate(probs, axis=1), mvbd_ref[0, 0])

    y = _dot(mem_out.astype(BF16), wout_ref[mix_width:, :])
    y = y + jnp.concatenate(
        [_dot_tn(mix_ref[0, i], wout_ref[:mix_width, :]) for i in range(mix_ref.shape[1])], axis=0)
    x1 = x_ref[0] + y

    hn = _rms(x1, ffn_g_ref[...]).astype(BF16)
    chunk_cols = [slice(c0, min(c0 + FF_CHUNK, D_FF)) for c0 in range(0, D_FF, FF_CHUNK)]
    n_chunks = len(chunk_cols)

    def up(c):
        cols = chunk_cols[c]
        gate_cols = slice(D_FF + cols.start, D_FF + cols.stop)
        return _dot(hn, wup_ref[0, :, cols]), _dot(hn, wup_ref[0, :, gate_cols])

    u, g = up(0)
    for c in range(n_chunks):
        cols = chunk_cols[c]
        nxt = up(c + 1) if c + 1 < n_chunks else None
        prev = halo_ref[:, cols]
        halo_ref[:, cols] = g[tm - HALO:, :]
        gc = _causal_taps(g, prev, cw_ref[:, cols]) + cb_ref[:, cols]
        act_ref[:, cols] = (_silu(gc) * u).astype(BF16)
        if nxt is not None:
            u, g = nxt
    x2 = x1 + _dot(act_ref[...], wd_ref[0])

    if last:
        out_ref[0] = _rms(x2, final_g_ref[...])
    else:
        out_ref[0] = x2
        hkv = _rms(x2, kv_g_ref[...]).astype(BF16)
        k_ref[0] = _dot(hkv, wk_ref[...]).astype(BF16)
        vt_ref[0] = _dot_nt(wvt_ref[...], hkv).astype(BF16)
        hq = _rms(x2, nxt_g_ref[...]).astype(BF16)
        qnt = _dot_nt(wqt_ref[...], hq).astype(BF16)
        for i in range(qnt_ref.shape[1]):
            qnt_ref[0, i] = qnt[:, i * TQ_BAND:(i + 1) * TQ_BAND]
        qmemn_ref[0] = _dot(hq, wqmem_ref[...]).astype(BF16)


def _post_mixer(x, mix, qmem, mkbdt, mvbd, layer, wup_all, wd_all, wout, ffn_g, cw, cb, tail, last):
    bsz, seq, _ = x.shape
    tm = TM_POST
    row_spec = lambda width: pl.BlockSpec((1, tm, width), lambda b, j: (b, j, 0))
    col_spec = lambda width: pl.BlockSpec((1, width, tm), lambda b, j: (b, 0, j))
    tile_spec = lambda width, tile: pl.BlockSpec((1, tm // tile, width, tile), lambda b, j: (b, j, 0, 0))
    mix_spec = tile_spec(mix.shape[2], mix.shape[3])
    of_layer = lambda w: pl.BlockSpec((1,) + w.shape[1:], lambda b, j: (layer, 0, 0), pipeline_mode=pl.Buffered(1))
    weights = (wout, ffn_g, cw, cb) + tuple(tail)
    in_specs = [
        row_spec(D_MODEL), mix_spec, row_spec(MEM_WIDTH),
        pl.BlockSpec((1, 1) + mkbdt.shape[2:], lambda b, j: (layer, b, 0, 0)),
        pl.BlockSpec((1, 1) + mvbd.shape[2:], lambda b, j: (layer, b, 0, 0)),
        of_layer(wup_all), of_layer(wd_all),
    ] + [_resident(w.shape) for w in weights]
    if last:
        out_specs = row_spec(D_MODEL)
        out_shape = jax.ShapeDtypeStruct((bsz, seq, D_MODEL), F32)
    else:
        out_specs = [row_spec(D_MODEL), row_spec(B_WIDTH), col_spec(B_WIDTH), tile_spec(B_WIDTH, TQ_BAND),
                     row_spec(MEM_WIDTH)]
        out_shape = [
            jax.ShapeDtypeStruct((bsz, seq, D_MODEL), F32),
            jax.ShapeDtypeStruct((bsz, seq, B_WIDTH), BF16),
            jax.ShapeDtypeStruct((bsz, B_WIDTH, seq), BF16),
            jax.ShapeDtypeStruct((bsz, seq // TQ_BAND, B_WIDTH, TQ_BAND), BF16),
            jax.ShapeDtypeStruct((bsz, seq, MEM_WIDTH), BF16),
        ]
    return pl.pallas_call(
        functools.partial(_post_mixer_kernel, last=last),
        grid=(bsz, seq // tm),
        in_specs=in_specs,
        out_specs=out_specs,
        out_shape=out_shape,
        scratch_shapes=[pltpu.VMEM((HALO, D_FF), F32), pltpu.VMEM((tm, D_FF), BF16)],
        compiler_params=pltpu.CompilerParams(
            dimension_semantics=("arbitrary", "arbitrary"), vmem_limit_bytes=VMEM_LIMIT),
        name="post_mixer_last" if last else "post_mixer",
    )(x, mix, qmem, mkbdt, mvbd, wup_all, wd_all, *weights)


def _band_bias_kernel(tbl_ref, out_ref):
    h = pl.program_id(0)
    b = lax.broadcasted_iota(jnp.int32, (SUBLANES, BIAS_LANES), 0)
    c = lax.broadcasted_iota(jnp.int32, (SUBLANES, BIAS_LANES), 1)
    r_signed = jnp.where(c < TQ_HALF, c, c - BIAS_LANES)
    idx = jnp.clip(BAND_PREV + r_signed - b, -(CHUNK - 1), MAX_REL) + (CHUNK - 1)

    def body(e, acc):
        return jnp.where(idx == e, tbl_ref[h, e], acc)

    base = lax.fori_loop(0, REL_SIZE, body, jnp.zeros((SUBLANES, BIAS_LANES), F32), unroll=8)
    kb = lax.broadcasted_iota(jnp.int32, (SUBLANES, TQ_HALF), 0)
    r = lax.broadcasted_iota(jnp.int32, (SUBLANES, TQ_HALF), 1)
    for k0 in range(0, BIAS_ROWS, SUBLANES):
        first = -k0 % BIAS_LANES
        lo = first // TQ_HALF * TQ_HALF
        hi = (lo + TQ_HALF) % BIAS_LANES
        window = jnp.concatenate([base[:, lo:lo + TQ_HALF], base[:, hi:hi + TQ_HALF]], axis=1)
        if first != lo:
            window = pltpu.roll(window, 2 * TQ_HALF - (first - lo), axis=1)
        dchunk = (BAND_PREV + r) // CHUNK - (k0 + kb) // CHUNK
        valid = (dchunk >= 0) & (dchunk < BAND_CHUNKS)
        out_ref[0, k0:k0 + SUBLANES, :] = jnp.where(valid, window[:, :TQ_HALF], NEG_BIG)


def _band_bias(rel_table):
    return pl.pallas_call(
        _band_bias_kernel,
        grid=(B_HEADS,),
        in_specs=[pl.BlockSpec(memory_space=pltpu.SMEM)],
        out_specs=pl.BlockSpec((1, BIAS_ROWS, TQ_HALF), lambda h: (h, 0, 0)),
        out_shape=jax.ShapeDtypeStruct((B_HEADS, BIAS_ROWS, TQ_HALF), F32),
        compiler_params=pltpu.CompilerParams(dimension_semantics=("arbitrary",)),
        name="band_bias",
    )(rel_table)


def _band_attn_kernel(qt_ref, k_ref, vt_ref, bias_ref, out_ref):
    j = pl.program_id(1)
    first_half = 2 * j - HALVES_BACK
    start = pl.multiple_of(jnp.maximum(first_half, 0) * TQ_HALF, TQ_HALF)
    pair_width = 2 * B_HEAD_DIM
    pair_row = lax.broadcasted_iota(jnp.int32, (pair_width, TQ_BAND), 0)
    zeros_half = jnp.zeros((TQ_HALF, TQ_HALF), BF16)
    masked_half = jnp.full((TQ_HALF, TQ_HALF), NEG_BIG, F32)

    def scores(pair):
        lo = pair * pair_width
        k_pair = k_ref[0, pl.ds(start, TK_BAND), lo:lo + pair_width]
        q_pair = qt_ref[0, 0, lo:lo + pair_width, :]
        zero = jnp.zeros_like(q_pair)
        q_sel = jnp.concatenate([jnp.where(pair_row < B_HEAD_DIM, q_pair, zero),
                                 jnp.where(pair_row >= B_HEAD_DIM, q_pair, zero)], axis=1)
        return _dot(k_pair, q_sel)

    def softmax_half(t, row0, c0):
        blocks = [t[row0 + r:row0 + r + SOFTMAX_ROWS, c0:c0 + TQ_HALF] for r in range(0, TK_HALF, SOFTMAX_ROWS)]
        m_acc = blocks[0]
        for blk in blocks[1:]:
            m_acc = jnp.maximum(m_acc, blk)
        m = jnp.max(m_acc, axis=0, keepdims=True)
        ps = []
        l_acc = None
        for blk in blocks:
            e = jnp.exp(blk - m)
            ps.append(e.astype(BF16))
            l_acc = e if l_acc is None else l_acc + e
        return jnp.concatenate(ps, axis=0), jnp.sum(l_acc, axis=0, keepdims=True)

    def tile(off_b, bias_row_a, bias_row_b):
        def bias_tile(pair):
            cols = []
            for h in (2 * pair, 2 * pair + 1):
                bias_a = bias_ref[h, pl.ds(bias_row_a, TK_HALF), :]
                bias_b = bias_ref[h, pl.ds(bias_row_b, TK_HALF), :]
                cols.append(jnp.concatenate([bias_a, masked_half], axis=0))
                cols.append(jnp.concatenate([masked_half, bias_b] if off_b else [bias_b, masked_half], axis=0))
            return jnp.concatenate(cols, axis=1)

        n_pairs = B_HEADS // 2
        live = [dict() for _ in range(n_pairs)]

        def biased_scores(pair):
            live[pair]["t"] = scores(pair) + bias_tile(pair)

        def softmax(pair):
            t = live[pair].pop("t")
            cols, sums = [], []
            for parity in range(2):
                c0 = parity * TQ_BAND
                p_a, l_a = softmax_half(t, 0, c0)
                p_b, l_b = softmax_half(t, off_b, c0 + TQ_HALF)
                cols.append(jnp.concatenate([p_a, zeros_half], axis=0))
                cols.append(jnp.concatenate([zeros_half, p_b] if off_b else [p_b, zeros_half], axis=0))
                sums += [l_a, l_b]
            live[pair]["p"] = jnp.concatenate(cols, axis=1)
            live[pair]["inv"] = 1.0 / jnp.concatenate(sums, axis=1)

        def values(pair):
            lo = pair * pair_width
            vt = vt_ref[0, lo:lo + pair_width, pl.ds(start, TK_BAND)]
            o = _dot(vt, live[pair].pop("p"))
            inv = live[pair].pop("inv")
            out_ref[0, 0, lo:lo + B_HEAD_DIM, :] = (o[:B_HEAD_DIM, :TQ_BAND] * inv[:, :TQ_BAND]).astype(BF16)
            out_ref[0, 0, lo + B_HEAD_DIM:lo + pair_width, :] = (
                o[B_HEAD_DIM:, TQ_BAND:] * inv[:, TQ_BAND:]).astype(BF16)

        stages = (biased_scores, softmax, values)
        for slot in range(n_pairs + len(stages) - 1):
            for depth in (0, 2, 1):
                if 0 <= slot - depth < n_pairs:
                    stages[depth](slot - depth)

    pl.when(first_half >= 0)(functools.partial(tile, TQ_HALF, 0, 0))

    @pl.when(first_half < 0)
    def _():
        row_a = pl.multiple_of(-first_half * TQ_HALF, TQ_HALF)
        tile(0, row_a, pl.multiple_of(row_a - TQ_HALF, TQ_HALF))


def _band_attn(qt, k, vt, bias):
    bsz, seq, _ = k.shape
    return pl.pallas_call(
        _band_attn_kernel,
        grid=(bsz, seq // TQ_BAND),
        in_specs=[
            pl.BlockSpec((1, 1, B_WIDTH, TQ_BAND), lambda b, j: (b, j, 0, 0)),
            pl.BlockSpec((1, seq, B_WIDTH), lambda b, j: (b, 0, 0)),
            pl.BlockSpec((1, B_WIDTH, seq), lambda b, j: (b, 0, 0)),
            _resident(bias.shape),
        ],
        out_specs=pl.BlockSpec((1, 1, B_WIDTH, TQ_BAND), lambda b, j: (b, j, 0, 0)),
        out_shape=jax.ShapeDtypeStruct((bsz, seq // TQ_BAND, B_WIDTH, TQ_BAND), BF16),
        compiler_params=pltpu.CompilerParams(
            dimension_semantics=("arbitrary", "arbitrary"), vmem_limit_bytes=VMEM_LIMIT),
        name="band_attn",
    )(qt, k, vt, bias)


def kernel(x, mem, norm_mix_g, norm_ffn_g, a_w_in, a_gate_b, a_conv_w, a_conv_b, a_head_g, a_w_out, kv_norm_g, w_kv,
           b_w_in, b_rel_bias, b_w_out, mem_w_kv, ffn_w_up, ffn_conv_w, ffn_conv_b, ffn_w_down, final_g):
    bsz, seq, d = x.shape
    assert d == D_MODEL and seq % max(TM_IN, L_MLSTM, TM_POST, TQ_BAND) == 0
    assert a_w_in.shape[0] == 1 and b_w_in.shape[0] == 1, "one mLSTM layer followed by one band-attention layer"
    assert TM_IN % L_MLSTM == 0, "the input projection emits per-chunk decay terms: whole chunks per tile"
    assert bsz % MLSTM_BATCH == 0
    row = lambda g: g.reshape(1, -1).astype(F32)

    mkbdt, mvbd = _mem_kv(mem, mem_w_kv)

    wqk, wvot, wk_b, wvt_b, wqt_b, wqmem_b = _prep_weights(a_w_in, w_kv, b_w_in)

    w = a_w_in[0]
    gate_pad = ((0, 0), (0, GATE_LANES - A_HEADS))
    g0 = 4 * A_WIDTH
    wgm = jnp.concatenate([jnp.pad(w[:, g0:g0 + A_HEADS], gate_pad), jnp.pad(w[:, g0 + A_HEADS:g0 + 2 * A_HEADS], gate_pad),
                           MEM_HEAD_DIM ** -0.5 * w[:, g0 + 2 * A_HEADS:]], axis=1).astype(BF16)
    gate_b = jnp.pad(a_gate_b[0].astype(F32).reshape(2, A_HEADS), gate_pad).reshape(1, 2 * GATE_LANES)
    vone = (jnp.arange(A_HEADS * V_AUG) % V_AUG == A_HEAD_DIM).astype(F32).reshape(A_HEADS * V_AUG, 1)
    q, k, vt, ot_half, gate_rows, a_cols, decay, qmem = _a_in(
        x, row(norm_mix_g[0]), wqk, wvot, wgm, gate_b, vone, a_conv_w[0].astype(F32), row(a_conv_b[0]))

    head_g_half = 0.5 * a_head_g[0].reshape(A_HEADS, A_HEAD_DIM, 1).astype(F32)
    mix = _mlstm(q, k, vt, ot_half, gate_rows, a_cols, decay, head_g_half)

    wup_all = ffn_w_up.astype(BF16)
    wd_all = ffn_w_down.astype(BF16)

    def ffn_small(l):
        return row(norm_ffn_g[l]), ffn_conv_w[l].astype(F32), row(ffn_conv_b[l])

    tail = (row(kv_norm_g), wk_b, wvt_b, row(norm_mix_g[1]), wqt_b, wqmem_b)
    x1, kb, vbt, qbt, qmem_b = _post_mixer(
        x, mix, qmem, mkbdt, mvbd, 0, wup_all, wd_all, a_w_out[0].astype(BF16), *ffn_small(0), tail=tail, last=False)

    bias = _band_bias(b_rel_bias[0].astype(F32))
    mix_b = _band_attn(qbt, kb, vbt, bias)
    return _post_mixer(
        x1, mix_b, qmem_b, mkbdt, mvbd, 1, wup_all, wd_all, b_w_out[0].astype(BF16), *ffn_small(1),
        tail=(row(final_g),), last=True)
```

```python
import functools
import math

import jax
import jax.numpy as jnp
from jax import lax
from jax.experimental import pallas as pl
from jax.experimental.pallas import tpu as pltpu

F32 = jnp.float32
BF16 = jnp.bfloat16

LANES = 128
SUBLANES = 8
BF16_SUBLANES = 16
VMEM_BYTES = 64 * 1024 * 1024
VMEM_LIMIT = VMEM_BYTES // 8 * 7

D_MODEL = 1024
CHUNK = 64
MEM_TOKENS = 256
MEM_HEADS = 4
MEM_WIDTH = 256
MEM_HEAD_DIM = MEM_WIDTH // MEM_HEADS
A_WIDTH = 768
A_HEADS = 4
A_HEAD_DIM = A_WIDTH // A_HEADS
A_CONV = 4
B_HEADS = 12
B_HEAD_DIM = 64
B_WIDTH = B_HEADS * B_HEAD_DIM
BAND_CHUNKS = 9
MAX_REL = 128
REL_SIZE = MAX_REL + CHUNK
D_FF = 2816
FFN_CONV = 3
EPS = 1e-6

V_AUG = -(-(A_HEAD_DIM + 1) // BF16_SUBLANES) * BF16_SUBLANES
GATE_LANES = LANES
GATE_TERMS = 4
TERM_G, TERM_INTER, TERM_EN, TERM_WEXP = range(GATE_TERMS)
HALO = SUBLANES
NEG_BIG = -1e30

TM_IN = 512
L_MLSTM = 256
MLSTM_BATCH = 4
TM_POST = 512
FF_CHUNK = 256
TQ_HALF = LANES
TQ_BAND = 2 * TQ_HALF
BAND_PREV = (BAND_CHUNKS - 1) * CHUNK
TK_HALF = BAND_PREV + TQ_HALF
TK_BAND = BAND_PREV + TQ_BAND
HALVES_BACK = BAND_PREV // TQ_HALF
BIAS_ROWS = TK_HALF + BAND_PREV
BIAS_LANES = BIAS_ROWS + TQ_HALF
SOFTMAX_ROWS = 8 * SUBLANES


def _resident(shape):
    nd = len(shape)
    return pl.BlockSpec(shape, lambda *_: (0,) * nd, pipeline_mode=pl.Buffered(1))


def _rms(x, g):
    return x * lax.rsqrt(jnp.mean(x * x, axis=-1, keepdims=True) + EPS) * g


def _dot(a, b):
    return jnp.dot(a, b, preferred_element_type=F32)


def _dot_nt(a, b):
    return lax.dot_general(a, b, (((1,), (1,)), ((), ())), preferred_element_type=F32)


def _dot_tn(a, b):
    return lax.dot_general(a, b, (((0,), (0,)), ((), ())), preferred_element_type=F32)


def _silu(x):
    half = 0.5 * x
    return half + half * jnp.tanh(half)


def _mem_kv_kernel(mem_ref, w_ref, mkbdt_ref, mvbd_ref):
    m = mem_ref[0].astype(BF16)
    w = w_ref[0].astype(BF16)
    mkt = _dot(m, w[:, :MEM_WIDTH]).T
    mv = _dot(m, w[:, MEM_WIDTH:])
    f_idx = lax.broadcasted_iota(jnp.int32, (MEM_WIDTH, MEM_HEADS * MEM_TOKENS), 0)
    c_idx = lax.broadcasted_iota(jnp.int32, (MEM_WIDTH, MEM_HEADS * MEM_TOKENS), 1)
    mkt4 = jnp.concatenate([mkt] * MEM_HEADS, axis=1)
    mkbdt_ref[0, 0] = jnp.where(f_idx // MEM_HEAD_DIM == c_idx // MEM_TOKENS, mkt4, 0.0).astype(BF16)
    r_idx = lax.broadcasted_iota(jnp.int32, (MEM_HEADS * MEM_TOKENS, MEM_WIDTH), 0)
    g_idx = lax.broadcasted_iota(jnp.int32, (MEM_HEADS * MEM_TOKENS, MEM_WIDTH), 1)
    mv4 = jnp.concatenate([mv] * MEM_HEADS, axis=0)
    mvbd_ref[0, 0] = jnp.where(r_idx // MEM_TOKENS == g_idx // MEM_HEAD_DIM, mv4, 0.0).astype(BF16)


def _mem_kv(mem, mem_w_kv):
    depth = mem_w_kv.shape[0]
    bsz = mem.shape[0]
    return pl.pallas_call(
        _mem_kv_kernel,
        grid=(depth, bsz),
        in_specs=[
            pl.BlockSpec((1, MEM_TOKENS, D_MODEL), lambda l, b: (b, 0, 0)),
            pl.BlockSpec((1, D_MODEL, 2 * MEM_WIDTH), lambda l, b: (l, 0, 0)),
        ],
        out_specs=[
            pl.BlockSpec((1, 1, MEM_WIDTH, MEM_HEADS * MEM_TOKENS), lambda l, b: (l, b, 0, 0)),
            pl.BlockSpec((1, 1, MEM_HEADS * MEM_TOKENS, MEM_WIDTH), lambda l, b: (l, b, 0, 0)),
        ],
        out_shape=[
            jax.ShapeDtypeStruct((depth, bsz, MEM_WIDTH, MEM_HEADS * MEM_TOKENS), BF16),
            jax.ShapeDtypeStruct((depth, bsz, MEM_HEADS * MEM_TOKENS, MEM_WIDTH), BF16),
        ],
        compiler_params=pltpu.CompilerParams(dimension_semantics=("arbitrary", "arbitrary")),
        name="mem_kv",
    )(mem, mem_w_kv)


def _prep_weights_kernel(a_ref, kv_ref, b_ref, wqk_ref, wvot_ref, wgm_ref, wk_ref, wvtb_ref, wqt_ref, wqmem_ref):
    wqk_ref[...] = a_ref[0, :, :2 * A_WIDTH].astype(BF16)
    g0 = 4 * A_WIDTH
    first = a_ref[0, :, g0:g0 + LANES]
    lane = lax.broadcasted_iota(jnp.int32, first.shape, 1)
    wgm_ref[:, :GATE_LANES] = jnp.where(lane < A_HEADS, first, 0.0).astype(BF16)
    wgm_ref[:, GATE_LANES:2 * GATE_LANES] = jnp.where(
        lane < A_HEADS, pltpu.roll(first, LANES - A_HEADS, axis=1), 0.0).astype(BF16)
    wgm_ref[:, 2 * GATE_LANES:] = (MEM_HEAD_DIM ** -0.5 * a_ref[0, :, g0 + 2 * A_HEADS:]).astype(BF16)
    vt = a_ref[0, :, 2 * A_WIDTH:3 * A_WIDTH].T
    pad = jnp.zeros((V_AUG - A_HEAD_DIM, D_MODEL), BF16)
    for h in range(A_HEADS):
        wvot_ref[h * V_AUG:h * V_AUG + A_HEAD_DIM, :] = vt[h * A_HEAD_DIM:(h + 1) * A_HEAD_DIM, :].astype(BF16)
        wvot_ref[h * V_AUG + A_HEAD_DIM:(h + 1) * V_AUG, :] = pad
    wvot_ref[A_HEADS * V_AUG:, :] = (0.5 * a_ref[0, :, 3 * A_WIDTH:4 * A_WIDTH]).T.astype(BF16)
    wk_ref[...] = kv_ref[:, :B_WIDTH].astype(BF16)
    wvtb_ref[...] = kv_ref[:, B_WIDTH:].T.astype(BF16)
    wqt_ref[...] = (B_HEAD_DIM ** -0.5 * b_ref[0, :, :B_WIDTH]).T.astype(BF16)
    wqmem_ref[...] = (MEM_HEAD_DIM ** -0.5 * b_ref[0, :, B_WIDTH:]).astype(BF16)


def _prep_weights(a_w_in, w_kv, b_w_in):
    assert math.log2(MEM_HEAD_DIM ** -0.5).is_integer() and math.log2(B_HEAD_DIM ** -0.5).is_integer()
    once = lambda shape: pl.BlockSpec(shape, lambda i: (0,) * len(shape), pipeline_mode=pl.Buffered(1))
    out_shapes = [(D_MODEL, 2 * A_WIDTH), (A_HEADS * V_AUG + A_WIDTH, D_MODEL), (D_MODEL, 2 * GATE_LANES + MEM_WIDTH),
                  (D_MODEL, B_WIDTH), (B_WIDTH, D_MODEL), (B_WIDTH, D_MODEL), (D_MODEL, MEM_WIDTH)]
    return pl.pallas_call(
        _prep_weights_kernel,
        grid=(1,),
        in_specs=[once(a_w_in.shape), once(w_kv.shape), once(b_w_in.shape)],
        out_specs=[once(s) for s in out_shapes],
        out_shape=[jax.ShapeDtypeStruct(s, BF16) for s in out_shapes],
        compiler_params=pltpu.CompilerParams(dimension_semantics=("arbitrary",), vmem_limit_bytes=VMEM_LIMIT),
        name="prep_weights",
    )(a_w_in, w_kv, b_w_in)


def _causal_taps(x, prev, taps):
    n_taps = taps.shape[0]
    halo_row = lax.broadcasted_iota(jnp.int32, (HALO, x.shape[1]), 0)
    y = taps[n_taps - 1:n_taps, :] * x
    for back in range(1, n_taps):
        head = jnp.where(halo_row < back, pltpu.roll(prev, back, axis=0), pltpu.roll(x[:HALO, :], back, axis=0))
        shifted = jnp.concatenate([head, pltpu.roll(x, back, axis=0)[HALO:, :]], axis=0)
        y = y + taps[n_taps - 1 - back:n_taps - back, :] * shifted
    return y


def _lane_scan(x, op, identity):
    lane = lax.broadcasted_iota(jnp.int32, x.shape, 1)
    step = 1
    while step < x.shape[1]:
        x = op(x, jnp.where(lane >= step, pltpu.roll(x, step, axis=1), identity))
        step *= 2
    return x


def _a_in_kernel(x_ref, g_ref, wqk_ref, wvot_ref, wgm_ref, gate_b_ref, vone_ref, cw_ref, cb_ref,
                 q_ref, k_ref, vt_ref, ot_ref, rows_ref, acol_ref, decay_ref, qmem_ref, halo_ref, m_ref):
    L = L_MLSTM
    n_chunks = x_ref.shape[1] // L

    @pl.when(pl.program_id(1) == 0)
    def _():
        halo_ref[...] = jnp.zeros(halo_ref.shape, F32)
        m_ref[...] = jnp.zeros(m_ref.shape, F32)

    def projections(c):
        tok = slice(c * L, (c + 1) * L)
        hn = _rms(x_ref[0, tok, :], g_ref[...]).astype(BF16)

        gm = _dot(hn, wgm_ref[...])
        qmem_ref[0, tok, :] = gm[:, 2 * GATE_LANES:].astype(BF16)
        gi = (gm[:, :GATE_LANES] + gate_b_ref[:, :GATE_LANES]).T[0:SUBLANES, :]
        fg = (gm[:, GATE_LANES:2 * GATE_LANES] + gate_b_ref[:, GATE_LANES:]).T[0:SUBLANES, :]
        head_row = lax.broadcasted_iota(jnp.int32, (SUBLANES, L), 0) < A_HEADS
        logf = jnp.where(head_row, jnp.minimum(fg, 0.0) - jnp.log(1.0 + jnp.exp(-jnp.abs(fg))), 0.0)
        b = _lane_scan(logf, jnp.add, 0.0)
        a = gi - b
        m_prev = m_ref[:, 0:1]
        g = jnp.maximum(_lane_scan(a, jnp.maximum, -jnp.inf), m_prev)
        g_last = g[:, L - 1:L]
        m_ref[...] = jnp.broadcast_to(b[:, L - 1:L] + g_last, m_ref.shape)
        decay_ref[0, c] = jnp.broadcast_to(jnp.exp(m_prev - g_last), decay_ref.shape[2:])
        terms = [None] * GATE_TERMS
        terms[TERM_G], terms[TERM_INTER] = g, jnp.exp(m_prev - g)
        terms[TERM_EN], terms[TERM_WEXP] = jnp.exp(-(b + g)), jnp.exp(a - g_last)
        rows_ref[0, :, tok] = jnp.concatenate(terms, axis=0)
        acol_ref[0, tok, :] = jnp.concatenate([a, jnp.zeros((GATE_LANES - SUBLANES, L), F32)], axis=0).T

        raw = _dot(hn, wqk_ref[...])
        vo = _dot_nt(wvot_ref[...], hn)
        n_v = vt_ref.shape[2]
        vt_ref[0, c] = (vo[:n_v] + vone_ref[...]).astype(BF16)
        ot_ref[0, c] = vo[n_v:]
        return raw

    def conv(c, raw):
        tok = slice(c * L, (c + 1) * L)
        prev = halo_ref[...]
        halo_ref[...] = raw[L - HALO:, :]
        qk = _silu(_causal_taps(raw, prev, cw_ref[...]) + cb_ref[...])
        for h in range(A_HEADS):
            lo = h * A_HEAD_DIM
            q_ref[0, h, tok, :] = qk[:, lo:lo + A_HEAD_DIM].astype(BF16)
            k_ref[0, h, tok, :] = (qk[:, A_WIDTH + lo:A_WIDTH + lo + A_HEAD_DIM]
                                   * (A_HEAD_DIM ** -0.5)).astype(BF16)

    raws = {}
    for slot in range(n_chunks + 1):
        if slot < n_chunks:
            raws[slot] = projections(slot)
        if slot >= 1:
            conv(slot - 1, raws.pop(slot - 1))


def _a_in(x, g, wqk, wvot, wgm, gate_b, vone, cw, cb):
    bsz, seq, _ = x.shape
    tm = TM_IN
    head_spec = lambda width: pl.BlockSpec((1, A_HEADS, tm, width), lambda b, j: (b, 0, j, 0))
    row_spec = lambda width: pl.BlockSpec((1, tm, width), lambda b, j: (b, j, 0))
    col_spec = lambda width: pl.BlockSpec((1, width, tm), lambda b, j: (b, 0, j))
    tile_spec = lambda width: pl.BlockSpec((1, tm // L_MLSTM, width, L_MLSTM), lambda b, j: (b, j, 0, 0))
    weights = (g, wqk, wvot, wgm, gate_b, vone, cw, cb)
    return pl.pallas_call(
        _a_in_kernel,
        grid=(bsz, seq // tm),
        in_specs=[row_spec(D_MODEL)] + [_resident(w.shape) for w in weights],
        out_specs=[
            head_spec(A_HEAD_DIM), head_spec(A_HEAD_DIM), tile_spec(A_HEADS * V_AUG), tile_spec(A_WIDTH),
            col_spec(GATE_TERMS * SUBLANES),
            row_spec(GATE_LANES),
            pl.BlockSpec((1, tm // L_MLSTM, SUBLANES, GATE_LANES), lambda b, j: (b, j, 0, 0)),
            row_spec(MEM_WIDTH),
        ],
        out_shape=[
            jax.ShapeDtypeStruct((bsz, A_HEADS, seq, A_HEAD_DIM), BF16),
            jax.ShapeDtypeStruct((bsz, A_HEADS, seq, A_HEAD_DIM), BF16),
            jax.ShapeDtypeStruct((bsz, seq // L_MLSTM, A_HEADS * V_AUG, L_MLSTM), BF16),
            jax.ShapeDtypeStruct((bsz, seq // L_MLSTM, A_WIDTH, L_MLSTM), F32),
            jax.ShapeDtypeStruct((bsz, GATE_TERMS * SUBLANES, seq), F32),
            jax.ShapeDtypeStruct((bsz, seq, GATE_LANES), F32),
            jax.ShapeDtypeStruct((bsz, seq // L_MLSTM, SUBLANES, GATE_LANES), F32),
            jax.ShapeDtypeStruct((bsz, seq, MEM_WIDTH), BF16),
        ],
        scratch_shapes=[pltpu.VMEM((HALO, 2 * A_WIDTH), F32), pltpu.VMEM((SUBLANES, GATE_LANES), F32)],
        compiler_params=pltpu.CompilerParams(
            dimension_semantics=("arbitrary", "arbitrary"), vmem_limit_bytes=VMEM_LIMIT),
        name="a_in",
    )(x, *weights)


def _mlstm_kernel(q_ref, k_ref, vt_ref, ot_ref, rows_ref, acol_ref, decay_ref, hg_ref, out_ref, ct_ref):
    L = q_ref.shape[2]
    half = L // 2

    @pl.when(pl.program_id(1) == 0)
    def _():
        ct_ref[...] = jnp.zeros(ct_ref.shape, F32)

    upper = (lax.broadcasted_iota(jnp.int32, (half, half), 0)
             <= lax.broadcasted_iota(jnp.int32, (half, half), 1))
    zero_block = jnp.zeros((half, half), BF16)
    items = [(n, h) for n in range(q_ref.shape[0]) for h in range(A_HEADS)]
    live = [dict() for _ in items]

    def term(n, which, h):
        return rows_ref[n, SUBLANES * which + h:SUBLANES * which + h + 1, :]

    def scores(i):
        n, h = items[i]
        live[i]["st"] = _dot_nt(k_ref[n, h], q_ref[n, h])

    def decay_weight(i):
        n, h = items[i]
        st = live[i].pop("st")
        a_col = acol_ref[n, :, h:h + 1]
        g_row = term(n, TERM_G, h)
        p00 = jnp.exp(jnp.where(upper, a_col[:half] - g_row[:, :half], -jnp.inf))
        p01 = jnp.exp(a_col[:half] - g_row[:, half:])
        p11 = jnp.exp(jnp.where(upper, a_col[half:] - g_row[:, half:], -jnp.inf))
        live[i]["s"] = jnp.concatenate([
            jnp.concatenate([(st[:half, :half] * p00).astype(BF16), (st[:half, half:] * p01).astype(BF16)], axis=1),
            jnp.concatenate([zero_block, (st[half:, half:] * p11).astype(BF16)], axis=1)], axis=0)

    def numerator(i):
        n, h = items[i]
        vt = vt_ref[n, 0, h * V_AUG:(h + 1) * V_AUG, :]
        ct_prev = ct_ref[n, h]
        live[i]["num"] = (_dot(vt, live[i].pop("s"))
                          + _dot_nt(ct_prev.astype(BF16), q_ref[n, h]) * term(n, TERM_INTER, h))
        wvt = (vt.astype(F32) * term(n, TERM_WEXP, h)).astype(BF16)
        ct_ref[n, h] = decay_ref[n, 0, h:h + 1, 0:1] * ct_prev + _dot(wvt, k_ref[n, h])

    def normalise(i):
        n, h = items[i]
        numt = live[i].pop("num")
        den = numt[A_HEAD_DIM:A_HEAD_DIM + 1, :]
        r = 1.0 / jnp.maximum(jnp.abs(den), term(n, TERM_EN, h))
        body = numt[:A_HEAD_DIM, :]
        ssq = jnp.sum(body * body, axis=0, keepdims=True)
        factor = r * lax.rsqrt(r * r * ssq * (1.0 / A_HEAD_DIM) + EPS)
        lo = h * A_HEAD_DIM
        gate = hg_ref[h] + hg_ref[h] * jnp.tanh(ot_ref[n, 0, lo:lo + A_HEAD_DIM, :])
        out_ref[n, 0, lo:lo + A_HEAD_DIM, :] = (body * factor * gate).astype(BF16)

    stages = (scores, decay_weight, numerator, normalise)
    for slot in range(len(items) + len(stages) - 1):
        for depth, stage in enumerate(stages):
            if 0 <= slot - depth < len(items):
                stage(slot - depth)


def _mlstm(q, k, vt, ot_half, rows, a_cols, decay, head_g_half):
    bsz, nh, seq, dh = q.shape
    L = L_MLSTM
    nb = MLSTM_BATCH
    head_spec = pl.BlockSpec((nb, nh, L, dh), lambda b, j: (b, 0, j, 0))
    col_spec = lambda width: pl.BlockSpec((nb, width, L), lambda b, j: (b, 0, j))
    tile_spec = lambda width: pl.BlockSpec((nb, 1, width, L), lambda b, j: (b, j, 0, 0))
    return pl.pallas_call(
        _mlstm_kernel,
        grid=(bsz // nb, seq // L),
        in_specs=[
            head_spec, head_spec, tile_spec(nh * V_AUG), tile_spec(nh * dh), col_spec(rows.shape[1]),
            pl.BlockSpec((nb, L, GATE_LANES), lambda b, j: (b, j, 0)),
            pl.BlockSpec((nb, 1, SUBLANES, GATE_LANES), lambda b, j: (b, j, 0, 0)),
            _resident(head_g_half.shape),
        ],
        out_specs=tile_spec(nh * dh),
        out_shape=jax.ShapeDtypeStruct((bsz, seq // L, nh * dh, L), BF16),
        scratch_shapes=[pltpu.VMEM((nb, nh, V_AUG, dh), F32)],
        compiler_params=pltpu.CompilerParams(
            dimension_semantics=("arbitrary", "arbitrary"), vmem_limit_bytes=VMEM_LIMIT),
        name="mlstm",
    )(q, k, vt, ot_half, rows, a_cols, decay, head_g_half)


def _post_mixer_kernel(*refs, last):
    if last:
        (x_ref, mix_ref, qmem_ref, mkbdt_ref, mvbd_ref, wup_ref, wd_ref, wout_ref, ffn_g_ref, cw_ref, cb_ref,
         final_g_ref, out_ref, halo_ref, act_ref) = refs
    else:
        (x_ref, mix_ref, qmem_ref, mkbdt_ref, mvbd_ref, wup_ref, wd_ref, wout_ref, ffn_g_ref, cw_ref, cb_ref,
         kv_g_ref, wk_ref, wvt_ref, nxt_g_ref, wqt_ref, wqmem_ref,
         out_ref, k_ref, vt_ref, qnt_ref, qmemn_ref, halo_ref, act_ref) = refs
    tm = x_ref.shape[1]
    mix_width = mix_ref.shape[2]

    @pl.when(pl.program_id(1) == 0)
    def _():
        halo_ref[...] = jnp.zeros((HALO, D_FF), F32)

    s = _dot(qmem_ref[0], mkbdt_ref[0, 0])
    probs = []
    for h in range(MEM_HEADS):
        sh = s[:, h * MEM_TOKENS:(h + 1) * MEM_TOKENS]
        e = jnp.exp(sh - jnp.max(sh, axis=1, keepdims=True))
        probs.append((e * (1.0 / jnp.sum(e, axis=1, keepdims=True))).astype(BF16))
    mem_out = _dot(jnp.concatenate(probs, axis=1), mvbd_ref[0, 0])

    y = _dot(mem_out.astype(BF16), wout_ref[mix_width:, :])
    y = y + jnp.concatenate(
        [_dot_tn(mix_ref[0, i], wout_ref[:mix_width, :]) for i in range(mix_ref.shape[1])], axis=0)
    x1 = x_ref[0] + y

    hn = _rms(x1, ffn_g_ref[...]).astype(BF16)
    chunk_cols = [slice(c0, min(c0 + FF_CHUNK, D_FF)) for c0 in range(0, D_FF, FF_CHUNK)]
    n_chunks = len(chunk_cols)

    def up(c):
        cols = chunk_cols[c]
        gate_cols = slice(D_FF + cols.start, D_FF + cols.stop)
        return _dot(hn, wup_ref[0, :, cols]), _dot(hn, wup_ref[0, :, gate_cols])

    u, g = up(0)
    for c in range(n_chunks):
        cols = chunk_cols[c]
        nxt = up(c + 1) if c + 1 < n_chunks else None
        prev = halo_ref[:, cols]
        halo_ref[:, cols] = g[tm - HALO:, :]
        gc = _causal_taps(g, prev, cw_ref[:, cols]) + cb_ref[:, cols]
        act_ref[:, cols] = (_silu(gc) * u).astype(BF16)
        if nxt is not None:
            u, g = nxt
    x2 = x1 + _dot(act_ref[...], wd_ref[0])

    if last:
        out_ref[0] = _rms(x2, final_g_ref[...])
    else:
        out_ref[0] = x2
        hkv = _rms(x2, kv_g_ref[...]).astype(BF16)
        k_ref[0] = _dot(hkv, wk_ref[...]).astype(BF16)
        vt_ref[0] = _dot_nt(wvt_ref[...], hkv).astype(BF16)
        hq = _rms(x2, nxt_g_ref[...]).astype(BF16)
        qnt = _dot_nt(wqt_ref[...], hq).astype(BF16)
        for i in range(qnt_ref.shape[1]):
            qnt_ref[0, i] = qnt[:, i * TQ_BAND:(i + 1) * TQ_BAND]
        qmemn_ref[0] = _dot(hq, wqmem_ref[...]).astype(BF16)


def _post_mixer(x, mix, qmem, mkbdt, mvbd, layer, wup_all, wd_all, wout, ffn_g, cw, cb, tail, last):
    bsz, seq, _ = x.shape
    tm = TM_POST
    row_spec = lambda width: pl.BlockSpec((1, tm, width), lambda b, j: (b, j, 0))
    col_spec = lambda width: pl.BlockSpec((1, width, tm), lambda b, j: (b, 0, j))
    tile_spec = lambda width, tile: pl.BlockSpec((1, tm // tile, width, tile), lambda b, j: (b, j, 0, 0))
    mix_spec = tile_spec(mix.shape[2], mix.shape[3])
    of_layer = lambda w: pl.BlockSpec((1,) + w.shape[1:], lambda b, j: (layer, 0, 0), pipeline_mode=pl.Buffered(1))
    weights = (wout, ffn_g, cw, cb) + tuple(tail)
    in_specs = [
        row_spec(D_MODEL), mix_spec, row_spec(MEM_WIDTH),
        pl.BlockSpec((1, 1) + mkbdt.shape[2:], lambda b, j: (layer, b, 0, 0)),
        pl.BlockSpec((1, 1) + mvbd.shape[2:], lambda b, j: (layer, b, 0, 0)),
        of_layer(wup_all), of_layer(wd_all),
    ] + [_resident(w.shape) for w in weights]
    if last:
        out_specs = row_spec(D_MODEL)
        out_shape = jax.ShapeDtypeStruct((bsz, seq, D_MODEL), F32)
    else:
        out_specs = [row_spec(D_MODEL), row_spec(B_WIDTH), col_spec(B_WIDTH), tile_spec(B_WIDTH, TQ_BAND),
                     row_spec(MEM_WIDTH)]
        out_shape = [
            jax.ShapeDtypeStruct((bsz, seq, D_MODEL), F32),
            jax.ShapeDtypeStruct((bsz, seq, B_WIDTH), BF16),
            jax.ShapeDtypeStruct((bsz, B_WIDTH, seq), BF16),
            jax.ShapeDtypeStruct((bsz, seq // TQ_BAND, B_WIDTH, TQ_BAND), BF16),
            jax.ShapeDtypeStruct((bsz, seq, MEM_WIDTH), BF16),
        ]
    return pl.pallas_call(
        functools.partial(_post_mixer_kernel, last=last),
        grid=(bsz, seq // tm),
        in_specs=in_specs,
        out_specs=out_specs,
        out_shape=out_shape,
        scratch_shapes=[pltpu.VMEM((HALO, D_FF), F32), pltpu.VMEM((tm, D_FF), BF16)],
        compiler_params=pltpu.CompilerParams(
            dimension_semantics=("arbitrary", "arbitrary"), vmem_limit_bytes=VMEM_LIMIT),
        name="post_mixer_last" if last else "post_mixer",
    )(x, mix, qmem, mkbdt, mvbd, wup_all, wd_all, *weights)


def _band_bias_kernel(tbl_ref, out_ref):
    h = pl.program_id(0)
    b = lax.broadcasted_iota(jnp.int32, (SUBLANES, BIAS_LANES), 0)
    c = lax.broadcasted_iota(jnp.int32, (SUBLANES, BIAS_LANES), 1)
    r_signed = jnp.where(c < TQ_HALF, c, c - BIAS_LANES)
    idx = jnp.clip(BAND_PREV + r_signed - b, -(CHUNK - 1), MAX_REL) + (CHUNK - 1)

    def body(e, acc):
        return jnp.where(idx == e, tbl_ref[h, e], acc)

    base = lax.fori_loop(0, REL_SIZE, body, jnp.zeros((SUBLANES, BIAS_LANES), F32), unroll=8)
    kb = lax.broadcasted_iota(jnp.int32, (SUBLANES, TQ_HALF), 0)
    r = lax.broadcasted_iota(jnp.int32, (SUBLANES, TQ_HALF), 1)
    for k0 in range(0, BIAS_ROWS, SUBLANES):
        first = -k0 % BIAS_LANES
        lo = first // TQ_HALF * TQ_HALF
        hi = (lo + TQ_HALF) % BIAS_LANES
        window = jnp.concatenate([base[:, lo:lo + TQ_HALF], base[:, hi:hi + TQ_HALF]], axis=1)
        if first != lo:
            window = pltpu.roll(window, 2 * TQ_HALF - (first - lo), axis=1)
        dchunk = (BAND_PREV + r) // CHUNK - (k0 + kb) // CHUNK
        valid = (dchunk >= 0) & (dchunk < BAND_CHUNKS)
        out_ref[0, k0:k0 + SUBLANES, :] = jnp.where(valid, window[:, :TQ_HALF], NEG_BIG)


def _band_bias(rel_table):
    return pl.pallas_call(
        _band_bias_kernel,
        grid=(B_HEADS,),
        in_specs=[pl.BlockSpec(memory_space=pltpu.SMEM)],
        out_specs=pl.BlockSpec((1, BIAS_ROWS, TQ_HALF), lambda h: (h, 0, 0)),
        out_shape=jax.ShapeDtypeStruct((B_HEADS, BIAS_ROWS, TQ_HALF), F32),
        compiler_params=pltpu.CompilerParams(dimension_semantics=("arbitrary",)),
        name="band_bias",
    )(rel_table)


def _band_attn_kernel(qt_ref, k_ref, vt_ref, bias_ref, out_ref):
    j = pl.program_id(1)
    first_half = 2 * j - HALVES_BACK
    start = pl.multiple_of(jnp.maximum(first_half, 0) * TQ_HALF, TQ_HALF)
    pair_width = 2 * B_HEAD_DIM
    pair_row = lax.broadcasted_iota(jnp.int32, (pair_width, TQ_BAND), 0)
    zeros_half = jnp.zeros((TQ_HALF, TQ_HALF), BF16)
    masked_half = jnp.full((TQ_HALF, TQ_HALF), NEG_BIG, F32)

    def scores(pair):
        lo = pair * pair_width
        k_pair = k_ref[0, pl.ds(start, TK_BAND), lo:lo + pair_width]
        q_pair = qt_ref[0, 0, lo:lo + pair_width, :]
        zero = jnp.zeros_like(q_pair)
        q_sel = jnp.concatenate([jnp.where(pair_row < B_HEAD_DIM, q_pair, zero),
                                 jnp.where(pair_row >= B_HEAD_DIM, q_pair, zero)], axis=1)
        return _dot(k_pair, q_sel)

    def softmax_half(t, row0, c0):
        blocks = [t[row0 + r:row0 + r + SOFTMAX_ROWS, c0:c0 + TQ_HALF] for r in range(0, TK_HALF, SOFTMAX_ROWS)]
        m_acc = blocks[0]
        for blk in blocks[1:]:
            m_acc = jnp.maximum(m_acc, blk)
        m = jnp.max(m_acc, axis=0, keepdims=True)
        ps = []
        l_acc = None
        for blk in blocks:
            e = jnp.exp(blk - m)
            ps.append(e.astype(BF16))
            l_acc = e if l_acc is None else l_acc + e
        return jnp.concatenate(ps, axis=0), jnp.sum(l_acc, axis=0, keepdims=True)

    def tile(off_b, bias_row_a, bias_row_b):
        def bias_tile(pair):
            cols = []
            for h in (2 * pair, 2 * pair + 1):
                bias_a = bias_ref[h, pl.ds(bias_row_a, TK_HALF), :]
                bias_b = bias_ref[h, pl.ds(bias_row_b, TK_HALF), :]
                cols.append(jnp.concatenate([bias_a, masked_half], axis=0))
                cols.append(jnp.concatenate([masked_half, bias_b] if off_b else [bias_b, masked_half], axis=0))
            return jnp.concatenate(cols, axis=1)

        n_pairs = B_HEADS // 2
        live = [dict() for _ in range(n_pairs)]

        def biased_scores(pair):
            live[pair]["t"] = scores(pair) + bias_tile(pair)

        def softmax(pair):
            t = live[pair].pop("t")
            cols, sums = [], []
            for parity in range(2):
                c0 = parity * TQ_BAND
                p_a, l_a = softmax_half(t, 0, c0)
                p_b, l_b = softmax_half(t, off_b, c0 + TQ_HALF)
                cols.append(jnp.concatenate([p_a, zeros_half], axis=0))
                cols.append(jnp.concatenate([zeros_half, p_b] if off_b else [p_b, zeros_half], axis=0))
                sums += [l_a, l_b]
            live[pair]["p"] = jnp.concatenate(cols, axis=1)
            live[pair]["inv"] = 1.0 / jnp.concatenate(sums, axis=1)

        def values(pair):
            lo = pair * pair_width
            vt = vt_ref[0, lo:lo + pair_width, pl.ds(start, TK_BAND)]
            o = _dot(vt, live[pair].pop("p"))
            inv = live[pair].pop("inv")
            out_ref[0, 0, lo:lo + B_HEAD_DIM, :] = (o[:B_HEAD_DIM, :TQ_BAND] * inv[:, :TQ_BAND]).astype(BF16)
            out_ref[0, 0, lo + B_HEAD_DIM:lo + pair_width, :] = (
                o[B_HEAD_DIM:, TQ_BAND:] * inv[:, TQ_BAND:]).astype(BF16)

        stages = (biased_scores, softmax, values)
        for slot in range(n_pairs + len(stages) - 1):
            for depth in (0, 2, 1):
                if 0 <= slot - depth < n_pairs:
                    stages[depth](slot - depth)

    pl.when(first_half >= 0)(functools.partial(tile, TQ_HALF, 0, 0))

    @pl.when(first_half < 0)
    def _():
        row_a = pl.multiple_of(-first_half * TQ_HALF, TQ_HALF)
        tile(0, row_a, pl.multiple_of(row_a - TQ_HALF, TQ_HALF))


def _band_attn(qt, k, vt, bias):
    bsz, seq, _ = k.shape
    return pl.pallas_call(
        _band_attn_kernel,
        grid=(bsz, seq // TQ_BAND),
        in_specs=[
            pl.BlockSpec((1, 1, B_WIDTH, TQ_BAND), lambda b, j: (b, j, 0, 0)),
            pl.BlockSpec((1, seq, B_WIDTH), lambda b, j: (b, 0, 0)),
            pl.BlockSpec((1, B_WIDTH, seq), lambda b, j: (b, 0, 0)),
            _resident(bias.shape),
        ],
        out_specs=pl.BlockSpec((1, 1, B_WIDTH, TQ_BAND), lambda b, j: (b, j, 0, 0)),
        out_shape=jax.ShapeDtypeStruct((bsz, seq // TQ_BAND, B_WIDTH, TQ_BAND), BF16),
        compiler_params=pltpu.CompilerParams(
            dimension_semantics=("arbitrary", "arbitrary"), vmem_limit_bytes=VMEM_LIMIT),
        name="band_attn",
    )(qt, k, vt, bias)


def kernel(x, mem, norm_mix_g, norm_ffn_g, a_w_in, a_gate_b, a_conv_w, a_conv_b, a_head_g, a_w_out, kv_norm_g, w_kv,
           b_w_in, b_rel_bias, b_w_out, mem_w_kv, ffn_w_up, ffn_conv_w, ffn_conv_b, ffn_w_down, final_g):
    bsz, seq, d = x.shape
    assert d == D_MODEL and seq % max(TM_IN, L_MLSTM, TM_POST, TQ_BAND) == 0
    assert a_w_in.shape[0] == 1 and b_w_in.shape[0] == 1, "one mLSTM layer followed by one band-attention layer"
    assert TM_IN % L_MLSTM == 0, "the input projection emits per-chunk decay terms: whole chunks per tile"
    assert bsz % MLSTM_BATCH == 0
    row = lambda g: g.reshape(1, -1).astype(F32)

    mkbdt, mvbd = _mem_kv(mem, mem_w_kv)

    wqk, wvot, wgm, wk_b, wvt_b, wqt_b, wqmem_b = _prep_weights(a_w_in, w_kv, b_w_in)

    gate_pad = ((0, 0), (0, GATE_LANES - A_HEADS))
    gate_b = jnp.pad(a_gate_b[0].astype(F32).reshape(2, A_HEADS), gate_pad).reshape(1, 2 * GATE_LANES)
    vone = (jnp.arange(A_HEADS * V_AUG) % V_AUG == A_HEAD_DIM).astype(F32).reshape(A_HEADS * V_AUG, 1)
    q, k, vt, ot_half, gate_rows, a_cols, decay, qmem = _a_in(
        x, row(norm_mix_g[0]), wqk, wvot, wgm, gate_b, vone, a_conv_w[0].astype(F32), row(a_conv_b[0]))

    head_g_half = 0.5 * a_head_g[0].reshape(A_HEADS, A_HEAD_DIM, 1).astype(F32)
    mix = _mlstm(q, k, vt, ot_half, gate_rows, a_cols, decay, head_g_half)

    wup_all = ffn_w_up.astype(BF16)
    wd_all = ffn_w_down.astype(BF16)

    def ffn_small(l):
        return row(norm_ffn_g[l]), ffn_conv_w[l].astype(F32), row(ffn_conv_b[l])

    tail = (row(kv_norm_g), wk_b, wvt_b, row(norm_mix_g[1]), wqt_b, wqmem_b)
    x1, kb, vbt, qbt, qmem_b = _post_mixer(
        x, mix, qmem, mkbdt, mvbd, 0, wup_all, wd_all, a_w_out[0].astype(BF16), *ffn_small(0), tail=tail, last=False)

    bias = _band_bias(b_rel_bias[0].astype(F32))
    mix_b = _band_attn(qbt, kb, vbt, bias)
    return _post_mixer(
        x1, mix_b, qmem_b, mkbdt, mvbd, 1, wup_all, wd_all, b_w_out[0].astype(BF16), *ffn_small(1),
        tail=(row(final_g),), last=True)
```

```python
import functools
import math

import jax
import jax.numpy as jnp
from jax import lax
from jax.experimental import pallas as pl
from jax.experimental.pallas import tpu as pltpu

F32 = jnp.float32
BF16 = jnp.bfloat16

LANES = 128
SUBLANES = 8
BF16_SUBLANES = 16
VMEM_BYTES = 64 * 1024 * 1024
VMEM_LIMIT = VMEM_BYTES // 8 * 7

D_MODEL = 1024
CHUNK = 64
MEM_TOKENS = 256
MEM_HEADS = 4
MEM_WIDTH = 256
MEM_HEAD_DIM = MEM_WIDTH // MEM_HEADS
A_WIDTH = 768
A_HEADS = 4
A_HEAD_DIM = A_WIDTH // A_HEADS
A_CONV = 4
B_HEADS = 12
B_HEAD_DIM = 64
B_WIDTH = B_HEADS * B_HEAD_DIM
BAND_CHUNKS = 9
MAX_REL = 128
REL_SIZE = MAX_REL + CHUNK
D_FF = 2816
FFN_CONV = 3
EPS = 1e-6

V_AUG = -(-(A_HEAD_DIM + 1) // BF16_SUBLANES) * BF16_SUBLANES
GATE_LANES = LANES
GATE_TERMS = 4
TERM_G, TERM_INTER, TERM_EN, TERM_WEXP = range(GATE_TERMS)
HALO = SUBLANES
NEG_BIG = -1e30

TM_IN = 512
L_MLSTM = 256
MLSTM_BATCH = 4
TM_POST = 512
FF_CHUNK = 256
TQ_HALF = LANES
TQ_BAND = 2 * TQ_HALF
BAND_PREV = (BAND_CHUNKS - 1) * CHUNK
TK_HALF = BAND_PREV + TQ_HALF
TK_BAND = BAND_PREV + TQ_BAND
HALVES_BACK = BAND_PREV // TQ_HALF
BIAS_ROWS = TK_HALF + BAND_PREV
BIAS_LANES = BIAS_ROWS + TQ_HALF
SOFTMAX_ROWS = 8 * SUBLANES


def _resident(shape):
    nd = len(shape)
    return pl.BlockSpec(shape, lambda *_: (0,) * nd, pipeline_mode=pl.Buffered(1))


def _rms(x, g):
    return x * lax.rsqrt(jnp.mean(x * x, axis=-1, keepdims=True) + EPS) * g


def _dot(a, b):
    return jnp.dot(a, b, preferred_element_type=F32)


def _dot_nt(a, b):
    return lax.dot_general(a, b, (((1,), (1,)), ((), ())), preferred_element_type=F32)


def _dot_tn(a, b):
    return lax.dot_general(a, b, (((0,), (0,)), ((), ())), preferred_element_type=F32)


def _silu(x):
    half = 0.5 * x
    return half + half * jnp.tanh(half)


def _mem_kv_kernel(mem_ref, w_ref, mkbdt_ref, mvbd_ref):
    m = mem_ref[0].astype(BF16)
    w = w_ref[0].astype(BF16)
    mkt = _dot(m, w[:, :MEM_WIDTH]).T
    mv = _dot(m, w[:, MEM_WIDTH:])
    f_idx = lax.broadcasted_iota(jnp.int32, (MEM_WIDTH, MEM_HEADS * MEM_TOKENS), 0)
    c_idx = lax.broadcasted_iota(jnp.int32, (MEM_WIDTH, MEM_HEADS * MEM_TOKENS), 1)
    mkt4 = jnp.concatenate([mkt] * MEM_HEADS, axis=1)
    mkbdt_ref[0, 0] = jnp.where(f_idx // MEM_HEAD_DIM == c_idx // MEM_TOKENS, mkt4, 0.0).astype(BF16)
    r_idx = lax.broadcasted_iota(jnp.int32, (MEM_HEADS * MEM_TOKENS, MEM_WIDTH), 0)
    g_idx = lax.broadcasted_iota(jnp.int32, (MEM_HEADS * MEM_TOKENS, MEM_WIDTH), 1)
    mv4 = jnp.concatenate([mv] * MEM_HEADS, axis=0)
    mvbd_ref[0, 0] = jnp.where(r_idx // MEM_TOKENS == g_idx // MEM_HEAD_DIM, mv4, 0.0).astype(BF16)


def _mem_kv(mem, mem_w_kv):
    depth = mem_w_kv.shape[0]
    bsz = mem.shape[0]
    return pl.pallas_call(
        _mem_kv_kernel,
        grid=(depth, bsz),
        in_specs=[
            pl.BlockSpec((1, MEM_TOKENS, D_MODEL), lambda l, b: (b, 0, 0)),
            pl.BlockSpec((1, D_MODEL, 2 * MEM_WIDTH), lambda l, b: (l, 0, 0)),
        ],
        out_specs=[
            pl.BlockSpec((1, 1, MEM_WIDTH, MEM_HEADS * MEM_TOKENS), lambda l, b: (l, b, 0, 0)),
            pl.BlockSpec((1, 1, MEM_HEADS * MEM_TOKENS, MEM_WIDTH), lambda l, b: (l, b, 0, 0)),
        ],
        out_shape=[
            jax.ShapeDtypeStruct((depth, bsz, MEM_WIDTH, MEM_HEADS * MEM_TOKENS), BF16),
            jax.ShapeDtypeStruct((depth, bsz, MEM_HEADS * MEM_TOKENS, MEM_WIDTH), BF16),
        ],
        compiler_params=pltpu.CompilerParams(dimension_semantics=("arbitrary", "arbitrary")),
        name="mem_kv",
    )(mem, mem_w_kv)


def _prep_weights_kernel(at_ref, kv_ref, b_ref, wqk_ref, wvot_ref, wgm_ref, wk_ref, wvtb_ref, wqt_ref, wqmem_ref):
    wqk_ref[...] = at_ref[0, :2 * A_WIDTH, :].T.astype(BF16)
    g0 = 4 * A_WIDTH
    gates = at_ref[0, g0:g0 + SUBLANES, :]
    sub = lax.broadcasted_iota(jnp.int32, gates.shape, 0)
    gap = jnp.zeros((GATE_LANES - SUBLANES, D_MODEL), F32)
    wgm_t = jnp.concatenate([
        jnp.where(sub < A_HEADS, gates, 0.0), gap,
        jnp.where(sub < A_HEADS, pltpu.roll(gates, SUBLANES - A_HEADS, axis=0), 0.0), gap,
        MEM_HEAD_DIM ** -0.5 * at_ref[0, g0 + 2 * A_HEADS:, :]], axis=0)
    wgm_ref[...] = wgm_t.T.astype(BF16)
    pad = jnp.zeros((V_AUG - A_HEAD_DIM, D_MODEL), BF16)
    for h in range(A_HEADS):
        v0 = 2 * A_WIDTH + h * A_HEAD_DIM
        wvot_ref[h * V_AUG:h * V_AUG + A_HEAD_DIM, :] = at_ref[0, v0:v0 + A_HEAD_DIM, :].astype(BF16)
        wvot_ref[h * V_AUG + A_HEAD_DIM:(h + 1) * V_AUG, :] = pad
    wvot_ref[A_HEADS * V_AUG:, :] = (0.5 * at_ref[0, 3 * A_WIDTH:4 * A_WIDTH, :]).astype(BF16)
    wk_ref[...] = kv_ref[:, :B_WIDTH].astype(BF16)
    wvtb_ref[...] = kv_ref[:, B_WIDTH:].T.astype(BF16)
    wqt_ref[...] = (B_HEAD_DIM ** -0.5 * b_ref[0, :, :B_WIDTH]).T.astype(BF16)
    wqmem_ref[...] = (MEM_HEAD_DIM ** -0.5 * b_ref[0, :, B_WIDTH:]).astype(BF16)


def _prep_weights(a_w_in, w_kv, b_w_in):
    assert math.log2(MEM_HEAD_DIM ** -0.5).is_integer() and math.log2(B_HEAD_DIM ** -0.5).is_integer()
    assert 2 * A_HEADS == SUBLANES, "the gate rows of the transposed projection fill one sublane tile"
    a_w_in = jnp.swapaxes(a_w_in, 1, 2)
    once = lambda shape: pl.BlockSpec(shape, lambda i: (0,) * len(shape), pipeline_mode=pl.Buffered(1))
    out_shapes = [(D_MODEL, 2 * A_WIDTH), (A_HEADS * V_AUG + A_WIDTH, D_MODEL), (D_MODEL, 2 * GATE_LANES + MEM_WIDTH),
                  (D_MODEL, B_WIDTH), (B_WIDTH, D_MODEL), (B_WIDTH, D_MODEL), (D_MODEL, MEM_WIDTH)]
    return pl.pallas_call(
        _prep_weights_kernel,
        grid=(1,),
        in_specs=[once(a_w_in.shape), once(w_kv.shape), once(b_w_in.shape)],
        out_specs=[once(s) for s in out_shapes],
        out_shape=[jax.ShapeDtypeStruct(s, BF16) for s in out_shapes],
        compiler_params=pltpu.CompilerParams(dimension_semantics=("arbitrary",), vmem_limit_bytes=VMEM_LIMIT),
        name="prep_weights",
    )(a_w_in, w_kv, b_w_in)


def _causal_taps(x, prev, taps):
    n_taps = taps.shape[0]
    halo_row = lax.broadcasted_iota(jnp.int32, (HALO, x.shape[1]), 0)
    y = taps[n_taps - 1:n_taps, :] * x
    for back in range(1, n_taps):
        head = jnp.where(halo_row < back, pltpu.roll(prev, back, axis=0), pltpu.roll(x[:HALO, :], back, axis=0))
        shifted = jnp.concatenate([head, pltpu.roll(x, back, axis=0)[HALO:, :]], axis=0)
        y = y + taps[n_taps - 1 - back:n_taps - back, :] * shifted
    return y


def _lane_scan(x, op, identity):
    lane = lax.broadcasted_iota(jnp.int32, x.shape, 1)
    step = 1
    while step < x.shape[1]:
        x = op(x, jnp.where(lane >= step, pltpu.roll(x, step, axis=1), identity))
        step *= 2
    return x


def _a_in_kernel(x_ref, g_ref, wqk_ref, wvot_ref, wgm_ref, gate_b_ref, vone_ref, cw_ref, cb_ref,
                 q_ref, k_ref, vt_ref, ot_ref, rows_ref, acol_ref, decay_ref, qmem_ref, halo_ref, m_ref):
    L = L_MLSTM
    n_chunks = x_ref.shape[1] // L

    @pl.when(pl.program_id(1) == 0)
    def _():
        halo_ref[...] = jnp.zeros(halo_ref.shape, F32)
        m_ref[...] = jnp.zeros(m_ref.shape, F32)

    def projections(c):
        tok = slice(c * L, (c + 1) * L)
        hn = _rms(x_ref[0, tok, :], g_ref[...]).astype(BF16)

        gm = _dot(hn, wgm_ref[...])
        qmem_ref[0, tok, :] = gm[:, 2 * GATE_LANES:].astype(BF16)
        gi = (gm[:, :GATE_LANES] + gate_b_ref[:, :GATE_LANES]).T[0:SUBLANES, :]
        fg = (gm[:, GATE_LANES:2 * GATE_LANES] + gate_b_ref[:, GATE_LANES:]).T[0:SUBLANES, :]
        head_row = lax.broadcasted_iota(jnp.int32, (SUBLANES, L), 0) < A_HEADS
        logf = jnp.where(head_row, jnp.minimum(fg, 0.0) - jnp.log(1.0 + jnp.exp(-jnp.abs(fg))), 0.0)
        b = _lane_scan(logf, jnp.add, 0.0)
        a = gi - b
        m_prev = m_ref[:, 0:1]
        g = jnp.maximum(_lane_scan(a, jnp.maximum, -jnp.inf), m_prev)
        g_last = g[:, L - 1:L]
        m_ref[...] = jnp.broadcast_to(b[:, L - 1:L] + g_last, m_ref.shape)
        decay_ref[0, c] = jnp.broadcast_to(jnp.exp(m_prev - g_last), decay_ref.shape[2:])
        terms = [None] * GATE_TERMS
        terms[TERM_G], terms[TERM_INTER] = g, jnp.exp(m_prev - g)
        terms[TERM_EN], terms[TERM_WEXP] = jnp.exp(-(b + g)), jnp.exp(a - g_last)
        rows_ref[0, :, tok] = jnp.concatenate(terms, axis=0)
        acol_ref[0, tok, :] = jnp.concatenate([a, jnp.zeros((GATE_LANES - SUBLANES, L), F32)], axis=0).T

        raw = _dot(hn, wqk_ref[...])
        vo = _dot_nt(wvot_ref[...], hn)
        n_v = vt_ref.shape[2]
        vt_ref[0, c] = (vo[:n_v] + vone_ref[...]).astype(BF16)
        ot_ref[0, c] = vo[n_v:]
        return raw

    def conv(c, raw):
        tok = slice(c * L, (c + 1) * L)
        prev = halo_ref[...]
        halo_ref[...] = raw[L - HALO:, :]
        qk = _silu(_causal_taps(raw, prev, cw_ref[...]) + cb_ref[...])
        for h in range(A_HEADS):
            lo = h * A_HEAD_DIM
            q_ref[0, h, tok, :] = qk[:, lo:lo + A_HEAD_DIM].astype(BF16)
            k_ref[0, h, tok, :] = (qk[:, A_WIDTH + lo:A_WIDTH + lo + A_HEAD_DIM]
                                   * (A_HEAD_DIM ** -0.5)).astype(BF16)

    raws = {}
    for slot in range(n_chunks + 1):
        if slot < n_chunks:
            raws[slot] = projections(slot)
        if slot >= 1:
            conv(slot - 1, raws.pop(slot - 1))


def _a_in(x, g, wqk, wvot, wgm, gate_b, vone, cw, cb):
    bsz, seq, _ = x.shape
    tm = TM_IN
    head_spec = lambda width: pl.BlockSpec((1, A_HEADS, tm, width), lambda b, j: (b, 0, j, 0))
    row_spec = lambda width: pl.BlockSpec((1, tm, width), lambda b, j: (b, j, 0))
    col_spec = lambda width: pl.BlockSpec((1, width, tm), lambda b, j: (b, 0, j))
    tile_spec = lambda width: pl.BlockSpec((1, tm // L_MLSTM, width, L_MLSTM), lambda b, j: (b, j, 0, 0))
    weights = (g, wqk, wvot, wgm, gate_b, vone, cw, cb)
    return pl.pallas_call(
        _a_in_kernel,
        grid=(bsz, seq // tm),
        in_specs=[row_spec(D_MODEL)] + [_resident(w.shape) for w in weights],
        out_specs=[
            head_spec(A_HEAD_DIM), head_spec(A_HEAD_DIM), tile_spec(A_HEADS * V_AUG), tile_spec(A_WIDTH),
            col_spec(GATE_TERMS * SUBLANES),
            row_spec(GATE_LANES),
            pl.BlockSpec((1, tm // L_MLSTM, SUBLANES, GATE_LANES), lambda b, j: (b, j, 0, 0)),
            row_spec(MEM_WIDTH),
        ],
        out_shape=[
            jax.ShapeDtypeStruct((bsz, A_HEADS, seq, A_HEAD_DIM), BF16),
            jax.ShapeDtypeStruct((bsz, A_HEADS, seq, A_HEAD_DIM), BF16),
            jax.ShapeDtypeStruct((bsz, seq // L_MLSTM, A_HEADS * V_AUG, L_MLSTM), BF16),
            jax.ShapeDtypeStruct((bsz, seq // L_MLSTM, A_WIDTH, L_MLSTM), F32),
            jax.ShapeDtypeStruct((bsz, GATE_TERMS * SUBLANES, seq), F32),
            jax.ShapeDtypeStruct((bsz, seq, GATE_LANES), F32),
            jax.ShapeDtypeStruct((bsz, seq // L_MLSTM, SUBLANES, GATE_LANES), F32),
            jax.ShapeDtypeStruct((bsz, seq, MEM_WIDTH), BF16),
        ],
        scratch_shapes=[pltpu.VMEM((HALO, 2 * A_WIDTH), F32), pltpu.VMEM((SUBLANES, GATE_LANES), F32)],
        compiler_params=pltpu.CompilerParams(
            dimension_semantics=("arbitrary", "arbitrary"), vmem_limit_bytes=VMEM_LIMIT),
        name="a_in",
    )(x, *weights)


def _mlstm_kernel(q_ref, k_ref, vt_ref, ot_ref, rows_ref, acol_ref, decay_ref, hg_ref, out_ref, ct_ref):
    L = q_ref.shape[2]
    half = L // 2

    @pl.when(pl.program_id(1) == 0)
    def _():
        ct_ref[...] = jnp.zeros(ct_ref.shape, F32)

    upper = (lax.broadcasted_iota(jnp.int32, (half, half), 0)
             <= lax.broadcasted_iota(jnp.int32, (half, half), 1))
    zero_block = jnp.zeros((half, half), BF16)
    items = [(n, h) for n in range(q_ref.shape[0]) for h in range(A_HEADS)]
    live = [dict() for _ in items]

    def term(n, which, h):
        return rows_ref[n, SUBLANES * which + h:SUBLANES * which + h + 1, :]

    def scores(i):
        n, h = items[i]
        live[i]["st"] = _dot_nt(k_ref[n, h], q_ref[n, h])

    def decay_weight(i):
        n, h = items[i]
        st = live[i].pop("st")
        a_col = acol_ref[n, :, h:h + 1]
        g_row = term(n, TERM_G, h)
        p00 = jnp.exp(jnp.where(upper, a_col[:half] - g_row[:, :half], -jnp.inf))
        p01 = jnp.exp(a_col[:half] - g_row[:, half:])
        p11 = jnp.exp(jnp.where(upper, a_col[half:] - g_row[:, half:], -jnp.inf))
        live[i]["s"] = jnp.concatenate([
            jnp.concatenate([(st[:half, :half] * p00).astype(BF16), (st[:half, half:] * p01).astype(BF16)], axis=1),
            jnp.concatenate([zero_block, (st[half:, half:] * p11).astype(BF16)], axis=1)], axis=0)

    def numerator(i):
        n, h = items[i]
        vt = vt_ref[n, 0, h * V_AUG:(h + 1) * V_AUG, :]
        ct_prev = ct_ref[n, h]
        live[i]["num"] = (_dot(vt, live[i].pop("s"))
                          + _dot_nt(ct_prev.astype(BF16), q_ref[n, h]) * term(n, TERM_INTER, h))
        wvt = (vt.astype(F32) * term(n, TERM_WEXP, h)).astype(BF16)
        ct_ref[n, h] = decay_ref[n, 0, h:h + 1, 0:1] * ct_prev + _dot(wvt, k_ref[n, h])

    def normalise(i):
        n, h = items[i]
        numt = live[i].pop("num")
        den = numt[A_HEAD_DIM:A_HEAD_DIM + 1, :]
        r = 1.0 / jnp.maximum(jnp.abs(den), term(n, TERM_EN, h))
        body = numt[:A_HEAD_DIM, :]
        ssq = jnp.sum(body * body, axis=0, keepdims=True)
        factor = r * lax.rsqrt(r * r * ssq * (1.0 / A_HEAD_DIM) + EPS)
        lo = h * A_HEAD_DIM
        gate = hg_ref[h] + hg_ref[h] * jnp.tanh(ot_ref[n, 0, lo:lo + A_HEAD_DIM, :])
        out_ref[n, 0, lo:lo + A_HEAD_DIM, :] = (body * factor * gate).astype(BF16)

    stages = (scores, decay_weight, numerator, normalise)
    for slot in range(len(items) + len(stages) - 1):
        for depth, stage in enumerate(stages):
            if 0 <= slot - depth < len(items):
                stage(slot - depth)


def _mlstm(q, k, vt, ot_half, rows, a_cols, decay, head_g_half):
    bsz, nh, seq, dh = q.shape
    L = L_MLSTM
    nb = MLSTM_BATCH
    head_spec = pl.BlockSpec((nb, nh, L, dh), lambda b, j: (b, 0, j, 0))
    col_spec = lambda width: pl.BlockSpec((nb, width, L), lambda b, j: (b, 0, j))
    tile_spec = lambda width: pl.BlockSpec((nb, 1, width, L), lambda b, j: (b, j, 0, 0))
    return pl.pallas_call(
        _mlstm_kernel,
        grid=(bsz // nb, seq // L),
        in_specs=[
            head_spec, head_spec, tile_spec(nh * V_AUG), tile_spec(nh * dh), col_spec(rows.shape[1]),
            pl.BlockSpec((nb, L, GATE_LANES), lambda b, j: (b, j, 0)),
            pl.BlockSpec((nb, 1, SUBLANES, GATE_LANES), lambda b, j: (b, j, 0, 0)),
            _resident(head_g_half.shape),
        ],
        out_specs=tile_spec(nh * dh),
        out_shape=jax.ShapeDtypeStruct((bsz, seq // L, nh * dh, L), BF16),
        scratch_shapes=[pltpu.VMEM((nb, nh, V_AUG, dh), F32)],
        compiler_params=pltpu.CompilerParams(
            dimension_semantics=("arbitrary", "arbitrary"), vmem_limit_bytes=VMEM_LIMIT),
        name="mlstm",
    )(q, k, vt, ot_half, rows, a_cols, decay, head_g_half)


def _post_mixer_kernel(*refs, last):
    if last:
        (x_ref, mix_ref, qmem_ref, mkbdt_ref, mvbd_ref, wup_ref, wd_ref, wout_ref, ffn_g_ref, cw_ref, cb_ref,
         final_g_ref, out_ref, halo_ref, act_ref) = refs
    else:
        (x_ref, mix_ref, qmem_ref, mkbdt_ref, mvbd_ref, wup_ref, wd_ref, wout_ref, ffn_g_ref, cw_ref, cb_ref,
         kv_g_ref, wk_ref, wvt_ref, nxt_g_ref, wqt_ref, wqmem_ref,
         out_ref, k_ref, vt_ref, qnt_ref, qmemn_ref, halo_ref, act_ref) = refs
    tm = x_ref.shape[1]
    mix_width = mix_ref.shape[2]

    @pl.when(pl.program_id(1) == 0)
    def _():
        halo_ref[...] = jnp.zeros((HALO, D_FF), F32)

    s = _dot(qmem_ref[0], mkbdt_ref[0, 0])
    probs = []
    for h in range(MEM_HEADS):
        sh = s[:, h * MEM_TOKENS:(h + 1) * MEM_TOKENS]
        e = jnp.exp(sh - jnp.max(sh, axis=1, keepdims=True))
        probs.append((e * (1.0 / jnp.sum(e, axis=1, keepdims=True))).astype(BF16))
    mem_out = _dot(jnp.concatenate(probs, axis=1), mvbd_ref[0, 0])

    y = _dot(mem_out.astype(BF16), wout_ref[mix_width:, :])
    y = y + jnp.concatenate(
        [_dot_tn(mix_ref[0, i], wout_ref[:mix_width, :]) for i in range(mix_ref.shape[1])], axis=0)
    x1 = x_ref[0] + y

    hn = _rms(x1, ffn_g_ref[...]).astype(BF16)
    chunk_cols = [slice(c0, min(c0 + FF_CHUNK, D_FF)) for c0 in range(0, D_FF, FF_CHUNK)]
    n_chunks = len(chunk_cols)

    def up(c):
        cols = chunk_cols[c]
        gate_cols = slice(D_FF + cols.start, D_FF + cols.stop)
        return _dot(hn, wup_ref[0, :, cols]), _dot(hn, wup_ref[0, :, gate_cols])

    u, g = up(0)
    for c in range(n_chunks):
        cols = chunk_cols[c]
        nxt = up(c + 1) if c + 1 < n_chunks else None
        prev = halo_ref[:, cols]
        halo_ref[:, cols] = g[tm - HALO:, :]
        gc = _causal_taps(g, prev, cw_ref[:, cols]) + cb_ref[:, cols]
        act_ref[:, cols] = (_silu(gc) * u).astype(BF16)
        if nxt is not None:
            u, g = nxt
    x2 = x1 + _dot(act_ref[...], wd_ref[0])

    if last:
        out_ref[0] = _rms(x2, final_g_ref[...])
    else:
        out_ref[0] = x2
        hkv = _rms(x2, kv_g_ref[...]).astype(BF16)
        k_ref[0] = _dot(hkv, wk_ref[...]).astype(BF16)
        vt_ref[0] = _dot_nt(wvt_ref[...], hkv).astype(BF16)
        hq = _rms(x2, nxt_g_ref[...]).astype(BF16)
        qnt = _dot_nt(wqt_ref[...], hq).astype(BF16)
        for i in range(qnt_ref.shape[1]):
            qnt_ref[0, i] = qnt[:, i * TQ_BAND:(i + 1) * TQ_BAND]
        qmemn_ref[0] = _dot(hq, wqmem_ref[...]).astype(BF16)


def _post_mixer(x, mix, qmem, mkbdt, mvbd, layer, wup_all, wd_all, wout, ffn_g, cw, cb, tail, last):
    bsz, seq, _ = x.shape
    tm = TM_POST
    row_spec = lambda width: pl.BlockSpec((1, tm, width), lambda b, j: (b, j, 0))
    col_spec = lambda width: pl.BlockSpec((1, width, tm), lambda b, j: (b, 0, j))
    tile_spec = lambda width, tile: pl.BlockSpec((1, tm // tile, width, tile), lambda b, j: (b, j, 0, 0))
    mix_spec = tile_spec(mix.shape[2], mix.shape[3])
    of_layer = lambda w: pl.BlockSpec((1,) + w.shape[1:], lambda b, j: (layer, 0, 0), pipeline_mode=pl.Buffered(1))
    weights = (wout, ffn_g, cw, cb) + tuple(tail)
    in_specs = [
        row_spec(D_MODEL), mix_spec, row_spec(MEM_WIDTH),
        pl.BlockSpec((1, 1) + mkbdt.shape[2:], lambda b, j: (layer, b, 0, 0)),
        pl.BlockSpec((1, 1) + mvbd.shape[2:], lambda b, j: (layer, b, 0, 0)),
        of_layer(wup_all), of_layer(wd_all),
    ] + [_resident(w.shape) for w in weights]
    if last:
        out_specs = row_spec(D_MODEL)
        out_shape = jax.ShapeDtypeStruct((bsz, seq, D_MODEL), F32)
    else:
        out_specs = [row_spec(D_MODEL), row_spec(B_WIDTH), col_spec(B_WIDTH), tile_spec(B_WIDTH, TQ_BAND),
                     row_spec(MEM_WIDTH)]
        out_shape = [
            jax.ShapeDtypeStruct((bsz, seq, D_MODEL), F32),
            jax.ShapeDtypeStruct((bsz, seq, B_WIDTH), BF16),
            jax.ShapeDtypeStruct((bsz, B_WIDTH, seq), BF16),
            jax.ShapeDtypeStruct((bsz, seq // TQ_BAND, B_WIDTH, TQ_BAND), BF16),
            jax.ShapeDtypeStruct((bsz, seq, MEM_WIDTH), BF16),
        ]
    return pl.pallas_call(
        functools.partial(_post_mixer_kernel, last=last),
        grid=(bsz, seq // tm),
        in_specs=in_specs,
        out_specs=out_specs,
        out_shape=out_shape,
        scratch_shapes=[pltpu.VMEM((HALO, D_FF), F32), pltpu.VMEM((tm, D_FF), BF16)],
        compiler_params=pltpu.CompilerParams(
            dimension_semantics=("arbitrary", "arbitrary"), vmem_limit_bytes=VMEM_LIMIT),
        name="post_mixer_last" if last else "post_mixer",
    )(x, mix, qmem, mkbdt, mvbd, wup_all, wd_all, *weights)


def _band_bias_kernel(tbl_ref, out_ref):
    h = pl.program_id(0)
    b = lax.broadcasted_iota(jnp.int32, (SUBLANES, BIAS_LANES), 0)
    c = lax.broadcasted_iota(jnp.int32, (SUBLANES, BIAS_LANES), 1)
    r_signed = jnp.where(c < TQ_HALF, c, c - BIAS_LANES)
    idx = jnp.clip(BAND_PREV + r_signed - b, -(CHUNK - 1), MAX_REL) + (CHUNK - 1)

    def body(e, acc):
        return jnp.where(idx == e, tbl_ref[h, e], acc)

    base = lax.fori_loop(0, REL_SIZE, body, jnp.zeros((SUBLANES, BIAS_LANES), F32), unroll=8)
    kb = lax.broadcasted_iota(jnp.int32, (SUBLANES, TQ_HALF), 0)
    r = lax.broadcasted_iota(jnp.int32, (SUBLANES, TQ_HALF), 1)
    for k0 in range(0, BIAS_ROWS, SUBLANES):
        first = -k0 % BIAS_LANES
        lo = first // TQ_HALF * TQ_HALF
        hi = (lo + TQ_HALF) % BIAS_LANES
        window = jnp.concatenate([base[:, lo:lo + TQ_HALF], base[:, hi:hi + TQ_HALF]], axis=1)
        if first != lo:
            window = pltpu.roll(window, 2 * TQ_HALF - (first - lo), axis=1)
        dchunk = (BAND_PREV + r) // CHUNK - (k0 + kb) // CHUNK
        valid = (dchunk >= 0) & (dchunk < BAND_CHUNKS)
        out_ref[0, k0:k0 + SUBLANES, :] = jnp.where(valid, window[:, :TQ_HALF], NEG_BIG)


def _band_bias(rel_table):
    return pl.pallas_call(
        _band_bias_kernel,
        grid=(B_HEADS,),
        in_specs=[pl.BlockSpec(memory_space=pltpu.SMEM)],
        out_specs=pl.BlockSpec((1, BIAS_ROWS, TQ_HALF), lambda h: (h, 0, 0)),
        out_shape=jax.ShapeDtypeStruct((B_HEADS, BIAS_ROWS, TQ_HALF), F32),
        compiler_params=pltpu.CompilerParams(dimension_semantics=("arbitrary",)),
        name="band_bias",
    )(rel_table)


def _band_attn_kernel(qt_ref, k_ref, vt_ref, bias_ref, out_ref):
    j = pl.program_id(1)
    first_half = 2 * j - HALVES_BACK
    start = pl.multiple_of(jnp.maximum(first_half, 0) * TQ_HALF, TQ_HALF)
    pair_width = 2 * B_HEAD_DIM
    pair_row = lax.broadcasted_iota(jnp.int32, (pair_width, TQ_BAND), 0)
    zeros_half = jnp.zeros((TQ_HALF, TQ_HALF), BF16)
    masked_half = jnp.full((TQ_HALF, TQ_HALF), NEG_BIG, F32)

    def scores(pair):
        lo = pair * pair_width
        k_pair = k_ref[0, pl.ds(start, TK_BAND), lo:lo + pair_width]
        q_pair = qt_ref[0, 0, lo:lo + pair_width, :]
        zero = jnp.zeros_like(q_pair)
        q_sel = jnp.concatenate([jnp.where(pair_row < B_HEAD_DIM, q_pair, zero),
                                 jnp.where(pair_row >= B_HEAD_DIM, q_pair, zero)], axis=1)
        return _dot(k_pair, q_sel)

    def softmax_half(t, row0, c0):
        blocks = [t[row0 + r:row0 + r + SOFTMAX_ROWS, c0:c0 + TQ_HALF] for r in range(0, TK_HALF, SOFTMAX_ROWS)]
        m_acc = blocks[0]
        for blk in blocks[1:]:
            m_acc = jnp.maximum(m_acc, blk)
        m = jnp.max(m_acc, axis=0, keepdims=True)
        ps = []
        l_acc = None
        for blk in blocks:
            e = jnp.exp(blk - m)
            ps.append(e.astype(BF16))
            l_acc = e if l_acc is None else l_acc + e
        return jnp.concatenate(ps, axis=0), jnp.sum(l_acc, axis=0, keepdims=True)

    def tile(off_b, bias_row_a, bias_row_b):
        def bias_tile(pair):
            cols = []
            for h in (2 * pair, 2 * pair + 1):
                bias_a = bias_ref[h, pl.ds(bias_row_a, TK_HALF), :]
                bias_b = bias_ref[h, pl.ds(bias_row_b, TK_HALF), :]
                cols.append(jnp.concatenate([bias_a, masked_half], axis=0))
                cols.append(jnp.concatenate([masked_half, bias_b] if off_b else [bias_b, masked_half], axis=0))
            return jnp.concatenate(cols, axis=1)

        n_pairs = B_HEADS // 2
        live = [dict() for _ in range(n_pairs)]

        def biased_scores(pair):
            live[pair]["t"] = scores(pair) + bias_tile(pair)

        def softmax(pair):
            t = live[pair].pop("t")
            cols, sums = [], []
            for parity in range(2):
                c0 = parity * TQ_BAND
                p_a, l_a = softmax_half(t, 0, c0)
                p_b, l_b = softmax_half(t, off_b, c0 + TQ_HALF)
                cols.append(jnp.concatenate([p_a, zeros_half], axis=0))
                cols.append(jnp.concatenate([zeros_half, p_b] if off_b else [p_b, zeros_half], axis=0))
                sums += [l_a, l_b]
            live[pair]["p"] = jnp.concatenate(cols, axis=1)
            live[pair]["inv"] = 1.0 / jnp.concatenate(sums, axis=1)

        def values(pair):
            lo = pair * pair_width
            vt = vt_ref[0, lo:lo + pair_width, pl.ds(start, TK_BAND)]
            o = _dot(vt, live[pair].pop("p"))
            inv = live[pair].pop("inv")
            out_ref[0, 0, lo:lo + B_HEAD_DIM, :] = (o[:B_HEAD_DIM, :TQ_BAND] * inv[:, :TQ_BAND]).astype(BF16)
            out_ref[0, 0, lo + B_HEAD_DIM:lo + pair_width, :] = (
                o[B_HEAD_DIM:, TQ_BAND:] * inv[:, TQ_BAND:]).astype(BF16)

        stages = (biased_scores, softmax, values)
        for slot in range(n_pairs + len(stages) - 1):
            for depth in (0, 2, 1):
                if 0 <= slot - depth < n_pairs:
                    stages[depth](slot - depth)

    pl.when(first_half >= 0)(functools.partial(tile, TQ_HALF, 0, 0))

    @pl.when(first_half < 0)
    def _():
        row_a = pl.multiple_of(-first_half * TQ_HALF, TQ_HALF)
        tile(0, row_a, pl.multiple_of(row_a - TQ_HALF, TQ_HALF))


def _band_attn(qt, k, vt, bias):
    bsz, seq, _ = k.shape
    return pl.pallas_call(
        _band_attn_kernel,
        grid=(bsz, seq // TQ_BAND),
        in_specs=[
            pl.BlockSpec((1, 1, B_WIDTH, TQ_BAND), lambda b, j: (b, j, 0, 0)),
            pl.BlockSpec((1, seq, B_WIDTH), lambda b, j: (b, 0, 0)),
            pl.BlockSpec((1, B_WIDTH, seq), lambda b, j: (b, 0, 0)),
            _resident(bias.shape),
        ],
        out_specs=pl.BlockSpec((1, 1, B_WIDTH, TQ_BAND), lambda b, j: (b, j, 0, 0)),
        out_shape=jax.ShapeDtypeStruct((bsz, seq // TQ_BAND, B_WIDTH, TQ_BAND), BF16),
        compiler_params=pltpu.CompilerParams(
            dimension_semantics=("arbitrary", "arbitrary"), vmem_limit_bytes=VMEM_LIMIT),
        name="band_attn",
    )(qt, k, vt, bias)


def kernel(x, mem, norm_mix_g, norm_ffn_g, a_w_in, a_gate_b, a_conv_w, a_conv_b, a_head_g, a_w_out, kv_norm_g, w_kv,
           b_w_in, b_rel_bias, b_w_out, mem_w_kv, ffn_w_up, ffn_conv_w, ffn_conv_b, ffn_w_down, final_g):
    bsz, seq, d = x.shape
    assert d == D_MODEL and seq % max(TM_IN, L_MLSTM, TM_POST, TQ_BAND) == 0
    assert a_w_in.shape[0] == 1 and b_w_in.shape[0] == 1, "one mLSTM layer followed by one band-attention layer"
    assert TM_IN % L_MLSTM == 0, "the input projection emits per-chunk decay terms: whole chunks per tile"
    assert bsz % MLSTM_BATCH == 0
    row = lambda g: g.reshape(1, -1).astype(F32)

    mkbdt, mvbd = _mem_kv(mem, mem_w_kv)

    wqk, wvot, wgm, wk_b, wvt_b, wqt_b, wqmem_b = _prep_weights(a_w_in, w_kv, b_w_in)

    gate_pad = ((0, 0), (0, GATE_LANES - A_HEADS))
    gate_b = jnp.pad(a_gate_b[0].astype(F32).reshape(2, A_HEADS), gate_pad).reshape(1, 2 * GATE_LANES)
    vone = (jnp.arange(A_HEADS * V_AUG) % V_AUG == A_HEAD_DIM).astype(F32).reshape(A_HEADS * V_AUG, 1)
    q, k, vt, ot_half, gate_rows, a_cols, decay, qmem = _a_in(
        x, row(norm_mix_g[0]), wqk, wvot, wgm, gate_b, vone, a_conv_w[0].astype(F32), row(a_conv_b[0]))

    head_g_half = 0.5 * a_head_g[0].reshape(A_HEADS, A_HEAD_DIM, 1).astype(F32)
    mix = _mlstm(q, k, vt, ot_half, gate_rows, a_cols, decay, head_g_half)

    wup_all = ffn_w_up.astype(BF16)
    wd_all = ffn_w_down.astype(BF16)

    def ffn_small(l):
        return row(norm_ffn_g[l]), ffn_conv_w[l].astype(F32), row(ffn_conv_b[l])

    tail = (row(kv_norm_g), wk_b, wvt_b, row(norm_mix_g[1]), wqt_b, wqmem_b)
    x1, kb, vbt, qbt, qmem_b = _post_mixer(
        x, mix, qmem, mkbdt, mvbd, 0, wup_all, wd_all, a_w_out[0].astype(BF16), *ffn_small(0), tail=tail, last=False)

    bias = _band_bias(b_rel_bias[0].astype(F32))
    mix_b = _band_attn(qbt, kb, vbt, bias)
    return _post_mixer(
        x1, mix_b, qmem_b, mkbdt, mvbd, 1, wup_all, wd_all, b_w_out[0].astype(BF16), *ffn_small(1),
        tail=(row(final_g),), last=True)
```

```python
import functools
import math

import jax
import jax.numpy as jnp
from jax import lax
from jax.experimental import pallas as pl
from jax.experimental.pallas import tpu as pltpu

F32 = jnp.float32
BF16 = jnp.bfloat16

LANES = 128
SUBLANES = 8
BF16_SUBLANES = 16
VMEM_BYTES = 64 * 1024 * 1024
VMEM_LIMIT = VMEM_BYTES // 8 * 7

D_MODEL = 1024
CHUNK = 64
MEM_TOKENS = 256
MEM_HEADS = 4
MEM_WIDTH = 256
MEM_HEAD_DIM = MEM_WIDTH // MEM_HEADS
A_WIDTH = 768
A_HEADS = 4
A_HEAD_DIM = A_WIDTH // A_HEADS
A_CONV = 4
B_HEADS = 12
B_HEAD_DIM = 64
B_WIDTH = B_HEADS * B_HEAD_DIM
BAND_CHUNKS = 9
MAX_REL = 128
REL_SIZE = MAX_REL + CHUNK
D_FF = 2816
FFN_CONV = 3
EPS = 1e-6

V_AUG = -(-(A_HEAD_DIM + 1) // BF16_SUBLANES) * BF16_SUBLANES
GATE_LANES = LANES
GATE_TERMS = 4
TERM_G, TERM_INTER, TERM_EN, TERM_WEXP = range(GATE_TERMS)
HALO = SUBLANES
NEG_BIG = -1e30

TM_IN = 512
L_MLSTM = 256
MLSTM_BATCH = 8
TM_POST = 512
FF_CHUNK = 256
TQ_HALF = LANES
TQ_BAND = 2 * TQ_HALF
BAND_PREV = (BAND_CHUNKS - 1) * CHUNK
TK_HALF = BAND_PREV + TQ_HALF
TK_BAND = BAND_PREV + TQ_BAND
HALVES_BACK = BAND_PREV // TQ_HALF
BIAS_ROWS = TK_HALF + BAND_PREV
BIAS_LANES = BIAS_ROWS + TQ_HALF
SOFTMAX_ROWS = 8 * SUBLANES


def _resident(shape):
    nd = len(shape)
    return pl.BlockSpec(shape, lambda *_: (0,) * nd, pipeline_mode=pl.Buffered(1))


def _rms(x, g):
    return x * lax.rsqrt(jnp.mean(x * x, axis=-1, keepdims=True) + EPS) * g


def _dot(a, b):
    return jnp.dot(a, b, preferred_element_type=F32)


def _dot_nt(a, b):
    return lax.dot_general(a, b, (((1,), (1,)), ((), ())), preferred_element_type=F32)


def _dot_tn(a, b):
    return lax.dot_general(a, b, (((0,), (0,)), ((), ())), preferred_element_type=F32)


def _silu(x):
    half = 0.5 * x
    return half + half * jnp.tanh(half)


def _mem_kv_kernel(mem_ref, w_ref, mkbdt_ref, mvbd_ref):
    m = mem_ref[0].astype(BF16)
    w = w_ref[0].astype(BF16)
    mkt = _dot(m, w[:, :MEM_WIDTH]).T
    mv = _dot(m, w[:, MEM_WIDTH:])
    f_idx = lax.broadcasted_iota(jnp.int32, (MEM_WIDTH, MEM_HEADS * MEM_TOKENS), 0)
    c_idx = lax.broadcasted_iota(jnp.int32, (MEM_WIDTH, MEM_HEADS * MEM_TOKENS), 1)
    mkt4 = jnp.concatenate([mkt] * MEM_HEADS, axis=1)
    mkbdt_ref[0, 0] = jnp.where(f_idx // MEM_HEAD_DIM == c_idx // MEM_TOKENS, mkt4, 0.0).astype(BF16)
    r_idx = lax.broadcasted_iota(jnp.int32, (MEM_HEADS * MEM_TOKENS, MEM_WIDTH), 0)
    g_idx = lax.broadcasted_iota(jnp.int32, (MEM_HEADS * MEM_TOKENS, MEM_WIDTH), 1)
    mv4 = jnp.concatenate([mv] * MEM_HEADS, axis=0)
    mvbd_ref[0, 0] = jnp.where(r_idx // MEM_TOKENS == g_idx // MEM_HEAD_DIM, mv4, 0.0).astype(BF16)


def _mem_kv(mem, mem_w_kv):
    depth = mem_w_kv.shape[0]
    bsz = mem.shape[0]
    return pl.pallas_call(
        _mem_kv_kernel,
        grid=(depth, bsz),
        in_specs=[
            pl.BlockSpec((1, MEM_TOKENS, D_MODEL), lambda l, b: (b, 0, 0)),
            pl.BlockSpec((1, D_MODEL, 2 * MEM_WIDTH), lambda l, b: (l, 0, 0)),
        ],
        out_specs=[
            pl.BlockSpec((1, 1, MEM_WIDTH, MEM_HEADS * MEM_TOKENS), lambda l, b: (l, b, 0, 0)),
            pl.BlockSpec((1, 1, MEM_HEADS * MEM_TOKENS, MEM_WIDTH), lambda l, b: (l, b, 0, 0)),
        ],
        out_shape=[
            jax.ShapeDtypeStruct((depth, bsz, MEM_WIDTH, MEM_HEADS * MEM_TOKENS), BF16),
            jax.ShapeDtypeStruct((depth, bsz, MEM_HEADS * MEM_TOKENS, MEM_WIDTH), BF16),
        ],
        compiler_params=pltpu.CompilerParams(dimension_semantics=("arbitrary", "arbitrary")),
        name="mem_kv",
    )(mem, mem_w_kv)


def _prep_weights_kernel(at_ref, kv_ref, b_ref, wqk_ref, wvot_ref, wgm_ref, wk_ref, wvtb_ref, wqt_ref, wqmem_ref):
    wqk_ref[...] = at_ref[0, :2 * A_WIDTH, :].T.astype(BF16)
    g0 = 4 * A_WIDTH
    gates = at_ref[0, g0:g0 + SUBLANES, :]
    sub = lax.broadcasted_iota(jnp.int32, gates.shape, 0)
    gap = jnp.zeros((GATE_LANES - SUBLANES, D_MODEL), F32)
    wgm_t = jnp.concatenate([
        jnp.where(sub < A_HEADS, gates, 0.0), gap,
        jnp.where(sub < A_HEADS, pltpu.roll(gates, SUBLANES - A_HEADS, axis=0), 0.0), gap,
        MEM_HEAD_DIM ** -0.5 * at_ref[0, g0 + 2 * A_HEADS:, :]], axis=0)
    wgm_ref[...] = wgm_t.T.astype(BF16)
    pad = jnp.zeros((V_AUG - A_HEAD_DIM, D_MODEL), BF16)
    for h in range(A_HEADS):
        v0 = 2 * A_WIDTH + h * A_HEAD_DIM
        wvot_ref[h * V_AUG:h * V_AUG + A_HEAD_DIM, :] = at_ref[0, v0:v0 + A_HEAD_DIM, :].astype(BF16)
        wvot_ref[h * V_AUG + A_HEAD_DIM:(h + 1) * V_AUG, :] = pad
    wvot_ref[A_HEADS * V_AUG:, :] = (0.5 * at_ref[0, 3 * A_WIDTH:4 * A_WIDTH, :]).astype(BF16)
    wk_ref[...] = kv_ref[:, :B_WIDTH].astype(BF16)
    wvtb_ref[...] = kv_ref[:, B_WIDTH:].T.astype(BF16)
    wqt_ref[...] = (B_HEAD_DIM ** -0.5 * b_ref[0, :, :B_WIDTH]).T.astype(BF16)
    wqmem_ref[...] = (MEM_HEAD_DIM ** -0.5 * b_ref[0, :, B_WIDTH:]).astype(BF16)


def _prep_weights(a_w_in, w_kv, b_w_in):
    assert math.log2(MEM_HEAD_DIM ** -0.5).is_integer() and math.log2(B_HEAD_DIM ** -0.5).is_integer()
    assert 2 * A_HEADS == SUBLANES, "the gate rows of the transposed projection fill one sublane tile"
    a_w_in = jnp.swapaxes(a_w_in, 1, 2)
    once = lambda shape: pl.BlockSpec(shape, lambda i: (0,) * len(shape), pipeline_mode=pl.Buffered(1))
    out_shapes = [(D_MODEL, 2 * A_WIDTH), (A_HEADS * V_AUG + A_WIDTH, D_MODEL), (D_MODEL, 2 * GATE_LANES + MEM_WIDTH),
                  (D_MODEL, B_WIDTH), (B_WIDTH, D_MODEL), (B_WIDTH, D_MODEL), (D_MODEL, MEM_WIDTH)]
    return pl.pallas_call(
        _prep_weights_kernel,
        grid=(1,),
        in_specs=[once(a_w_in.shape), once(w_kv.shape), once(b_w_in.shape)],
        out_specs=[once(s) for s in out_shapes],
        out_shape=[jax.ShapeDtypeStruct(s, BF16) for s in out_shapes],
        compiler_params=pltpu.CompilerParams(dimension_semantics=("arbitrary",), vmem_limit_bytes=VMEM_LIMIT),
        name="prep_weights",
    )(a_w_in, w_kv, b_w_in)


def _causal_taps(x, prev, taps):
    n_taps = taps.shape[0]
    halo_row = lax.broadcasted_iota(jnp.int32, (HALO, x.shape[1]), 0)
    y = taps[n_taps - 1:n_taps, :] * x
    for back in range(1, n_taps):
        head = jnp.where(halo_row < back, pltpu.roll(prev, back, axis=0), pltpu.roll(x[:HALO, :], back, axis=0))
        shifted = jnp.concatenate([head, pltpu.roll(x, back, axis=0)[HALO:, :]], axis=0)
        y = y + taps[n_taps - 1 - back:n_taps - back, :] * shifted
    return y


def _lane_scan(x, op, identity):
    lane = lax.broadcasted_iota(jnp.int32, x.shape, 1)
    step = 1
    while step < x.shape[1]:
        x = op(x, jnp.where(lane >= step, pltpu.roll(x, step, axis=1), identity))
        step *= 2
    return x


def _a_in_kernel(x_ref, g_ref, wqk_ref, wvot_ref, wgm_ref, gate_b_ref, vone_ref, cw_ref, cb_ref,
                 q_ref, k_ref, vt_ref, ot_ref, rows_ref, acol_ref, decay_ref, qmem_ref, halo_ref, m_ref):
    L = L_MLSTM
    n_chunks = x_ref.shape[1] // L

    @pl.when(pl.program_id(1) == 0)
    def _():
        halo_ref[...] = jnp.zeros(halo_ref.shape, F32)
        m_ref[...] = jnp.zeros(m_ref.shape, F32)

    def projections(c):
        tok = slice(c * L, (c + 1) * L)
        hn = _rms(x_ref[0, tok, :], g_ref[...]).astype(BF16)

        gm = _dot(hn, wgm_ref[...])
        qmem_ref[0, tok, :] = gm[:, 2 * GATE_LANES:].astype(BF16)
        gi = (gm[:, :GATE_LANES] + gate_b_ref[:, :GATE_LANES]).T[0:SUBLANES, :]
        fg = (gm[:, GATE_LANES:2 * GATE_LANES] + gate_b_ref[:, GATE_LANES:]).T[0:SUBLANES, :]
        head_row = lax.broadcasted_iota(jnp.int32, (SUBLANES, L), 0) < A_HEADS
        logf = jnp.where(head_row, jnp.minimum(fg, 0.0) - jnp.log(1.0 + jnp.exp(-jnp.abs(fg))), 0.0)
        b = _lane_scan(logf, jnp.add, 0.0)
        a = gi - b
        m_prev = m_ref[:, 0:1]
        g = jnp.maximum(_lane_scan(a, jnp.maximum, -jnp.inf), m_prev)
        g_last = g[:, L - 1:L]
        m_ref[...] = jnp.broadcast_to(b[:, L - 1:L] + g_last, m_ref.shape)
        decay_ref[0, c] = jnp.broadcast_to(jnp.exp(m_prev - g_last), decay_ref.shape[2:])
        terms = [None] * GATE_TERMS
        terms[TERM_G], terms[TERM_INTER] = g, jnp.exp(m_prev - g)
        terms[TERM_EN], terms[TERM_WEXP] = jnp.exp(-(b + g)), jnp.exp(a - g_last)
        rows_ref[0, :, tok] = jnp.concatenate(terms, axis=0)
        acol_ref[0, tok, :] = jnp.concatenate([a, jnp.zeros((GATE_LANES - SUBLANES, L), F32)], axis=0).T

        raw = _dot(hn, wqk_ref[...])
        vo = _dot_nt(wvot_ref[...], hn)
        n_v = vt_ref.shape[2]
        vt_ref[0, c] = (vo[:n_v] + vone_ref[...]).astype(BF16)
        ot_ref[0, c] = vo[n_v:]
        return raw

    def conv(c, raw):
        tok = slice(c * L, (c + 1) * L)
        prev = halo_ref[...]
        halo_ref[...] = raw[L - HALO:, :]
        qk = _silu(_causal_taps(raw, prev, cw_ref[...]) + cb_ref[...])
        for h in range(A_HEADS):
            lo = h * A_HEAD_DIM
            q_ref[0, h, tok, :] = qk[:, lo:lo + A_HEAD_DIM].astype(BF16)
            k_ref[0, h, tok, :] = (qk[:, A_WIDTH + lo:A_WIDTH + lo + A_HEAD_DIM]
                                   * (A_HEAD_DIM ** -0.5)).astype(BF16)

    raws = {}
    for slot in range(n_chunks + 1):
        if slot < n_chunks:
            raws[slot] = projections(slot)
        if slot >= 1:
            conv(slot - 1, raws.pop(slot - 1))


def _a_in(x, g, wqk, wvot, wgm, gate_b, vone, cw, cb):
    bsz, seq, _ = x.shape
    tm = TM_IN
    head_spec = lambda width: pl.BlockSpec((1, A_HEADS, tm, width), lambda b, j: (b, 0, j, 0))
    row_spec = lambda width: pl.BlockSpec((1, tm, width), lambda b, j: (b, j, 0))
    col_spec = lambda width: pl.BlockSpec((1, width, tm), lambda b, j: (b, 0, j))
    tile_spec = lambda width: pl.BlockSpec((1, tm // L_MLSTM, width, L_MLSTM), lambda b, j: (b, j, 0, 0))
    weights = (g, wqk, wvot, wgm, gate_b, vone, cw, cb)
    return pl.pallas_call(
        _a_in_kernel,
        grid=(bsz, seq // tm),
        in_specs=[row_spec(D_MODEL)] + [_resident(w.shape) for w in weights],
        out_specs=[
            head_spec(A_HEAD_DIM), head_spec(A_HEAD_DIM), tile_spec(A_HEADS * V_AUG), tile_spec(A_WIDTH),
            col_spec(GATE_TERMS * SUBLANES),
            row_spec(GATE_LANES),
            pl.BlockSpec((1, tm // L_MLSTM, SUBLANES, GATE_LANES), lambda b, j: (b, j, 0, 0)),
            row_spec(MEM_WIDTH),
        ],
        out_shape=[
            jax.ShapeDtypeStruct((bsz, A_HEADS, seq, A_HEAD_DIM), BF16),
            jax.ShapeDtypeStruct((bsz, A_HEADS, seq, A_HEAD_DIM), BF16),
            jax.ShapeDtypeStruct((bsz, seq // L_MLSTM, A_HEADS * V_AUG, L_MLSTM), BF16),
            jax.ShapeDtypeStruct((bsz, seq // L_MLSTM, A_WIDTH, L_MLSTM), F32),
            jax.ShapeDtypeStruct((bsz, GATE_TERMS * SUBLANES, seq), F32),
            jax.ShapeDtypeStruct((bsz, seq, GATE_LANES), F32),
            jax.ShapeDtypeStruct((bsz, seq // L_MLSTM, SUBLANES, GATE_LANES), F32),
            jax.ShapeDtypeStruct((bsz, seq, MEM_WIDTH), BF16),
        ],
        scratch_shapes=[pltpu.VMEM((HALO, 2 * A_WIDTH), F32), pltpu.VMEM((SUBLANES, GATE_LANES), F32)],
        compiler_params=pltpu.CompilerParams(
            dimension_semantics=("arbitrary", "arbitrary"), vmem_limit_bytes=VMEM_LIMIT),
        name="a_in",
    )(x, *weights)


def _mlstm_kernel(q_ref, k_ref, vt_ref, ot_ref, rows_ref, acol_ref, decay_ref, hg_ref, out_ref, ct_ref):
    L = q_ref.shape[2]
    half = L // 2

    @pl.when(pl.program_id(1) == 0)
    def _():
        ct_ref[...] = jnp.zeros(ct_ref.shape, F32)

    upper = (lax.broadcasted_iota(jnp.int32, (half, half), 0)
             <= lax.broadcasted_iota(jnp.int32, (half, half), 1))
    zero_block = jnp.zeros((half, half), BF16)
    items = [(n, h) for n in range(q_ref.shape[0]) for h in range(A_HEADS)]
    live = [dict() for _ in items]

    def term(n, which, h):
        return rows_ref[n, SUBLANES * which + h:SUBLANES * which + h + 1, :]

    def scores(i):
        n, h = items[i]
        live[i]["st"] = _dot_nt(k_ref[n, h], q_ref[n, h])

    def decay_weight(i):
        n, h = items[i]
        st = live[i].pop("st")
        a_col = acol_ref[n, :, h:h + 1]
        g_row = term(n, TERM_G, h)
        p00 = jnp.exp(jnp.where(upper, a_col[:half] - g_row[:, :half], -jnp.inf))
        p01 = jnp.exp(a_col[:half] - g_row[:, half:])
        p11 = jnp.exp(jnp.where(upper, a_col[half:] - g_row[:, half:], -jnp.inf))
        live[i]["s"] = jnp.concatenate([
            jnp.concatenate([(st[:half, :half] * p00).astype(BF16), (st[:half, half:] * p01).astype(BF16)], axis=1),
            jnp.concatenate([zero_block, (st[half:, half:] * p11).astype(BF16)], axis=1)], axis=0)

    def numerator(i):
        n, h = items[i]
        vt = vt_ref[n, 0, h * V_AUG:(h + 1) * V_AUG, :]
        ct_prev = ct_ref[n, h]
        live[i]["num"] = (_dot(vt, live[i].pop("s"))
                          + _dot_nt(ct_prev.astype(BF16), q_ref[n, h]) * term(n, TERM_INTER, h))
        wvt = (vt.astype(F32) * term(n, TERM_WEXP, h)).astype(BF16)
        ct_ref[n, h] = decay_ref[n, 0, h:h + 1, 0:1] * ct_prev + _dot(wvt, k_ref[n, h])

    def normalise(i):
        n, h = items[i]
        numt = live[i].pop("num")
        den = numt[A_HEAD_DIM:A_HEAD_DIM + 1, :]
        r = 1.0 / jnp.maximum(jnp.abs(den), term(n, TERM_EN, h))
        body = numt[:A_HEAD_DIM, :]
        ssq = jnp.sum(body * body, axis=0, keepdims=True)
        factor = r * lax.rsqrt(r * r * ssq * (1.0 / A_HEAD_DIM) + EPS)
        lo = h * A_HEAD_DIM
        gate = hg_ref[h] + hg_ref[h] * jnp.tanh(ot_ref[n, 0, lo:lo + A_HEAD_DIM, :])
        out_ref[n, 0, lo:lo + A_HEAD_DIM, :] = (body * factor * gate).astype(BF16)

    stages = (scores, decay_weight, numerator, normalise)
    for slot in range(len(items) + len(stages) - 1):
        for depth, stage in enumerate(stages):
            if 0 <= slot - depth < len(items):
                stage(slot - depth)


def _mlstm(q, k, vt, ot_half, rows, a_cols, decay, head_g_half):
    bsz, nh, seq, dh = q.shape
    L = L_MLSTM
    nb = MLSTM_BATCH
    head_spec = pl.BlockSpec((nb, nh, L, dh), lambda b, j: (b, 0, j, 0))
    col_spec = lambda width: pl.BlockSpec((nb, width, L), lambda b, j: (b, 0, j))
    tile_spec = lambda width: pl.BlockSpec((nb, 1, width, L), lambda b, j: (b, j, 0, 0))
    return pl.pallas_call(
        _mlstm_kernel,
        grid=(bsz // nb, seq // L),
        in_specs=[
            head_spec, head_spec, tile_spec(nh * V_AUG), tile_spec(nh * dh), col_spec(rows.shape[1]),
            pl.BlockSpec((nb, L, GATE_LANES), lambda b, j: (b, j, 0)),
            pl.BlockSpec((nb, 1, SUBLANES, GATE_LANES), lambda b, j: (b, j, 0, 0)),
            _resident(head_g_half.shape),
        ],
        out_specs=tile_spec(nh * dh),
        out_shape=jax.ShapeDtypeStruct((bsz, seq // L, nh * dh, L), BF16),
        scratch_shapes=[pltpu.VMEM((nb, nh, V_AUG, dh), F32)],
        compiler_params=pltpu.CompilerParams(
            dimension_semantics=("arbitrary", "arbitrary"), vmem_limit_bytes=VMEM_LIMIT),
        name="mlstm",
    )(q, k, vt, ot_half, rows, a_cols, decay, head_g_half)


def _post_mixer_kernel(*refs, last):
    if last:
        (x_ref, mix_ref, qmem_ref, mkbdt_ref, mvbd_ref, wup_ref, wd_ref, wout_ref, ffn_g_ref, cw_ref, cb_ref,
         final_g_ref, out_ref, halo_ref, act_ref) = refs
    else:
        (x_ref, mix_ref, qmem_ref, mkbdt_ref, mvbd_ref, wup_ref, wd_ref, wout_ref, ffn_g_ref, cw_ref, cb_ref,
         kv_g_ref, wk_ref, wvt_ref, nxt_g_ref, wqt_ref, wqmem_ref,
         out_ref, k_ref, vt_ref, qnt_ref, qmemn_ref, halo_ref, act_ref) = refs
    tm = x_ref.shape[1]
    mix_width = mix_ref.shape[2]

    @pl.when(pl.program_id(1) == 0)
    def _():
        halo_ref[...] = jnp.zeros((HALO, D_FF), F32)

    s = _dot(qmem_ref[0], mkbdt_ref[0, 0])
    probs = []
    for h in range(MEM_HEADS):
        sh = s[:, h * MEM_TOKENS:(h + 1) * MEM_TOKENS]
        e = jnp.exp(sh - jnp.max(sh, axis=1, keepdims=True))
        probs.append((e * (1.0 / jnp.sum(e, axis=1, keepdims=True))).astype(BF16))
    mem_out = _dot(jnp.concatenate(probs, axis=1), mvbd_ref[0, 0])

    y = _dot(mem_out.astype(BF16), wout_ref[mix_width:, :])
    y = y + jnp.concatenate(
        [_dot_tn(mix_ref[0, i], wout_ref[:mix_width, :]) for i in range(mix_ref.shape[1])], axis=0)
    x1 = x_ref[0] + y

    hn = _rms(x1, ffn_g_ref[...]).astype(BF16)
    chunk_cols = [slice(c0, min(c0 + FF_CHUNK, D_FF)) for c0 in range(0, D_FF, FF_CHUNK)]
    n_chunks = len(chunk_cols)

    def up(c):
        cols = chunk_cols[c]
        gate_cols = slice(D_FF + cols.start, D_FF + cols.stop)
        return _dot(hn, wup_ref[0, :, cols]), _dot(hn, wup_ref[0, :, gate_cols])

    u, g = up(0)
    for c in range(n_chunks):
        cols = chunk_cols[c]
        nxt = up(c + 1) if c + 1 < n_chunks else None
        prev = halo_ref[:, cols]
        halo_ref[:, cols] = g[tm - HALO:, :]
        gc = _causal_taps(g, prev, cw_ref[:, cols]) + cb_ref[:, cols]
        act_ref[:, cols] = (_silu(gc) * u).astype(BF16)
        if nxt is not None:
            u, g = nxt
    x2 = x1 + _dot(act_ref[...], wd_ref[0])

    if last:
        out_ref[0] = _rms(x2, final_g_ref[...])
    else:
        out_ref[0] = x2
        hkv = _rms(x2, kv_g_ref[...]).astype(BF16)
        k_ref[0] = _dot(hkv, wk_ref[...]).astype(BF16)
        vt_ref[0] = _dot_nt(wvt_ref[...], hkv).astype(BF16)
        hq = _rms(x2, nxt_g_ref[...]).astype(BF16)
        qnt = _dot_nt(wqt_ref[...], hq).astype(BF16)
        for i in range(qnt_ref.shape[1]):
            qnt_ref[0, i] = qnt[:, i * TQ_BAND:(i + 1) * TQ_BAND]
        qmemn_ref[0] = _dot(hq, wqmem_ref[...]).astype(BF16)


def _post_mixer(x, mix, qmem, mkbdt, mvbd, layer, wup_all, wd_all, wout, ffn_g, cw, cb, tail, last):
    bsz, seq, _ = x.shape
    tm = TM_POST
    row_spec = lambda width: pl.BlockSpec((1, tm, width), lambda b, j: (b, j, 0))
    col_spec = lambda width: pl.BlockSpec((1, width, tm), lambda b, j: (b, 0, j))
    tile_spec = lambda width, tile: pl.BlockSpec((1, tm // tile, width, tile), lambda b, j: (b, j, 0, 0))
    mix_spec = tile_spec(mix.shape[2], mix.shape[3])
    of_layer = lambda w: pl.BlockSpec((1,) + w.shape[1:], lambda b, j: (layer, 0, 0), pipeline_mode=pl.Buffered(1))
    weights = (wout, ffn_g, cw, cb) + tuple(tail)
    in_specs = [
        row_spec(D_MODEL), mix_spec, row_spec(MEM_WIDTH),
        pl.BlockSpec((1, 1) + mkbdt.shape[2:], lambda b, j: (layer, b, 0, 0)),
        pl.BlockSpec((1, 1) + mvbd.shape[2:], lambda b, j: (layer, b, 0, 0)),
        of_layer(wup_all), of_layer(wd_all),
    ] + [_resident(w.shape) for w in weights]
    if last:
        out_specs = row_spec(D_MODEL)
        out_shape = jax.ShapeDtypeStruct((bsz, seq, D_MODEL), F32)
    else:
        out_specs = [row_spec(D_MODEL), row_spec(B_WIDTH), col_spec(B_WIDTH), tile_spec(B_WIDTH, TQ_BAND),
                     row_spec(MEM_WIDTH)]
        out_shape = [
            jax.ShapeDtypeStruct((bsz, seq, D_MODEL), F32),
            jax.ShapeDtypeStruct((bsz, seq, B_WIDTH), BF16),
            jax.ShapeDtypeStruct((bsz, B_WIDTH, seq), BF16),
            jax.ShapeDtypeStruct((bsz, seq // TQ_BAND, B_WIDTH, TQ_BAND), BF16),
            jax.ShapeDtypeStruct((bsz, seq, MEM_WIDTH), BF16),
        ]
    return pl.pallas_call(
        functools.partial(_post_mixer_kernel, last=last),
        grid=(bsz, seq // tm),
        in_specs=in_specs,
        out_specs=out_specs,
        out_shape=out_shape,
        scratch_shapes=[pltpu.VMEM((HALO, D_FF), F32), pltpu.VMEM((tm, D_FF), BF16)],
        compiler_params=pltpu.CompilerParams(
            dimension_semantics=("arbitrary", "arbitrary"), vmem_limit_bytes=VMEM_LIMIT),
        name="post_mixer_last" if last else "post_mixer",
    )(x, mix, qmem, mkbdt, mvbd, wup_all, wd_all, *weights)


def _band_bias_kernel(tbl_ref, out_ref):
    h = pl.program_id(0)
    b = lax.broadcasted_iota(jnp.int32, (SUBLANES, BIAS_LANES), 0)
    c = lax.broadcasted_iota(jnp.int32, (SUBLANES, BIAS_LANES), 1)
    r_signed = jnp.where(c < TQ_HALF, c, c - BIAS_LANES)
    idx = jnp.clip(BAND_PREV + r_signed - b, -(CHUNK - 1), MAX_REL) + (CHUNK - 1)

    def body(e, acc):
        return jnp.where(idx == e, tbl_ref[h, e], acc)

    base = lax.fori_loop(0, REL_SIZE, body, jnp.zeros((SUBLANES, BIAS_LANES), F32), unroll=8)
    kb = lax.broadcasted_iota(jnp.int32, (SUBLANES, TQ_HALF), 0)
    r = lax.broadcasted_iota(jnp.int32, (SUBLANES, TQ_HALF), 1)
    for k0 in range(0, BIAS_ROWS, SUBLANES):
        first = -k0 % BIAS_LANES
        lo = first // TQ_HALF * TQ_HALF
        hi = (lo + TQ_HALF) % BIAS_LANES
        window = jnp.concatenate([base[:, lo:lo + TQ_HALF], base[:, hi:hi + TQ_HALF]], axis=1)
        if first != lo:
            window = pltpu.roll(window, 2 * TQ_HALF - (first - lo), axis=1)
        dchunk = (BAND_PREV + r) // CHUNK - (k0 + kb) // CHUNK
        valid = (dchunk >= 0) & (dchunk < BAND_CHUNKS)
        out_ref[0, k0:k0 + SUBLANES, :] = jnp.where(valid, window[:, :TQ_HALF], NEG_BIG)


def _band_bias(rel_table):
    return pl.pallas_call(
        _band_bias_kernel,
        grid=(B_HEADS,),
        in_specs=[pl.BlockSpec(memory_space=pltpu.SMEM)],
        out_specs=pl.BlockSpec((1, BIAS_ROWS, TQ_HALF), lambda h: (h, 0, 0)),
        out_shape=jax.ShapeDtypeStruct((B_HEADS, BIAS_ROWS, TQ_HALF), F32),
        compiler_params=pltpu.CompilerParams(dimension_semantics=("arbitrary",)),
        name="band_bias",
    )(rel_table)


def _band_attn_kernel(qt_ref, k_ref, vt_ref, bias_ref, out_ref):
    j = pl.program_id(1)
    first_half = 2 * j - HALVES_BACK
    start = pl.multiple_of(jnp.maximum(first_half, 0) * TQ_HALF, TQ_HALF)
    pair_width = 2 * B_HEAD_DIM
    pair_row = lax.broadcasted_iota(jnp.int32, (pair_width, TQ_BAND), 0)
    zeros_half = jnp.zeros((TQ_HALF, TQ_HALF), BF16)
    masked_half = jnp.full((TQ_HALF, TQ_HALF), NEG_BIG, F32)

    def scores(pair):
        lo = pair * pair_width
        k_pair = k_ref[0, pl.ds(start, TK_BAND), lo:lo + pair_width]
        q_pair = qt_ref[0, 0, lo:lo + pair_width, :]
        zero = jnp.zeros_like(q_pair)
        q_sel = jnp.concatenate([jnp.where(pair_row < B_HEAD_DIM, q_pair, zero),
                                 jnp.where(pair_row >= B_HEAD_DIM, q_pair, zero)], axis=1)
        return _dot(k_pair, q_sel)

    def softmax_half(t, row0, c0):
        blocks = [t[row0 + r:row0 + r + SOFTMAX_ROWS, c0:c0 + TQ_HALF] for r in range(0, TK_HALF, SOFTMAX_ROWS)]
        m_acc = blocks[0]
        for blk in blocks[1:]:
            m_acc = jnp.maximum(m_acc, blk)
        m = jnp.max(m_acc, axis=0, keepdims=True)
        ps = []
        l_acc = None
        for blk in blocks:
            e = jnp.exp(blk - m)
            ps.append(e.astype(BF16))
            l_acc = e if l_acc is None else l_acc + e
        return jnp.concatenate(ps, axis=0), jnp.sum(l_acc, axis=0, keepdims=True)

    def tile(off_b, bias_row_a, bias_row_b):
        def bias_tile(pair):
            cols = []
            for h in (2 * pair, 2 * pair + 1):
                bias_a = bias_ref[h, pl.ds(bias_row_a, TK_HALF), :]
                bias_b = bias_ref[h, pl.ds(bias_row_b, TK_HALF), :]
                cols.append(jnp.concatenate([bias_a, masked_half], axis=0))
                cols.append(jnp.concatenate([masked_half, bias_b] if off_b else [bias_b, masked_half], axis=0))
            return jnp.concatenate(cols, axis=1)

        n_pairs = B_HEADS // 2
        live = [dict() for _ in range(n_pairs)]

        def biased_scores(pair):
            live[pair]["t"] = scores(pair) + bias_tile(pair)

        def softmax(pair):
            t = live[pair].pop("t")
            cols, sums = [], []
            for parity in range(2):
                c0 = parity * TQ_BAND
                p_a, l_a = softmax_half(t, 0, c0)
                p_b, l_b = softmax_half(t, off_b, c0 + TQ_HALF)
                cols.append(jnp.concatenate([p_a, zeros_half], axis=0))
                cols.append(jnp.concatenate([zeros_half, p_b] if off_b else [p_b, zeros_half], axis=0))
                sums += [l_a, l_b]
            live[pair]["p"] = jnp.concatenate(cols, axis=1)
            live[pair]["inv"] = 1.0 / jnp.concatenate(sums, axis=1)

        def values(pair):
            lo = pair * pair_width
            vt = vt_ref[0, lo:lo + pair_width, pl.ds(start, TK_BAND)]
            o = _dot(vt, live[pair].pop("p"))
            inv = live[pair].pop("inv")
            out_ref[0, 0, lo:lo + B_HEAD_DIM, :] = (o[:B_HEAD_DIM, :TQ_BAND] * inv[:, :TQ_BAND]).astype(BF16)
            out_ref[0, 0, lo + B_HEAD_DIM:lo + pair_width, :] = (
                o[B_HEAD_DIM:, TQ_BAND:] * inv[:, TQ_BAND:]).astype(BF16)

        stages = (biased_scores, softmax, values)
        for slot in range(n_pairs + len(stages) - 1):
            for depth in (0, 2, 1):
                if 0 <= slot - depth < n_pairs:
                    stages[depth](slot - depth)

    pl.when(first_half >= 0)(functools.partial(tile, TQ_HALF, 0, 0))

    @pl.when(first_half < 0)
    def _():
        row_a = pl.multiple_of(-first_half * TQ_HALF, TQ_HALF)
        tile(0, row_a, pl.multiple_of(row_a - TQ_HALF, TQ_HALF))


def _band_attn(qt, k, vt, bias):
    bsz, seq, _ = k.shape
    return pl.pallas_call(
        _band_attn_kernel,
        grid=(bsz, seq // TQ_BAND),
        in_specs=[
            pl.BlockSpec((1, 1, B_WIDTH, TQ_BAND), lambda b, j: (b, j, 0, 0)),
            pl.BlockSpec((1, seq, B_WIDTH), lambda b, j: (b, 0, 0)),
            pl.BlockSpec((1, B_WIDTH, seq), lambda b, j: (b, 0, 0)),
            _resident(bias.shape),
        ],
        out_specs=pl.BlockSpec((1, 1, B_WIDTH, TQ_BAND), lambda b, j: (b, j, 0, 0)),
        out_shape=jax.ShapeDtypeStruct((bsz, seq // TQ_BAND, B_WIDTH, TQ_BAND), BF16),
        compiler_params=pltpu.CompilerParams(
            dimension_semantics=("arbitrary", "arbitrary"), vmem_limit_bytes=VMEM_LIMIT),
        name="band_attn",
    )(qt, k, vt, bias)


def kernel(x, mem, norm_mix_g, norm_ffn_g, a_w_in, a_gate_b, a_conv_w, a_conv_b, a_head_g, a_w_out, kv_norm_g, w_kv,
           b_w_in, b_rel_bias, b_w_out, mem_w_kv, ffn_w_up, ffn_conv_w, ffn_conv_b, ffn_w_down, final_g):
    bsz, seq, d = x.shape
    assert d == D_MODEL and seq % max(TM_IN, L_MLSTM, TM_POST, TQ_BAND) == 0
    assert a_w_in.shape[0] == 1 and b_w_in.shape[0] == 1, "one mLSTM layer followed by one band-attention layer"
    assert TM_IN % L_MLSTM == 0, "the input projection emits per-chunk decay terms: whole chunks per tile"
    assert bsz % MLSTM_BATCH == 0
    row = lambda g: g.reshape(1, -1).astype(F32)

    mkbdt, mvbd = _mem_kv(mem, mem_w_kv)

    wqk, wvot, wgm, wk_b, wvt_b, wqt_b, wqmem_b = _prep_weights(a_w_in, w_kv, b_w_in)

    gate_pad = ((0, 0), (0, GATE_LANES - A_HEADS))
    gate_b = jnp.pad(a_gate_b[0].astype(F32).reshape(2, A_HEADS), gate_pad).reshape(1, 2 * GATE_LANES)
    vone = (jnp.arange(A_HEADS * V_AUG) % V_AUG == A_HEAD_DIM).astype(F32).reshape(A_HEADS * V_AUG, 1)
    q, k, vt, ot_half, gate_rows, a_cols, decay, qmem = _a_in(
        x, row(norm_mix_g[0]), wqk, wvot, wgm, gate_b, vone, a_conv_w[0].astype(F32), row(a_conv_b[0]))

    head_g_half = 0.5 * a_head_g[0].reshape(A_HEADS, A_HEAD_DIM, 1).astype(F32)
    mix = _mlstm(q, k, vt, ot_half, gate_rows, a_cols, decay, head_g_half)

    wup_all = ffn_w_up.astype(BF16)
    wd_all = ffn_w_down.astype(BF16)

    def ffn_small(l):
        return row(norm_ffn_g[l]), ffn_conv_w[l].astype(F32), row(ffn_conv_b[l])

    tail = (row(kv_norm_g), wk_b, wvt_b, row(norm_mix_g[1]), wqt_b, wqmem_b)
    x1, kb, vbt, qbt, qmem_b = _post_mixer(
        x, mix, qmem, mkbdt, mvbd, 0, wup_all, wd_all, a_w_out[0].astype(BF16), *ffn_small(0), tail=tail, last=False)

    bias = _band_bias(b_rel_bias[0].astype(F32))
    mix_b = _band_attn(qbt, kb, vbt, bias)
    return _post_mixer(
        x1, mix_b, qmem_b, mkbdt, mvbd, 1, wup_all, wd_all, b_w_out[0].astype(BF16), *ffn_small(1),
        tail=(row(final_g),), last=True)
```
